```python
import functools
import jax
import jax.numpy as jnp
from jax import lax
import numpy as np


D_MODEL = 4096
BATCH = 4
SEQ = 2048
DEPTH = 1

HEAD_DIM = 128
N_ATTN_HEADS = 16
ATTN_WIDTH = N_ATTN_HEADS * HEAD_DIM
MOBA_BLOCK = 256
MOBA_TOP_K = 3
MOBA_Q_SUB = 16
ROPE_THETA = 10000.0
CONV_DIM = 2048
CONV_WIDTH = 31
D_FF = 11008
FFN_CONV_WIDTH = 3
EPS = 1e-6
IN_WIDTH = 3 * ATTN_WIDTH + 2 * CONV_DIM + 2 * D_MODEL

kernel_name = "moba_conformer_gated_hybrid"


def rms_norm(x, g):
    xf = x.astype(jnp.float32)
    y = xf * lax.rsqrt(jnp.mean(xf * xf, axis=-1, keepdims=True) + EPS)
    return (y * g.astype(jnp.float32)).astype(x.dtype)


def layer_norm(x, g, b):
    xf = x.astype(jnp.float32)
    mu = jnp.mean(xf, axis=-1, keepdims=True)
    xc = xf - mu
    var = jnp.mean(xc * xc, axis=-1, keepdims=True)
    y = xc * lax.rsqrt(var + EPS) * g.astype(jnp.float32) + b.astype(jnp.float32)
    return y.astype(x.dtype)


def rope(x, positions):
    half = HEAD_DIM // 2
    inv_freq = ROPE_THETA ** (-jnp.arange(half, dtype=jnp.float32) / half)
    ang = positions.astype(jnp.float32)[..., None] * inv_freq
    cos = jnp.cos(ang)[:, None]
    sin = jnp.sin(ang)[:, None]
    xf = x.astype(jnp.float32)
    x1, x2 = xf[..., :half], xf[..., half:]
    out = jnp.concatenate([x1 * cos - x2 * sin, x2 * cos + x1 * sin], axis=-1)
    return out.astype(x.dtype)


def causal_depthwise_conv(x, w, b):
    width = w.shape[0]
    y = lax.conv_general_dilated(
        x, w[:, None, :].astype(x.dtype), window_strides=(1,),
        padding=[(width - 1, 0)], dimension_numbers=('NWC', 'WIO', 'NWC'),
        feature_group_count=x.shape[-1])
    return y + b.astype(x.dtype)


def gather_blocks(blocks, idx):
    return jax.vmap(jax.vmap(lambda kb, ix: kb[ix]))(blocks, idx)


def moba_attention(q, k, v):
    B, H, S, hd = q.shape
    L = MOBA_BLOCK
    nb = -(-S // L)
    pad = nb * L - S
    if pad:
        cfg = ((0, 0), (0, 0), (0, pad), (0, 0))
        q, k, v = jnp.pad(q, cfg), jnp.pad(k, cfg), jnp.pad(v, cfg)
    q = q * jnp.asarray(HEAD_DIM ** -0.5, q.dtype)
    kb = k.reshape(B, H, nb, L, hd)
    vb = v.reshape(B, H, nb, L, hd)
    k_mean = jnp.mean(kb.astype(jnp.float32), axis=3)
    causal_local = jnp.tril(jnp.ones((L, L), dtype=bool))
    n_sub = L // MOBA_Q_SUB
    neg = jnp.float32(-jnp.inf)
    outs = []
    for i in range(nb):
        qi = q[:, :, i * L:(i + 1) * L]
        ki, vi = kb[:, :, i], vb[:, :, i]
        n_sel = min(MOBA_TOP_K, i)
        if n_sel == 0:
            s = jnp.einsum('bhqd,bhkd->bhqk', qi, ki).astype(jnp.float32)
            p = jax.nn.softmax(jnp.where(causal_local, s, neg), axis=-1).astype(v.dtype)
            outs.append(jnp.einsum('bhqk,bhkd->bhqd', p, vi))
            continue
        gate = jnp.einsum('bhqd,bhnd->bhqn', qi.astype(jnp.float32), k_mean[:, :, :i])
        _, idx = lax.top_k(gate, n_sel)
        kp, vp = kb[:, :, :i], vb[:, :, :i]
        q_sub = jnp.moveaxis(qi.reshape(B, H, n_sub, MOBA_Q_SUB, hd), 2, 0)
        idx_sub = jnp.moveaxis(idx.reshape(B, H, n_sub, MOBA_Q_SUB, n_sel), 2, 0)
        mask_sub = causal_local.reshape(n_sub, MOBA_Q_SUB, L)

        def step(args, kp=kp, vp=vp, ki=ki, vi=vi, n_sel=n_sel):
            qc, ic, mc = args
            kg = gather_blocks(kp, ic)
            vg = gather_blocks(vp, ic)
            s_sel = jnp.einsum('bhqd,bhqnld->bhqnl', qc, kg).astype(jnp.float32)
            s_sel = s_sel.reshape(B, H, MOBA_Q_SUB, n_sel * L)
            s_loc = jnp.einsum('bhqd,bhld->bhql', qc, ki).astype(jnp.float32)
            s_loc = jnp.where(mc, s_loc, neg)
            p = jax.nn.softmax(jnp.concatenate([s_sel, s_loc], axis=-1), axis=-1).astype(vi.dtype)
            p_sel = p[..., :n_sel * L].reshape(B, H, MOBA_Q_SUB, n_sel, L)
            p_loc = p[..., n_sel * L:]
            return (jnp.einsum('bhqnl,bhqnld->bhqd', p_sel, vg)
                    + jnp.einsum('bhql,bhld->bhqd', p_loc, vi))

        o = lax.map(step, (q_sub, idx_sub, mask_sub))
        outs.append(jnp.moveaxis(o, 0, 2).reshape(B, H, L, hd))
    return jnp.concatenate(outs, axis=2)[:, :, :S]


def hybrid_mixer(xn, positions, w_in, q_norm_g, k_norm_g, w_o_attn, conv_w, conv_b,
                 conv_ln_g, conv_ln_b, w_o_conv, w_out):
    B, S, _ = xn.shape
    proj = xn @ w_in
    q, k, v, u, gates = jnp.split(
        proj, [ATTN_WIDTH, 2 * ATTN_WIDTH, 3 * ATTN_WIDTH, 3 * ATTN_WIDTH + 2 * CONV_DIM], axis=-1)

    def heads(t):
        return t.reshape(B, S, N_ATTN_HEADS, HEAD_DIM).transpose(0, 2, 1, 3)

    q = rope(rms_norm(heads(q), q_norm_g), positions)
    k = rope(rms_norm(heads(k), k_norm_g), positions)
    attn = moba_attention(q, k, heads(v))
    y_attn = attn.transpose(0, 2, 1, 3).reshape(B, S, ATTN_WIDTH) @ w_o_attn

    a, b = jnp.split(u, 2, axis=-1)
    h = a * jax.nn.sigmoid(b)
    h = causal_depthwise_conv(h, conv_w, conv_b)
    h = jax.nn.silu(layer_norm(h, conv_ln_g, conv_ln_b))
    y_conv = h @ w_o_conv

    g = jax.nn.sigmoid(gates.astype(jnp.float32)).astype(xn.dtype)
    g_attn, g_conv = jnp.split(g, 2, axis=-1)
    return (g_attn * y_attn + g_conv * y_conv) @ w_out


def conv_ffn(xn, w_ffn_in, ffn_conv_w, ffn_conv_b, w_ffn_out):
    h = causal_depthwise_conv(xn @ w_ffn_in, ffn_conv_w, ffn_conv_b)
    gate, up = jnp.split(h, 2, axis=-1)
    return (jax.nn.silu(gate) * up) @ w_ffn_out


def setup_inputs(seed: int = 0) -> dict:
    key = jax.random.key(seed)
    ks = jax.random.split(key, 20)
    f32 = jnp.float32

    def w(k, shape, fan_in):
        return jax.random.normal(k, shape, f32) * (fan_in ** -0.5)

    def gain(k, shape):
        return 1.0 + 0.01 * jax.random.normal(k, shape, f32)

    def bias(k, shape):
        return 0.01 * jax.random.normal(k, shape, f32)

    x = jax.random.normal(ks[0], (BATCH, SEQ, D_MODEL), f32)
    positions = jnp.broadcast_to(jnp.arange(SEQ, dtype=jnp.int32), (BATCH, SEQ))
    return {
        'x': x,
        'positions': positions,
        'norm1_g': gain(ks[1], (DEPTH, D_MODEL)),
        'w_in': w(ks[2], (DEPTH, D_MODEL, IN_WIDTH), D_MODEL),
        'q_norm_g': gain(ks[3], (DEPTH, HEAD_DIM)),
        'k_norm_g': gain(ks[4], (DEPTH, HEAD_DIM)),
        'w_o_attn': w(ks[5], (DEPTH, ATTN_WIDTH, D_MODEL), ATTN_WIDTH),
        'conv_w': w(ks[6], (DEPTH, CONV_WIDTH, CONV_DIM), CONV_WIDTH),
        'conv_b': bias(ks[7], (DEPTH, CONV_DIM)),
        'conv_ln_g': gain(ks[8], (DEPTH, CONV_DIM)),
        'conv_ln_b': bias(ks[9], (DEPTH, CONV_DIM)),
        'w_o_conv': w(ks[10], (DEPTH, CONV_DIM, D_MODEL), CONV_DIM),
        'w_out': w(ks[11], (DEPTH, D_MODEL, D_MODEL), D_MODEL),
        'norm2_g': gain(ks[12], (DEPTH, D_MODEL)),
        'w_ffn_in': w(ks[13], (DEPTH, D_MODEL, 2 * D_FF), D_MODEL),
        'ffn_conv_w': w(ks[14], (DEPTH, FFN_CONV_WIDTH, 2 * D_FF), FFN_CONV_WIDTH),
        'ffn_conv_b': bias(ks[15], (DEPTH, 2 * D_FF)),
        'w_ffn_out': w(ks[16], (DEPTH, D_FF, D_MODEL), D_FF),
    }


def reference(x, positions, norm1_g, w_in, q_norm_g, k_norm_g, w_o_attn, conv_w, conv_b,
              conv_ln_g, conv_ln_b, w_o_conv, w_out, norm2_g, w_ffn_in, ffn_conv_w,
              ffn_conv_b, w_ffn_out):
    for l in range(DEPTH):
        xn = rms_norm(x, norm1_g[l])
        x = x + hybrid_mixer(xn, positions, w_in[l], q_norm_g[l], k_norm_g[l], w_o_attn[l],
                             conv_w[l], conv_b[l], conv_ln_g[l], conv_ln_b[l], w_o_conv[l],
                             w_out[l])
        xn = rms_norm(x, norm2_g[l])
        x = x + conv_ffn(xn, w_ffn_in[l], ffn_conv_w[l], ffn_conv_b[l], w_ffn_out[l])
    return x
```

```python
import functools

import jax
import jax.numpy as jnp
from jax import lax
from jax.experimental import pallas as pl
from jax.experimental.pallas import tpu as pltpu

HEAD_DIM = 128
MOBA_BLOCK = 256
MOBA_TOP_K = 3
ROPE_THETA = 10000.0
EPS = 1e-6

LANES = 128
SUBLANES = 8
VMEM_LIMIT_BYTES = 56 * 1024 * 1024

F32 = jnp.float32
BF16 = jnp.bfloat16


def _params(n_grid_dims):
    return pltpu.CompilerParams(
        dimension_semantics=("arbitrary",) * n_grid_dims,
        vmem_limit_bytes=VMEM_LIMIT_BYTES)


def _rmsnorm_kernel(x_ref, g_ref, o_ref):
    x = x_ref[...]
    ms = jnp.mean(x * x, axis=-1, keepdims=True)
    o_ref[...] = (x * lax.rsqrt(ms + EPS) * g_ref[...]).astype(o_ref.dtype)


def _rmsnorm(x, g, tr=256):
    t, d = x.shape
    return pl.pallas_call(
        _rmsnorm_kernel,
        grid=(t // tr,),
        in_specs=[pl.BlockSpec((tr, d), lambda i: (i, 0)),
                  pl.BlockSpec((1, d), lambda i: (0, 0))],
        out_specs=pl.BlockSpec((tr, d), lambda i: (i, 0)),
        out_shape=jax.ShapeDtypeStruct((t, d), BF16),
        compiler_params=_params(1),
        name="rmsnorm",
    )(x, g.reshape(1, d))


def _rope_table_kernel(pos_ref, invf_ref, sign_ref, cos_ref, sin_ref):
    ang = pos_ref[...] * invf_ref[...]
    cos_ref[...] = jnp.cos(ang)
    sin_ref[...] = jnp.sin(ang) * sign_ref[...]


def _rope_tables(positions, tr=1024):
    t = positions.size
    half = HEAD_DIM // 2
    inv = ROPE_THETA ** (-jnp.arange(half, dtype=F32) / half)
    invf = jnp.concatenate([inv, inv]).reshape(1, HEAD_DIM)
    sign = jnp.concatenate([-jnp.ones((half,), F32), jnp.ones((half,), F32)]).reshape(1, HEAD_DIM)
    pos = positions.astype(F32).reshape(t, 1)
    return pl.pallas_call(
        _rope_table_kernel,
        grid=(t // tr,),
        in_specs=[pl.BlockSpec((tr, 1), lambda i: (i, 0)),
                  pl.BlockSpec((1, HEAD_DIM), lambda i: (0, 0)),
                  pl.BlockSpec((1, HEAD_DIM), lambda i: (0, 0))],
        out_specs=[pl.BlockSpec((tr, HEAD_DIM), lambda i: (i, 0))] * 2,
        out_shape=[jax.ShapeDtypeStruct((t, HEAD_DIM), F32)] * 2,
        compiler_params=_params(1),
        name="rope_tables",
    )(pos, invf, sign)


def _ws_kernel(*refs, n_a, n_w, dots, n_extra, n_out, epilogue):
    a_refs = refs[:n_a]
    w_refs = refs[n_a:n_a + n_w]
    extra_refs = refs[n_a + n_w:n_a + n_w + n_extra]
    out_refs = refs[n_a + n_w + n_extra:n_a + n_w + n_extra + n_out]
    scratch = refs[n_a + n_w + n_extra + n_out:]
    to_cast = [w_ref for w_ref in w_refs if w_ref.dtype != BF16]
    cast_refs = scratch[:len(to_cast)]
    user_scratch = scratch[len(to_cast):]
    j = pl.program_id(0)
    i = pl.program_id(1)

    if to_cast:
        @pl.when(i == 0)
        def _cast_weights():
            for w_ref, wbf_ref in zip(to_cast, cast_refs):
                wbf_ref[...] = w_ref[...].astype(BF16)

    cast_iter = iter(cast_refs)
    wbf_refs = [w_ref if w_ref.dtype == BF16 else next(cast_iter) for w_ref in w_refs]

    accs = [jnp.dot(a_refs[ai][...], wbf_refs[wi][...], preferred_element_type=F32)
            for ai, wi in dots]
    epilogue(j, i, accs, extra_refs, out_refs, user_scratch)


def _ws_matmul(name, a_list, w_list, dots, n_col_tiles, tm, tn, epilogue,
               extras=(), extra_specs=(), out_shapes=(), out_specs=(), scratch=()):
    t = a_list[0].shape[0]
    in_specs = [pl.BlockSpec((tm, a.shape[1]), lambda j, i: (i, 0)) for a in a_list]
    for w, off in w_list:
        in_specs.append(pl.BlockSpec((w.shape[0], tn), functools.partial(
            lambda j, i, off: (0, j + off), off=off)))
    in_specs.extend(extra_specs)
    wbf_scratch = [pltpu.VMEM((w.shape[0], tn), BF16) for w, _ in w_list if w.dtype != BF16]
    kernel = functools.partial(
        _ws_kernel, n_a=len(a_list), n_w=len(w_list), dots=tuple(dots),
        n_extra=len(extras), n_out=len(out_shapes), epilogue=epilogue)
    return pl.pallas_call(
        kernel,
        grid=(n_col_tiles, t // tm),
        in_specs=in_specs,
        out_specs=list(out_specs),
        out_shape=list(out_shapes),
        scratch_shapes=wbf_scratch + list(scratch),
        compiler_params=_params(2),
        name=name,
    )(*a_list, *[w for w, _ in w_list], *extras)


def _tile_spec(tm, tn):
    return pl.BlockSpec((tm, tn), lambda j, i: (i, j))


def _qk_epilogue(j, i, accs, extra_refs, out_refs, scratch, *, n_q_tiles, tm, tn):
    cos_ref, sin_ref, g_ref = extra_refs
    o_ref, mean_ref = out_refs
    (acc_ref,) = scratch
    acc_ref[...] = accs[0]
    is_q = j < n_q_tiles
    g = jnp.where(is_q, g_ref[0:1, :], g_ref[1:2, :])
    scale = jnp.where(is_q, jnp.float32(HEAD_DIM ** -0.5), jnp.float32(1.0))
    cos = cos_ref[...]
    sin = sin_ref[...]
    rows_per_tile = tm // MOBA_BLOCK
    for h in range(tn // HEAD_DIM):
        cols = slice(h * HEAD_DIM, (h + 1) * HEAD_DIM)
        x = acc_ref[:, cols]
        ms = jnp.mean(x * x, axis=-1, keepdims=True)
        y = x * lax.rsqrt(ms + EPS) * g
        y = y * cos + pltpu.roll(y, HEAD_DIM // 2, 1) * sin
        o_ref[:, cols] = (y * scale).astype(o_ref.dtype)
        for r in range(rows_per_tile):
            blk = y[r * MOBA_BLOCK:(r + 1) * MOBA_BLOCK]
            mean_ref[0, r:r + 1, cols] = jnp.mean(blk, axis=0, keepdims=True)


def _store_bf16_epilogue(j, i, accs, extra_refs, out_refs, scratch):
    out_refs[0][...] = accs[0].astype(out_refs[0].dtype)


def _glu_epilogue(j, i, accs, extra_refs, out_refs, scratch):
    a, b = accs
    out_refs[0][...] = a * jax.nn.sigmoid(b)


def _sigmoid_epilogue(j, i, accs, extra_refs, out_refs, scratch):
    out_refs[0][...] = jax.nn.sigmoid(accs[0])


def _merge_epilogue(j, i, accs, extra_refs, out_refs, scratch):
    ga_ref, gc_ref = extra_refs
    y_attn, y_conv = accs
    out_refs[0][...] = (ga_ref[...] * y_attn + gc_ref[...] * y_conv).astype(out_refs[0].dtype)


def _residual_epilogue(j, i, accs, extra_refs, out_refs, scratch):
    out_refs[0][...] = extra_refs[0][...] + accs[0]


def _ffn_in_epilogue(j, i, accs, extra_refs, out_refs, scratch, *, tm, tiles_per_seq, width):
    cwg_ref, cwu_ref, cbg_ref, cbu_ref = extra_refs
    pad = SUBLANES

    @pl.when(i % tiles_per_seq == 0)
    def _zero_history():
        for hp in scratch:
            hp[0:pad, :] = jnp.zeros((pad, hp.shape[1]), F32)

    convs = []
    for acc, hp, cw_ref, cb_ref in zip(accs, scratch, (cwg_ref, cwu_ref), (cbg_ref, cbu_ref)):
        hp[pad:, :] = acc
        y = cb_ref[...] + cw_ref[width - 1:width, :] * hp[pad:, :]
        for w in range(width - 1):
            shift = width - 1 - w
            y = y + cw_ref[w:w + 1, :] * hp[pad - shift:pad - shift + tm, :]
        convs.append(y)
        hp[0:pad, :] = hp[tm:tm + pad, :]
    gate, up = convs
    out_refs[0][...] = (jax.nn.silu(gate) * up).astype(out_refs[0].dtype)


def _attn_kernel(q_ref, k_ref, v_ref, km_ref, o_ref, *, n_blocks):
    L = MOBA_BLOCK
    nt = (((1,), (1,)), ((), ()))
    km = km_ref[...].astype(BF16)
    neg = jnp.float32(-jnp.inf)
    row = lax.broadcasted_iota(jnp.int32, (L, L), 0)
    col = lax.broadcasted_iota(jnp.int32, (L, L), 1)
    causal = col <= row
    lane = lax.broadcasted_iota(jnp.int32, (L, LANES), 1)
    for i in range(n_blocks):
        qi = q_ref[i * L:(i + 1) * L, :]
        n_keys = (i + 1) * L
        s = lax.dot_general(qi, k_ref[0:n_keys, :], nt, preferred_element_type=F32)
        pen = None
        if i > MOBA_TOP_K:
            gate = lax.dot_general(qi, km, nt, preferred_element_type=F32)
            gate = jnp.concatenate(
                [gate, jnp.zeros((L, LANES - n_blocks), F32)], axis=1)
            rank = jnp.zeros((L, LANES), F32)
            for jp in range(i):
                gb = jnp.broadcast_to(gate[:, jp:jp + 1], (L, LANES))
                beats = (gb > gate) | ((gb == gate) & (jp < lane))
                rank = rank + jnp.where(beats & (lane != jp), 1.0, 0.0)
            pen = jnp.where(rank < MOBA_TOP_K, 0.0, neg)
        chunks = []
        for jb in range(i + 1):
            sj = s[:, jb * L:(jb + 1) * L]
            if jb == i:
                sj = jnp.where(causal, sj, neg)
            elif pen is not None:
                sj = sj + pen[:, jb:jb + 1]
            chunks.append(sj)
        m = chunks[0].max(axis=-1, keepdims=True)
        for c in chunks[1:]:
            m = jnp.maximum(m, c.max(axis=-1, keepdims=True))
        es = [jnp.exp(c - m) for c in chunks]
        l = es[0].sum(axis=-1, keepdims=True)
        for e in es[1:]:
            l = l + e.sum(axis=-1, keepdims=True)
        inv_l = 1.0 / l
        p = jnp.concatenate([(e * inv_l).astype(BF16) for e in es], axis=1)
        o = jnp.dot(p, v_ref[0:n_keys, :], preferred_element_type=F32)
        o_ref[i * L:(i + 1) * L, :] = o.astype(o_ref.dtype)


def _moba_attention(qk, v, kmean, batch, seq, n_heads):
    t = qk.shape[0]
    n_blocks = seq // MOBA_BLOCK
    return pl.pallas_call(
        functools.partial(_attn_kernel, n_blocks=n_blocks),
        grid=(batch, n_heads),
        in_specs=[pl.BlockSpec((seq, HEAD_DIM), lambda b, h: (b, h)),
                  pl.BlockSpec((seq, HEAD_DIM), lambda b, h: (b, n_heads + h)),
                  pl.BlockSpec((seq, HEAD_DIM), lambda b, h: (b, h)),
                  pl.BlockSpec((n_blocks, HEAD_DIM), lambda b, h: (b, n_heads + h))],
        out_specs=pl.BlockSpec((seq, HEAD_DIM), lambda b, h: (b, h)),
        out_shape=jax.ShapeDtypeStruct((t, n_heads * HEAD_DIM), BF16),
        compiler_params=_params(2),
        name="moba_attention",
    )(qk, qk, v, kmean)


def _conv_kernel(h_ref, halo_ref, cw_ref, cb_ref, g_ref, b_ref, o_ref, xpad_ref, y_ref,
                 *, ts, halo, width):
    t = pl.program_id(1)
    prev = halo_ref[...]
    xpad_ref[0:halo, :] = jnp.where(t == 0, jnp.zeros_like(prev), prev)
    xpad_ref[halo:, :] = h_ref[...]
    n_chunks = h_ref.shape[1] // LANES
    base = halo - (width - 1)

    def chunk(c, carry):
        cols = pl.ds(pl.multiple_of(c * LANES, LANES), LANES)
        acc = jnp.broadcast_to(cb_ref[:, cols], (ts, LANES))
        for w in range(width):
            acc = acc + cw_ref[w:w + 1, cols] * xpad_ref[base + w:base + w + ts, cols]
        y_ref[:, cols] = acc
        return carry

    lax.fori_loop(0, n_chunks, chunk, 0)
    y = y_ref[...]
    mu = jnp.mean(y, axis=-1, keepdims=True)
    yc = y - mu
    var = jnp.mean(yc * yc, axis=-1, keepdims=True)
    z = yc * lax.rsqrt(var + EPS) * g_ref[...] + b_ref[...]
    o_ref[...] = jax.nn.silu(z).astype(o_ref.dtype)


def _conv_module(h, conv_w, conv_b, ln_g, ln_b, batch, seq, ts=256, halo=32):
    t, c = h.shape
    width = conv_w.shape[0]
    tiles = seq // ts
    halo_per_tile = ts // halo

    def halo_map(b, s):
        return (jnp.maximum((b * tiles + s) * halo_per_tile - 1, 0), 0)

    vec = lambda: pl.BlockSpec((1, c), lambda b, s: (0, 0))
    return pl.pallas_call(
        functools.partial(_conv_kernel, ts=ts, halo=halo, width=width),
        grid=(batch, tiles),
        in_specs=[pl.BlockSpec((ts, c), lambda b, s: (b * tiles + s, 0)),
                  pl.BlockSpec((halo, c), halo_map),
                  pl.BlockSpec((width, c), lambda b, s: (0, 0)),
                  vec(), vec(), vec()],
        out_specs=pl.BlockSpec((ts, c), lambda b, s: (b * tiles + s, 0)),
        out_shape=jax.ShapeDtypeStruct((t, c), BF16),
        scratch_shapes=[pltpu.VMEM((halo + ts, c), F32), pltpu.VMEM((ts, c), F32)],
        compiler_params=_params(2),
        name="conv_module",
    )(h, h, conv_w, conv_b.reshape(1, c), ln_g.reshape(1, c), ln_b.reshape(1, c))


def _layer(x, positions, norm1_g, w_in, q_norm_g, k_norm_g, w_o_attn, conv_w, conv_b,
           conv_ln_g, conv_ln_b, w_o_conv, w_out, norm2_g, w_ffn_in, ffn_conv_w,
           ffn_conv_b, w_ffn_out):
    batch, seq, d_model = x.shape
    t = batch * seq
    attn_width = w_o_attn.shape[0]
    conv_dim = w_o_conv.shape[0]
    d_ff = w_ffn_out.shape[0]
    n_heads = attn_width // HEAD_DIM
    assert seq % MOBA_BLOCK == 0
    assert w_in.shape[1] == 3 * attn_width + 2 * conv_dim + 2 * d_model

    x2 = x.reshape(t, d_model)
    xn = _rmsnorm(x2, norm1_g)
    cos, sin = _rope_tables(positions)

    tm, tn = 1024, 512
    qk_g = jnp.stack([q_norm_g, k_norm_g])
    n_q_tiles = attn_width // tn
    qk, kmean = _ws_matmul(
        "qk_proj", [xn], [(w_in, 0)], [(0, 0)], 2 * n_q_tiles, tm, tn,
        functools.partial(_qk_epilogue, n_q_tiles=n_q_tiles, tm=tm, tn=tn),
        extras=(cos, sin, qk_g),
        extra_specs=(pl.BlockSpec((tm, HEAD_DIM), lambda j, i: (i, 0)),
                     pl.BlockSpec((tm, HEAD_DIM), lambda j, i: (i, 0)),
                     pl.BlockSpec((2, HEAD_DIM), lambda j, i: (0, 0))),
        out_shapes=(jax.ShapeDtypeStruct((t, 2 * attn_width), BF16),
                    jax.ShapeDtypeStruct((t // tm, tm // MOBA_BLOCK, 2 * attn_width), F32)),
        out_specs=(_tile_spec(tm, tn),
                   pl.BlockSpec((1, tm // MOBA_BLOCK, tn), lambda j, i: (i, 0, j))),
        scratch=(pltpu.VMEM((tm, tn), F32),))
    kmean = kmean.reshape(t // MOBA_BLOCK, 2 * attn_width)
    (v,) = _ws_matmul(
        "v_proj", [xn], [(w_in, 2 * attn_width // tn)], [(0, 0)], attn_width // tn, tm, tn,
        _store_bf16_epilogue,
        out_shapes=(jax.ShapeDtypeStruct((t, attn_width), BF16),),
        out_specs=(_tile_spec(tm, tn),))
    tg = 256
    u_off = 3 * attn_width
    (h,) = _ws_matmul(
        "glu_proj", [xn], [(w_in, u_off // tg), (w_in, (u_off + conv_dim) // tg)],
        [(0, 0), (0, 1)], conv_dim // tg, tm, tg, _glu_epilogue,
        out_shapes=(jax.ShapeDtypeStruct((t, conv_dim), F32),),
        out_specs=(_tile_spec(tm, tg),))
    g_off = u_off + 2 * conv_dim
    (gates,) = _ws_matmul(
        "gate_proj", [xn], [(w_in, g_off // tn)], [(0, 0)], 2 * d_model // tn, tm, tn,
        _sigmoid_epilogue,
        out_shapes=(jax.ShapeDtypeStruct((t, 2 * d_model), F32),),
        out_specs=(_tile_spec(tm, tn),))

    attn = _moba_attention(qk, v, kmean, batch, seq, n_heads)
    hc = _conv_module(h, conv_w, conv_b, conv_ln_g, conv_ln_b, batch, seq)

    n_model_tiles = d_model // tn
    (merged,) = _ws_matmul(
        "merge_proj", [attn, hc], [(w_o_attn, 0), (w_o_conv, 0)], [(0, 0), (1, 1)],
        n_model_tiles, tm, tn, _merge_epilogue,
        extras=(gates, gates),
        extra_specs=(pl.BlockSpec((tm, tn), lambda j, i: (i, j)),
                     pl.BlockSpec((tm, tn), functools.partial(
                         lambda j, i, off: (i, j + off), off=n_model_tiles))),
        out_shapes=(jax.ShapeDtypeStruct((t, d_model), BF16),),
        out_specs=(_tile_spec(tm, tn),))
    (x1,) = _ws_matmul(
        "out_proj", [merged], [(w_out, 0)], [(0, 0)], n_model_tiles, tm, tn,
        _residual_epilogue,
        extras=(x2,), extra_specs=(_tile_spec(tm, tn),),
        out_shapes=(jax.ShapeDtypeStruct((t, d_model), F32),),
        out_specs=(_tile_spec(tm, tn),))

    xn2 = _rmsnorm(x1, norm2_g)
    tf = 256
    assert d_ff % tf == 0 and seq % tm == 0
    n_ff_tiles = d_ff // tf
    fcw = ffn_conv_w
    fcb = ffn_conv_b.reshape(1, 2 * d_ff)
    width = fcw.shape[0]
    up_spec = lambda rows: pl.BlockSpec((rows, tf), functools.partial(
        lambda j, i, off: (0, j + off), off=n_ff_tiles))
    (act,) = _ws_matmul(
        "ffn_in", [xn2], [(w_ffn_in, 0), (w_ffn_in, n_ff_tiles)], [(0, 0), (0, 1)],
        n_ff_tiles, tm, tf,
        functools.partial(_ffn_in_epilogue, tm=tm, tiles_per_seq=seq // tm, width=width),
        extras=(fcw, fcw, fcb, fcb),
        extra_specs=(pl.BlockSpec((width, tf), lambda j, i: (0, j)), up_spec(width),
                     pl.BlockSpec((1, tf), lambda j, i: (0, j)), up_spec(1)),
        out_shapes=(jax.ShapeDtypeStruct((t, d_ff), BF16),),
        out_specs=(_tile_spec(tm, tf),),
        scratch=(pltpu.VMEM((SUBLANES + tm, tf), F32), pltpu.VMEM((SUBLANES + tm, tf), F32)))

    tmo = 512
    (out,) = _ws_matmul(
        "ffn_out", [act], [(w_ffn_out.astype(BF16), 0)], [(0, 0)], n_model_tiles, tmo, tn,
        _residual_epilogue,
        extras=(x1,), extra_specs=(_tile_spec(tmo, tn),),
        out_shapes=(jax.ShapeDtypeStruct((t, d_model), F32),),
        out_specs=(_tile_spec(tmo, tn),))
    return out.reshape(batch, seq, d_model)


def kernel(x, positions, norm1_g, w_in, q_norm_g, k_norm_g, w_o_attn, conv_w, conv_b,
           conv_ln_g, conv_ln_b, w_o_conv, w_out, norm2_g, w_ffn_in, ffn_conv_w,
           ffn_conv_b, w_ffn_out):
    depth = norm1_g.shape[0]
    params = (norm1_g, w_in, q_norm_g, k_norm_g, w_o_attn, conv_w, conv_b, conv_ln_g,
              conv_ln_b, w_o_conv, w_out, norm2_g, w_ffn_in, ffn_conv_w, ffn_conv_b, w_ffn_out)
    for l in range(depth):
        layer = [p.reshape(p.shape[1:]) if depth == 1 else p[l] for p in params]
        x = _layer(x, positions, *layer)
    return x
```

```python
import functools

import jax
import jax.numpy as jnp
from jax import lax
from jax.experimental import pallas as pl
from jax.experimental.pallas import tpu as pltpu

HEAD_DIM = 128
MOBA_BLOCK = 256
MOBA_TOP_K = 3
ROPE_THETA = 10000.0
EPS = 1e-6

LANES = 128
SUBLANES = 8
VMEM_LIMIT_BYTES = 56 * 1024 * 1024
FFN_ROW_CHUNK = 64

F32 = jnp.float32
BF16 = jnp.bfloat16


def _params(n_grid_dims):
    return pltpu.CompilerParams(
        dimension_semantics=("arbitrary",) * n_grid_dims,
        vmem_limit_bytes=VMEM_LIMIT_BYTES)


def _rmsnorm_kernel(x_ref, g_ref, o_ref):
    x = x_ref[...]
    ms = jnp.mean(x * x, axis=-1, keepdims=True)
    o_ref[...] = (x * lax.rsqrt(ms + EPS) * g_ref[...]).astype(o_ref.dtype)


def _rmsnorm(x, g, tr=256):
    t, d = x.shape
    return pl.pallas_call(
        _rmsnorm_kernel,
        grid=(t // tr,),
        in_specs=[pl.BlockSpec((tr, d), lambda i: (i, 0)),
                  pl.BlockSpec((1, d), lambda i: (0, 0))],
        out_specs=pl.BlockSpec((tr, d), lambda i: (i, 0)),
        out_shape=jax.ShapeDtypeStruct((t, d), BF16),
        compiler_params=_params(1),
        name="rmsnorm",
    )(x, g.reshape(1, d))


def _rope_table_kernel(pos_ref, invf_ref, sign_ref, cos_ref, sin_ref):
    ang = pos_ref[...] * invf_ref[...]
    cos_ref[...] = jnp.cos(ang)
    sin_ref[...] = jnp.sin(ang) * sign_ref[...]


def _rope_tables(positions, tr=1024):
    t = positions.size
    half = HEAD_DIM // 2
    inv = ROPE_THETA ** (-jnp.arange(half, dtype=F32) / half)
    invf = jnp.concatenate([inv, inv]).reshape(1, HEAD_DIM)
    sign = jnp.concatenate([-jnp.ones((half,), F32), jnp.ones((half,), F32)]).reshape(1, HEAD_DIM)
    pos = positions.astype(F32).reshape(t, 1)
    return pl.pallas_call(
        _rope_table_kernel,
        grid=(t // tr,),
        in_specs=[pl.BlockSpec((tr, 1), lambda i: (i, 0)),
                  pl.BlockSpec((1, HEAD_DIM), lambda i: (0, 0)),
                  pl.BlockSpec((1, HEAD_DIM), lambda i: (0, 0))],
        out_specs=[pl.BlockSpec((tr, HEAD_DIM), lambda i: (i, 0))] * 2,
        out_shape=[jax.ShapeDtypeStruct((t, HEAD_DIM), F32)] * 2,
        compiler_params=_params(1),
        name="rope_tables",
    )(pos, invf, sign)


def _ws_kernel(*refs, n_a, n_w, dots, n_extra, n_out, epilogue, n_row_tiles, n_tiles, pad):
    a_refs = refs[:n_a]
    w_refs = refs[n_a:n_a + n_w]
    extra_refs = refs[n_a + n_w:n_a + n_w + n_extra]
    out_refs = refs[n_a + n_w + n_extra:n_a + n_w + n_extra + n_out]
    scratch = refs[n_a + n_w + n_extra + n_out:]
    to_cast = [w_ref for w_ref in w_refs if w_ref.dtype != BF16]
    cast_refs = scratch[:len(to_cast)]
    n_raw = 2 * len(dots)
    raw_refs = scratch[len(to_cast):len(to_cast) + n_raw]
    user_scratch = scratch[len(to_cast) + n_raw:]
    s = pl.program_id(0)
    tile = jnp.minimum(s, n_tiles - 1)

    if to_cast:
        @pl.when((tile % n_row_tiles == 0) & (s < n_tiles))
        def _cast_weights():
            for w_ref, wbf_ref in zip(to_cast, cast_refs):
                wbf_ref[...] = w_ref[...].astype(BF16)

    @pl.when(s == 0)
    def _init():
        for raw in raw_refs[1::2]:
            raw[...] = jnp.zeros(raw.shape, F32)
        for u in user_scratch:
            u[...] = jnp.zeros(u.shape, u.dtype)

    cast_iter = iter(cast_refs)
    wbf_refs = [w_ref if w_ref.dtype == BF16 else next(cast_iter) for w_ref in w_refs]
    done = jnp.maximum(s - 1, 0)

    def step(slot):
        for d, (ai, wi) in enumerate(dots):
            raw_refs[2 * d + slot][pad:, :] = jnp.dot(
                a_refs[ai][...], wbf_refs[wi][...], preferred_element_type=F32)
        epilogue(done // n_row_tiles, done % n_row_tiles,
                 [raw_refs[2 * d + 1 - slot] for d in range(len(dots))],
                 extra_refs, out_refs, user_scratch)

    for slot in range(2):
        pl.when(s % 2 == slot)(functools.partial(step, slot))


def _ws_matmul(name, a_list, w_list, dots, n_col_tiles, tm, tn, epilogue,
               extras=(), extra_specs=(), out_shapes=(), out_specs=(), scratch=(), pad=0):
    t = a_list[0].shape[0]
    n_row_tiles = t // tm
    n_tiles = n_col_tiles * n_row_tiles

    def dot_tile(s):
        return jnp.minimum(s, n_tiles - 1)

    def lagged(index_map):
        def wrapped(s):
            done = jnp.maximum(s - 1, 0)
            return index_map(done // n_row_tiles, done % n_row_tiles)
        return wrapped

    in_specs = [pl.BlockSpec((tm, a.shape[1]), lambda s: (dot_tile(s) % n_row_tiles, 0))
                for a in a_list]
    for w, off in w_list:
        in_specs.append(pl.BlockSpec((w.shape[0], tn), functools.partial(
            lambda s, off: (0, dot_tile(s) // n_row_tiles + off), off=off)))
    in_specs.extend(pl.BlockSpec(shape, lagged(fn)) for shape, fn in extra_specs)
    wbf_scratch = [pltpu.VMEM((w.shape[0], tn), BF16) for w, _ in w_list if w.dtype != BF16]
    raw_scratch = [pltpu.VMEM((pad + tm, tn), F32) for _ in range(2 * len(dots))]
    kernel = functools.partial(
        _ws_kernel, n_a=len(a_list), n_w=len(w_list), dots=tuple(dots),
        n_extra=len(extras), n_out=len(out_shapes), epilogue=epilogue,
        n_row_tiles=n_row_tiles, n_tiles=n_tiles, pad=pad)
    return pl.pallas_call(
        kernel,
        grid=(n_tiles + 1,),
        in_specs=in_specs,
        out_specs=[pl.BlockSpec(shape, lagged(fn)) for shape, fn in out_specs],
        out_shape=list(out_shapes),
        scratch_shapes=wbf_scratch + raw_scratch + list(scratch),
        compiler_params=_params(1),
        name=name,
    )(*a_list, *[w for w, _ in w_list], *extras)


def _tile_spec(tm, tn, col_offset=0):
    return ((tm, tn), lambda j, i: (i, j + col_offset))


def _qk_epilogue(j, i, acc_refs, extra_refs, out_refs, scratch, *, n_q_tiles, tm, tn):
    cos_ref, sin_ref, g_ref = extra_refs
    o_ref, mean_ref = out_refs
    acc_ref = acc_refs[0]
    is_q = j < n_q_tiles
    g = jnp.where(is_q, g_ref[0:1, :], g_ref[1:2, :])
    scale = jnp.where(is_q, jnp.float32(HEAD_DIM ** -0.5), jnp.float32(1.0))
    cos = cos_ref[...]
    sin = sin_ref[...]
    rows_per_tile = tm // MOBA_BLOCK
    for h in range(tn // HEAD_DIM):
        cols = slice(h * HEAD_DIM, (h + 1) * HEAD_DIM)
        x = acc_ref[:, cols]
        ms = jnp.mean(x * x, axis=-1, keepdims=True)
        y = x * lax.rsqrt(ms + EPS) * g
        y = y * cos + pltpu.roll(y, HEAD_DIM // 2, 1) * sin
        o_ref[:, cols] = (y * scale).astype(o_ref.dtype)
        for r in range(rows_per_tile):
            blk = y[r * MOBA_BLOCK:(r + 1) * MOBA_BLOCK]
            mean_ref[0, r:r + 1, cols] = jnp.mean(blk, axis=0, keepdims=True)


def _store_bf16_epilogue(j, i, acc_refs, extra_refs, out_refs, scratch):
    out_refs[0][...] = acc_refs[0][...].astype(out_refs[0].dtype)


def _glu_epilogue(j, i, acc_refs, extra_refs, out_refs, scratch):
    out_refs[0][...] = acc_refs[0][...] * jax.nn.sigmoid(acc_refs[1][...])


def _sigmoid_epilogue(j, i, acc_refs, extra_refs, out_refs, scratch):
    out_refs[0][...] = jax.nn.sigmoid(acc_refs[0][...])


def _merge_epilogue(j, i, acc_refs, extra_refs, out_refs, scratch):
    ga_ref, gc_ref = extra_refs
    merged = ga_ref[...] * acc_refs[0][...] + gc_ref[...] * acc_refs[1][...]
    out_refs[0][...] = merged.astype(out_refs[0].dtype)


def _residual_epilogue(j, i, acc_refs, extra_refs, out_refs, scratch):
    out_refs[0][...] = extra_refs[0][...] + acc_refs[0][...]


def _ffn_in_epilogue(j, i, acc_refs, extra_refs, out_refs, scratch, *, tm, tiles_per_seq, width):
    cwg_ref, cwu_ref, cbg_ref, cbu_ref = extra_refs
    pad = SUBLANES
    seq_start = i % tiles_per_seq == 0
    for hp, carry in zip(acc_refs, scratch):
        prev = carry[...]
        hp[0:pad, :] = jnp.where(seq_start, jnp.zeros_like(prev), prev)
        carry[...] = hp[tm:tm + pad, :]
    for r in range(0, tm, FFN_ROW_CHUNK):
        convs = []
        for hp, cw_ref, cb_ref in zip(acc_refs, (cwg_ref, cwu_ref), (cbg_ref, cbu_ref)):
            y = cb_ref[...]
            for w in range(width):
                lo = pad + r - (width - 1 - w)
                y = y + cw_ref[w:w + 1, :] * hp[lo:lo + FFN_ROW_CHUNK, :]
            convs.append(y)
        gate, up = convs
        out_refs[0][r:r + FFN_ROW_CHUNK, :] = (jax.nn.silu(gate) * up).astype(out_refs[0].dtype)


def _attn_kernel(q_ref, k_ref, v_ref, km_ref, o_ref, *, n_blocks):
    L = MOBA_BLOCK
    nt = (((1,), (1,)), ((), ()))
    km = km_ref[...].astype(BF16)
    neg = jnp.float32(-jnp.inf)
    row = lax.broadcasted_iota(jnp.int32, (L, L), 0)
    col = lax.broadcasted_iota(jnp.int32, (L, L), 1)
    causal = col <= row
    lane = lax.broadcasted_iota(jnp.int32, (L, LANES), 1)
    for i in range(n_blocks):
        qi = q_ref[i * L:(i + 1) * L, :]
        n_keys = (i + 1) * L
        s = lax.dot_general(qi, k_ref[0:n_keys, :], nt, preferred_element_type=F32)
        pen = None
        if i > MOBA_TOP_K:
            gate = lax.dot_general(qi, km, nt, preferred_element_type=F32)
            gate = jnp.concatenate(
                [gate, jnp.zeros((L, LANES - n_blocks), F32)], axis=1)
            rank = jnp.zeros((L, LANES), F32)
            for jp in range(i):
                gb = jnp.broadcast_to(gate[:, jp:jp + 1], (L, LANES))
                beats = (gb > gate) | ((gb == gate) & (jp < lane))
                rank = rank + jnp.where(beats & (lane != jp), 1.0, 0.0)
            pen = jnp.where(rank < MOBA_TOP_K, 0.0, neg)
        chunks = []
        for jb in range(i + 1):
            sj = s[:, jb * L:(jb + 1) * L]
            if jb == i:
                sj = jnp.where(causal, sj, neg)
            elif pen is not None:
                sj = sj + pen[:, jb:jb + 1]
            chunks.append(sj)
        m = chunks[0].max(axis=-1, keepdims=True)
        for c in chunks[1:]:
            m = jnp.maximum(m, c.max(axis=-1, keepdims=True))
        es = [jnp.exp(c - m) for c in chunks]
        l = es[0].sum(axis=-1, keepdims=True)
        for e in es[1:]:
            l = l + e.sum(axis=-1, keepdims=True)
        inv_l = 1.0 / l
        p = jnp.concatenate([(e * inv_l).astype(BF16) for e in es], axis=1)
        o = jnp.dot(p, v_ref[0:n_keys, :], preferred_element_type=F32)
        o_ref[i * L:(i + 1) * L, :] = o.astype(o_ref.dtype)


def _moba_attention(qk, v, kmean, batch, seq, n_heads):
    t = qk.shape[0]
    n_blocks = seq // MOBA_BLOCK
    return pl.pallas_call(
        functools.partial(_attn_kernel, n_blocks=n_blocks),
        grid=(batch, n_heads),
        in_specs=[pl.BlockSpec((seq, HEAD_DIM), lambda b, h: (b, h)),
                  pl.BlockSpec((seq, HEAD_DIM), lambda b, h: (b, n_heads + h)),
                  pl.BlockSpec((seq, HEAD_DIM), lambda b, h: (b, h)),
                  pl.BlockSpec((n_blocks, HEAD_DIM), lambda b, h: (b, n_heads + h))],
        out_specs=pl.BlockSpec((seq, HEAD_DIM), lambda b, h: (b, h)),
        out_shape=jax.ShapeDtypeStruct((t, n_heads * HEAD_DIM), BF16),
        compiler_params=_params(2),
        name="moba_attention",
    )(qk, qk, v, kmean)


def _conv_kernel(h_ref, halo_ref, cw_ref, cb_ref, g_ref, b_ref, o_ref, xpad_ref, y_ref,
                 *, ts, halo, width):
    t = pl.program_id(1)
    prev = halo_ref[...]
    xpad_ref[0:halo, :] = jnp.where(t == 0, jnp.zeros_like(prev), prev)
    xpad_ref[halo:, :] = h_ref[...]
    n_chunks = h_ref.shape[1] // LANES
    base = halo - (width - 1)

    def chunk(c, carry):
        cols = pl.ds(pl.multiple_of(c * LANES, LANES), LANES)
        acc = jnp.broadcast_to(cb_ref[:, cols], (ts, LANES))
        for w in range(width):
            acc = acc + cw_ref[w:w + 1, cols] * xpad_ref[base + w:base + w + ts, cols]
        y_ref[:, cols] = acc
        return carry

    lax.fori_loop(0, n_chunks, chunk, 0)
    y = y_ref[...]
    mu = jnp.mean(y, axis=-1, keepdims=True)
    yc = y - mu
    var = jnp.mean(yc * yc, axis=-1, keepdims=True)
    z = yc * lax.rsqrt(var + EPS) * g_ref[...] + b_ref[...]
    o_ref[...] = jax.nn.silu(z).astype(o_ref.dtype)


def _conv_module(h, conv_w, conv_b, ln_g, ln_b, batch, seq, ts=256, halo=32):
    t, c = h.shape
    width = conv_w.shape[0]
    tiles = seq // ts
    halo_per_tile = ts // halo

    def halo_map(b, s):
        return (jnp.maximum((b * tiles + s) * halo_per_tile - 1, 0), 0)

    vec = lambda: pl.BlockSpec((1, c), lambda b, s: (0, 0))
    return pl.pallas_call(
        functools.partial(_conv_kernel, ts=ts, halo=halo, width=width),
        grid=(batch, tiles),
        in_specs=[pl.BlockSpec((ts, c), lambda b, s: (b * tiles + s, 0)),
                  pl.BlockSpec((halo, c), halo_map),
                  pl.BlockSpec((width, c), lambda b, s: (0, 0)),
                  vec(), vec(), vec()],
        out_specs=pl.BlockSpec((ts, c), lambda b, s: (b * tiles + s, 0)),
        out_shape=jax.ShapeDtypeStruct((t, c), BF16),
        scratch_shapes=[pltpu.VMEM((halo + ts, c), F32), pltpu.VMEM((ts, c), F32)],
        compiler_params=_params(2),
        name="conv_module",
    )(h, h, conv_w, conv_b.reshape(1, c), ln_g.reshape(1, c), ln_b.reshape(1, c))


def _layer(x, positions, norm1_g, w_in, q_norm_g, k_norm_g, w_o_attn, conv_w, conv_b,
           conv_ln_g, conv_ln_b, w_o_conv, w_out, norm2_g, w_ffn_in, ffn_conv_w,
           ffn_conv_b, w_ffn_out):
    batch, seq, d_model = x.shape
    t = batch * seq
    attn_width = w_o_attn.shape[0]
    conv_dim = w_o_conv.shape[0]
    d_ff = w_ffn_out.shape[0]
    n_heads = attn_width // HEAD_DIM
    assert seq % MOBA_BLOCK == 0
    assert w_in.shape[1] == 3 * attn_width + 2 * conv_dim + 2 * d_model

    x2 = x.reshape(t, d_model)
    xn = _rmsnorm(x2, norm1_g)
    cos, sin = _rope_tables(positions)

    tm, tn = 1024, 512
    qk_g = jnp.stack([q_norm_g, k_norm_g])
    n_q_tiles = attn_width // tn
    qk, kmean = _ws_matmul(
        "qk_proj", [xn], [(w_in, 0)], [(0, 0)], 2 * n_q_tiles, tm, tn,
        functools.partial(_qk_epilogue, n_q_tiles=n_q_tiles, tm=tm, tn=tn),
        extras=(cos, sin, qk_g),
        extra_specs=(((tm, HEAD_DIM), lambda j, i: (i, 0)),
                     ((tm, HEAD_DIM), lambda j, i: (i, 0)),
                     ((2, HEAD_DIM), lambda j, i: (0, 0))),
        out_shapes=(jax.ShapeDtypeStruct((t, 2 * attn_width), BF16),
                    jax.ShapeDtypeStruct((t // tm, tm // MOBA_BLOCK, 2 * attn_width), F32)),
        out_specs=(_tile_spec(tm, tn),
                   ((1, tm // MOBA_BLOCK, tn), lambda j, i: (i, 0, j))))
    kmean = kmean.reshape(t // MOBA_BLOCK, 2 * attn_width)
    (v,) = _ws_matmul(
        "v_proj", [xn], [(w_in, 2 * attn_width // tn)], [(0, 0)], attn_width // tn, tm, tn,
        _store_bf16_epilogue,
        out_shapes=(jax.ShapeDtypeStruct((t, attn_width), BF16),),
        out_specs=(_tile_spec(tm, tn),))
    tg = 256
    u_off = 3 * attn_width
    (h,) = _ws_matmul(
        "glu_proj", [xn], [(w_in, u_off // tg), (w_in, (u_off + conv_dim) // tg)],
        [(0, 0), (0, 1)], conv_dim // tg, tm, tg, _glu_epilogue,
        out_shapes=(jax.ShapeDtypeStruct((t, conv_dim), F32),),
        out_specs=(_tile_spec(tm, tg),))
    g_off = u_off + 2 * conv_dim
    (gates,) = _ws_matmul(
        "gate_proj", [xn], [(w_in, g_off // tn)], [(0, 0)], 2 * d_model // tn, tm, tn,
        _sigmoid_epilogue,
        out_shapes=(jax.ShapeDtypeStruct((t, 2 * d_model), F32),),
        out_specs=(_tile_spec(tm, tn),))

    attn = _moba_attention(qk, v, kmean, batch, seq, n_heads)
    hc = _conv_module(h, conv_w, conv_b, conv_ln_g, conv_ln_b, batch, seq)

    n_model_tiles = d_model // tn
    tmm = 512
    (merged,) = _ws_matmul(
        "merge_proj", [attn, hc], [(w_o_attn, 0), (w_o_conv, 0)], [(0, 0), (1, 1)],
        n_model_tiles, tmm, tn, _merge_epilogue,
        extras=(gates, gates),
        extra_specs=(_tile_spec(tmm, tn), _tile_spec(tmm, tn, n_model_tiles)),
        out_shapes=(jax.ShapeDtypeStruct((t, d_model), BF16),),
        out_specs=(_tile_spec(tmm, tn),))
    (x1,) = _ws_matmul(
        "out_proj", [merged], [(w_out, 0)], [(0, 0)], n_model_tiles, tm, tn,
        _residual_epilogue,
        extras=(x2,), extra_specs=(_tile_spec(tm, tn),),
        out_shapes=(jax.ShapeDtypeStruct((t, d_model), F32),),
        out_specs=(_tile_spec(tm, tn),))

    xn2 = _rmsnorm(x1, norm2_g)
    tf = 256
    assert d_ff % tf == 0 and seq % tm == 0
    n_ff_tiles = d_ff // tf
    fcw = ffn_conv_w
    fcb = ffn_conv_b.reshape(1, 2 * d_ff)
    width = fcw.shape[0]
    (act,) = _ws_matmul(
        "ffn_in", [xn2], [(w_ffn_in, 0), (w_ffn_in, n_ff_tiles)], [(0, 0), (0, 1)],
        n_ff_tiles, tm, tf,
        functools.partial(_ffn_in_epilogue, tm=tm, tiles_per_seq=seq // tm, width=width),
        extras=(fcw, fcw, fcb, fcb),
        extra_specs=(((width, tf), lambda j, i: (0, j)),
                     ((width, tf), lambda j, i: (0, j + n_ff_tiles)),
                     ((1, tf), lambda j, i: (0, j)),
                     ((1, tf), lambda j, i: (0, j + n_ff_tiles))),
        out_shapes=(jax.ShapeDtypeStruct((t, d_ff), BF16),),
        out_specs=(_tile_spec(tm, tf),),
        scratch=(pltpu.VMEM((SUBLANES, tf), F32), pltpu.VMEM((SUBLANES, tf), F32)),
        pad=SUBLANES)

    tmo = 512
    (out,) = _ws_matmul(
        "ffn_out", [act], [(w_ffn_out.astype(BF16), 0)], [(0, 0)], n_model_tiles, tmo, tn,
        _residual_epilogue,
        extras=(x1,), extra_specs=(_tile_spec(tmo, tn),),
        out_shapes=(jax.ShapeDtypeStruct((t, d_model), F32),),
        out_specs=(_tile_spec(tmo, tn),))
    return out.reshape(batch, seq, d_model)


def kernel(x, positions, norm1_g, w_in, q_norm_g, k_norm_g, w_o_attn, conv_w, conv_b,
           conv_ln_g, conv_ln_b, w_o_conv, w_out, norm2_g, w_ffn_in, ffn_conv_w,
           ffn_conv_b, w_ffn_out):
    depth = norm1_g.shape[0]
    params = (norm1_g, w_in, q_norm_g, k_norm_g, w_o_attn, conv_w, conv_b, conv_ln_g,
              conv_ln_b, w_o_conv, w_out, norm2_g, w_ffn_in, ffn_conv_w, ffn_conv_b, w_ffn_out)
    for l in range(depth):
        layer = [p.reshape(p.shape[1:]) if depth == 1 else p[l] for p in params]
        x = _layer(x, positions, *layer)
    return x
```

```python
import functools

import jax
import jax.numpy as jnp
from jax import lax
from jax.experimental import pallas as pl
from jax.experimental.pallas import tpu as pltpu

HEAD_DIM = 128
MOBA_BLOCK = 256
MOBA_TOP_K = 3
ROPE_THETA = 10000.0
EPS = 1e-6

LANES = 128
SUBLANES = 8
VMEM_LIMIT_BYTES = 56 * 1024 * 1024
FFN_ROW_CHUNK = 64
CONV_ROW_GROUP = 128

F32 = jnp.float32
BF16 = jnp.bfloat16


def _params(n_grid_dims):
    return pltpu.CompilerParams(
        dimension_semantics=("arbitrary",) * n_grid_dims,
        vmem_limit_bytes=VMEM_LIMIT_BYTES)


def _rmsnorm_kernel(x_ref, g_ref, o_ref):
    x = x_ref[...]
    ms = jnp.mean(x * x, axis=-1, keepdims=True)
    o_ref[...] = (x * lax.rsqrt(ms + EPS) * g_ref[...]).astype(o_ref.dtype)


def _rmsnorm(x, g, tr=256):
    t, d = x.shape
    return pl.pallas_call(
        _rmsnorm_kernel,
        grid=(t // tr,),
        in_specs=[pl.BlockSpec((tr, d), lambda i: (i, 0)),
                  pl.BlockSpec((1, d), lambda i: (0, 0))],
        out_specs=pl.BlockSpec((tr, d), lambda i: (i, 0)),
        out_shape=jax.ShapeDtypeStruct((t, d), BF16),
        compiler_params=_params(1),
        name="rmsnorm",
    )(x, g.reshape(1, d))


def _rope_table_kernel(pos_ref, invf_ref, sign_ref, cos_ref, sin_ref):
    ang = pos_ref[...] * invf_ref[...]
    cos_ref[...] = jnp.cos(ang)
    sin_ref[...] = jnp.sin(ang) * sign_ref[...]


def _rope_tables(positions, tr=1024):
    t = positions.size
    half = HEAD_DIM // 2
    inv = ROPE_THETA ** (-jnp.arange(half, dtype=F32) / half)
    invf = jnp.concatenate([inv, inv]).reshape(1, HEAD_DIM)
    sign = jnp.concatenate([-jnp.ones((half,), F32), jnp.ones((half,), F32)]).reshape(1, HEAD_DIM)
    pos = positions.astype(F32).reshape(t, 1)
    return pl.pallas_call(
        _rope_table_kernel,
        grid=(t // tr,),
        in_specs=[pl.BlockSpec((tr, 1), lambda i: (i, 0)),
                  pl.BlockSpec((1, HEAD_DIM), lambda i: (0, 0)),
                  pl.BlockSpec((1, HEAD_DIM), lambda i: (0, 0))],
        out_specs=[pl.BlockSpec((tr, HEAD_DIM), lambda i: (i, 0))] * 2,
        out_shape=[jax.ShapeDtypeStruct((t, HEAD_DIM), F32)] * 2,
        compiler_params=_params(1),
        name="rope_tables",
    )(pos, invf, sign)


def _ws_kernel(*refs, n_a, n_w, dots, n_extra, n_out, epilogue, n_row_tiles, n_tiles, pad, lag,
               lane_split):
    a_refs = refs[:n_a]
    w_refs = refs[n_a:n_a + n_w]
    extra_refs = refs[n_a + n_w:n_a + n_w + n_extra]
    out_refs = refs[n_a + n_w + n_extra:n_a + n_w + n_extra + n_out]
    scratch = refs[n_a + n_w + n_extra + n_out:]
    to_cast = [w_ref for w_ref in w_refs if w_ref.dtype != BF16]
    cast_refs = scratch[:len(to_cast)]
    n_slots = 1 + lag
    n_raw = n_slots * len(dots)
    raw_refs = scratch[len(to_cast):len(to_cast) + n_raw]
    user_scratch = scratch[len(to_cast) + n_raw:]
    s = pl.program_id(0)
    tile = jnp.minimum(s, n_tiles - 1)

    if to_cast:
        @pl.when((tile % n_row_tiles == 0) & (s < n_tiles))
        def _cast_weights():
            for w_ref, wbf_ref in zip(to_cast, cast_refs):
                wbf_ref[...] = w_ref[...].astype(BF16)

    if lag or user_scratch:
        @pl.when(s == 0)
        def _init():
            if lag:
                for raw in raw_refs[1::2]:
                    raw[...] = jnp.zeros(raw.shape, F32)
            for u in user_scratch:
                u[...] = jnp.zeros(u.shape, u.dtype)

    cast_iter = iter(cast_refs)
    wbf_refs = [w_ref if w_ref.dtype == BF16 else next(cast_iter) for w_ref in w_refs]
    done = jnp.maximum(s - lag, 0)

    def step(slot):
        for d, (ai, wi) in enumerate(dots):
            res = jnp.dot(a_refs[ai][...], wbf_refs[wi][...], preferred_element_type=F32)
            raw = raw_refs[n_slots * d + slot]
            if lane_split:
                for lt in range(raw.shape[0]):
                    raw[lt, pad:, :] = res[:, lt * LANES:(lt + 1) * LANES]
            else:
                raw[pad:, :] = res
        epilogue(done // n_row_tiles, done % n_row_tiles,
                 [raw_refs[n_slots * d + (slot + lag) % n_slots] for d in range(len(dots))],
                 extra_refs, out_refs, user_scratch)

    if lag:
        for slot in range(2):
            pl.when(s % 2 == slot)(functools.partial(step, slot))
    else:
        step(0)


def _ws_matmul(name, a_list, w_list, dots, n_col_tiles, tm, tn, epilogue,
               extras=(), extra_specs=(), out_shapes=(), out_specs=(), scratch=(), pad=0, lag=0,
               lane_split=False):
    t = a_list[0].shape[0]
    n_row_tiles = t // tm
    n_tiles = n_col_tiles * n_row_tiles

    def dot_tile(s):
        return jnp.minimum(s, n_tiles - 1)

    def lagged(index_map):
        def wrapped(s):
            done = jnp.maximum(s - lag, 0)
            return index_map(done // n_row_tiles, done % n_row_tiles)
        return wrapped

    in_specs = [pl.BlockSpec((tm, a.shape[1]), lambda s: (dot_tile(s) % n_row_tiles, 0))
                for a in a_list]
    for w, off in w_list:
        in_specs.append(pl.BlockSpec((w.shape[0], tn), functools.partial(
            lambda s, off: (0, dot_tile(s) // n_row_tiles + off), off=off)))
    in_specs.extend(pl.BlockSpec(shape, lagged(fn)) for shape, fn in extra_specs)
    wbf_scratch = [pltpu.VMEM((w.shape[0], tn), BF16) for w, _ in w_list if w.dtype != BF16]
    raw_shape = (tn // LANES, pad + tm, LANES) if lane_split else (pad + tm, tn)
    raw_scratch = [pltpu.VMEM(raw_shape, F32) for _ in range((1 + lag) * len(dots))]
    kernel = functools.partial(
        _ws_kernel, n_a=len(a_list), n_w=len(w_list), dots=tuple(dots),
        n_extra=len(extras), n_out=len(out_shapes), epilogue=epilogue,
        n_row_tiles=n_row_tiles, n_tiles=n_tiles, pad=pad, lag=lag, lane_split=lane_split)
    return pl.pallas_call(
        kernel,
        grid=(n_tiles + lag,),
        in_specs=in_specs,
        out_specs=[pl.BlockSpec(shape, lagged(fn)) for shape, fn in out_specs],
        out_shape=list(out_shapes),
        scratch_shapes=wbf_scratch + raw_scratch + list(scratch),
        compiler_params=_params(1),
        name=name,
    )(*a_list, *[w for w, _ in w_list], *extras)


def _tile_spec(tm, tn, col_offset=0):
    return ((tm, tn), lambda j, i: (i, j + col_offset))


def _qk_epilogue(j, i, acc_refs, extra_refs, out_refs, scratch, *, n_q_tiles, tm, tn):
    cos_ref, sin_ref, g_ref = extra_refs
    o_ref, mean_ref = out_refs
    acc_ref = acc_refs[0]
    is_q = j < n_q_tiles
    g = jnp.where(is_q, g_ref[0:1, :], g_ref[1:2, :])
    scale = jnp.where(is_q, jnp.float32(HEAD_DIM ** -0.5), jnp.float32(1.0))
    cos = cos_ref[...]
    sin = sin_ref[...]
    rows_per_tile = tm // MOBA_BLOCK
    for h in range(tn // HEAD_DIM):
        cols = slice(h * HEAD_DIM, (h + 1) * HEAD_DIM)
        x = acc_ref[:, cols]
        ms = jnp.mean(x * x, axis=-1, keepdims=True)
        y = x * lax.rsqrt(ms + EPS) * g
        y = y * cos + pltpu.roll(y, HEAD_DIM // 2, 1) * sin
        o_ref[:, cols] = (y * scale).astype(o_ref.dtype)
        for r in range(rows_per_tile):
            blk = y[r * MOBA_BLOCK:(r + 1) * MOBA_BLOCK]
            mean_ref[0, r:r + 1, cols] = jnp.mean(blk, axis=0, keepdims=True)


def _store_bf16_epilogue(j, i, acc_refs, extra_refs, out_refs, scratch):
    out_refs[0][...] = acc_refs[0][...].astype(out_refs[0].dtype)


def _glu_epilogue(j, i, acc_refs, extra_refs, out_refs, scratch):
    out_refs[0][...] = acc_refs[0][...] * jax.nn.sigmoid(acc_refs[1][...])


def _sigmoid_epilogue(j, i, acc_refs, extra_refs, out_refs, scratch):
    out_refs[0][...] = jax.nn.sigmoid(acc_refs[0][...])


def _merge_epilogue(j, i, acc_refs, extra_refs, out_refs, scratch):
    ga_ref, gc_ref = extra_refs
    merged = ga_ref[...] * acc_refs[0][...] + gc_ref[...] * acc_refs[1][...]
    out_refs[0][...] = merged.astype(out_refs[0].dtype)


def _residual_epilogue(j, i, acc_refs, extra_refs, out_refs, scratch):
    out_refs[0][...] = extra_refs[0][...] + acc_refs[0][...]


def _ffn_in_epilogue(j, i, acc_refs, extra_refs, out_refs, scratch, *, tm, tiles_per_seq, width):
    cwg_ref, cwu_ref, cbg_ref, cbu_ref = extra_refs
    pad = SUBLANES
    seq_start = i % tiles_per_seq == 0
    for hp, carry in zip(acc_refs, scratch):
        prev = carry[...]
        hp[:, 0:pad, :] = jnp.where(seq_start, jnp.zeros_like(prev), prev)
        carry[...] = hp[:, tm:tm + pad, :]
    for lt in range(acc_refs[0].shape[0]):
        lanes = slice(lt * LANES, (lt + 1) * LANES)
        for r in range(0, tm, FFN_ROW_CHUNK):
            convs = []
            for hp, cw_ref, cb_ref in zip(acc_refs, (cwg_ref, cwu_ref), (cbg_ref, cbu_ref)):
                y = cb_ref[:, lanes]
                for w in range(width):
                    lo = pad + r - (width - 1 - w)
                    y = y + cw_ref[w:w + 1, lanes] * hp[lt, lo:lo + FFN_ROW_CHUNK, :]
                convs.append(y)
            gate, up = convs
            out_refs[0][r:r + FFN_ROW_CHUNK, lanes] = (
                jax.nn.silu(gate) * up).astype(out_refs[0].dtype)


def _attn_kernel(q_ref, k_ref, v_ref, km_ref, o_ref, *, n_blocks):
    L = MOBA_BLOCK
    nt = (((1,), (1,)), ((), ()))
    tn = (((0,), (0,)), ((), ()))
    km = km_ref[...].astype(BF16)
    neg = jnp.float32(-jnp.inf)
    key = lax.broadcasted_iota(jnp.int32, (L, L), 0)
    qry = lax.broadcasted_iota(jnp.int32, (L, L), 1)
    causal = key <= qry
    blk = lax.broadcasted_iota(jnp.int32, (n_blocks, L), 0)
    for i in range(n_blocks):
        qi = q_ref[i * L:(i + 1) * L, :]
        n_keys = (i + 1) * L
        st = lax.dot_general(k_ref[0:n_keys, :], qi, nt, preferred_element_type=F32)
        pen = None
        if i > MOBA_TOP_K:
            gate = lax.dot_general(km, qi, nt, preferred_element_type=F32)
            rank = jnp.zeros((n_blocks, L), F32)
            for jp in range(i):
                gb = jnp.broadcast_to(gate[jp:jp + 1, :], (n_blocks, L))
                beats = (gb > gate) | ((gb == gate) & (jp < blk))
                rank = rank + jnp.where(beats, 1.0, 0.0)
            pen = jnp.where(rank < MOBA_TOP_K, 0.0, neg)
        chunks = []
        for jb in range(i + 1):
            sj = st[jb * L:(jb + 1) * L, :]
            if jb == i:
                sj = jnp.where(causal, sj, neg)
            elif pen is not None:
                sj = sj + pen[jb:jb + 1, :]
            chunks.append(sj)
        top = chunks[0]
        for c in chunks[1:]:
            top = jnp.maximum(top, c)
        m = top.max(axis=0, keepdims=True)
        es = [jnp.exp(c - m) for c in chunks]
        tot = es[0]
        for e in es[1:]:
            tot = tot + e
        inv_l = 1.0 / tot.sum(axis=0, keepdims=True)
        pt = jnp.concatenate([(e * inv_l).astype(BF16) for e in es], axis=0)
        ot = lax.dot_general(v_ref[0:n_keys, :], pt, tn, preferred_element_type=F32)
        o_ref[i * L:(i + 1) * L, :] = ot.T.astype(o_ref.dtype)


def _moba_attention(qk, v, kmean, batch, seq, n_heads):
    t = qk.shape[0]
    n_blocks = seq // MOBA_BLOCK
    return pl.pallas_call(
        functools.partial(_attn_kernel, n_blocks=n_blocks),
        grid=(batch, n_heads),
        in_specs=[pl.BlockSpec((seq, HEAD_DIM), lambda b, h: (b, h)),
                  pl.BlockSpec((seq, HEAD_DIM), lambda b, h: (b, n_heads + h)),
                  pl.BlockSpec((seq, HEAD_DIM), lambda b, h: (b, h)),
                  pl.BlockSpec((n_blocks, HEAD_DIM), lambda b, h: (b, n_heads + h))],
        out_specs=pl.BlockSpec((seq, HEAD_DIM), lambda b, h: (b, h)),
        out_shape=jax.ShapeDtypeStruct((t, n_heads * HEAD_DIM), BF16),
        compiler_params=_params(2),
        name="moba_attention",
    )(qk, qk, v, kmean)


def _conv_kernel(h_ref, halo_ref, cw_ref, cb_ref, g_ref, b_ref, o_ref, xpad_ref, y_ref, z_ref,
                 *, ts, halo, width):
    t = pl.program_id(1)
    prev = halo_ref[...]
    xpad_ref[0:halo, :] = jnp.where(t == 0, jnp.zeros_like(prev), prev)
    xpad_ref[halo:, :] = h_ref[...]
    n_chunks = h_ref.shape[1] // LANES
    base = halo - (width - 1)

    def chunk(c, carry):
        cols = pl.ds(pl.multiple_of(c * LANES, LANES), LANES)
        groups = []
        for r in range(SUBLANES):
            taps = [w for w in range(width) if (base + w) % SUBLANES == r]
            if taps:
                span = taps[-1] - taps[0] + ts
                z_ref[r, 0:span, :] = xpad_ref[base + taps[0]:base + taps[0] + span, cols]
                groups.append((r, taps))
        for r0 in range(0, ts, CONV_ROW_GROUP):
            acc = jnp.broadcast_to(cb_ref[:, cols], (CONV_ROW_GROUP, LANES))
            for r, taps in groups:
                for w in taps:
                    off = r0 + w - taps[0]
                    acc = acc + cw_ref[w:w + 1, cols] * z_ref[r, off:off + CONV_ROW_GROUP, :]
            y_ref[r0:r0 + CONV_ROW_GROUP, cols] = acc
        return carry

    lax.fori_loop(0, n_chunks, chunk, 0)
    y = y_ref[...]
    mu = jnp.mean(y, axis=-1, keepdims=True)
    yc = y - mu
    var = jnp.mean(yc * yc, axis=-1, keepdims=True)
    z = yc * lax.rsqrt(var + EPS) * g_ref[...] + b_ref[...]
    o_ref[...] = jax.nn.silu(z).astype(o_ref.dtype)


def _conv_module(h, conv_w, conv_b, ln_g, ln_b, batch, seq, ts=256, halo=32):
    t, c = h.shape
    width = conv_w.shape[0]
    tiles = seq // ts
    halo_per_tile = ts // halo

    def halo_map(b, s):
        return (jnp.maximum((b * tiles + s) * halo_per_tile - 1, 0), 0)

    vec = lambda: pl.BlockSpec((1, c), lambda b, s: (0, 0))
    return pl.pallas_call(
        functools.partial(_conv_kernel, ts=ts, halo=halo, width=width),
        grid=(batch, tiles),
        in_specs=[pl.BlockSpec((ts, c), lambda b, s: (b * tiles + s, 0)),
                  pl.BlockSpec((halo, c), halo_map),
                  pl.BlockSpec((width, c), lambda b, s: (0, 0)),
                  vec(), vec(), vec()],
        out_specs=pl.BlockSpec((ts, c), lambda b, s: (b * tiles + s, 0)),
        out_shape=jax.ShapeDtypeStruct((t, c), BF16),
        scratch_shapes=[pltpu.VMEM((halo + ts, c), F32), pltpu.VMEM((ts, c), F32),
                        pltpu.VMEM((SUBLANES, ts + halo, LANES), F32)],
        compiler_params=_params(2),
        name="conv_module",
    )(h, h, conv_w, conv_b.reshape(1, c), ln_g.reshape(1, c), ln_b.reshape(1, c))


def _layer(x, positions, norm1_g, w_in, q_norm_g, k_norm_g, w_o_attn, conv_w, conv_b,
           conv_ln_g, conv_ln_b, w_o_conv, w_out, norm2_g, w_ffn_in, ffn_conv_w,
           ffn_conv_b, w_ffn_out):
    batch, seq, d_model = x.shape
    t = batch * seq
    attn_width = w_o_attn.shape[0]
    conv_dim = w_o_conv.shape[0]
    d_ff = w_ffn_out.shape[0]
    n_heads = attn_width // HEAD_DIM
    assert seq % MOBA_BLOCK == 0
    assert w_in.shape[1] == 3 * attn_width + 2 * conv_dim + 2 * d_model

    x2 = x.reshape(t, d_model)
    xn = _rmsnorm(x2, norm1_g)
    cos, sin = _rope_tables(positions)

    tm, tn = 1024, 512
    qk_g = jnp.stack([q_norm_g, k_norm_g])
    n_q_tiles = attn_width // tn
    qk, kmean = _ws_matmul(
        "qk_proj", [xn], [(w_in, 0)], [(0, 0)], 2 * n_q_tiles, tm, tn,
        functools.partial(_qk_epilogue, n_q_tiles=n_q_tiles, tm=tm, tn=tn), lag=1,
        extras=(cos, sin, qk_g),
        extra_specs=(((tm, HEAD_DIM), lambda j, i: (i, 0)),
                     ((tm, HEAD_DIM), lambda j, i: (i, 0)),
                     ((2, HEAD_DIM), lambda j, i: (0, 0))),
        out_shapes=(jax.ShapeDtypeStruct((t, 2 * attn_width), BF16),
                    jax.ShapeDtypeStruct((t // tm, tm // MOBA_BLOCK, 2 * attn_width), F32)),
        out_specs=(_tile_spec(tm, tn),
                   ((1, tm // MOBA_BLOCK, tn), lambda j, i: (i, 0, j))))
    kmean = kmean.reshape(t // MOBA_BLOCK, 2 * attn_width)
    (v,) = _ws_matmul(
        "v_proj", [xn], [(w_in, 2 * attn_width // tn)], [(0, 0)], attn_width // tn, tm, tn,
        _store_bf16_epilogue,
        out_shapes=(jax.ShapeDtypeStruct((t, attn_width), BF16),),
        out_specs=(_tile_spec(tm, tn),))
    tg = 256
    u_off = 3 * attn_width
    (h,) = _ws_matmul(
        "glu_proj", [xn], [(w_in, u_off // tg), (w_in, (u_off + conv_dim) // tg)],
        [(0, 0), (0, 1)], conv_dim // tg, tm, tg, _glu_epilogue,
        out_shapes=(jax.ShapeDtypeStruct((t, conv_dim), F32),),
        out_specs=(_tile_spec(tm, tg),))
    g_off = u_off + 2 * conv_dim
    (gates,) = _ws_matmul(
        "gate_proj", [xn], [(w_in, g_off // tn)], [(0, 0)], 2 * d_model // tn, tm, tn,
        _sigmoid_epilogue,
        out_shapes=(jax.ShapeDtypeStruct((t, 2 * d_model), F32),),
        out_specs=(_tile_spec(tm, tn),))

    attn = _moba_attention(qk, v, kmean, batch, seq, n_heads)
    hc = _conv_module(h, conv_w, conv_b, conv_ln_g, conv_ln_b, batch, seq)

    n_model_tiles = d_model // tn
    tmm = 1024
    (merged,) = _ws_matmul(
        "merge_proj", [attn, hc], [(w_o_attn, 0), (w_o_conv, 0)], [(0, 0), (1, 1)],
        n_model_tiles, tmm, tn, _merge_epilogue,
        extras=(gates, gates),
        extra_specs=(_tile_spec(tmm, tn), _tile_spec(tmm, tn, n_model_tiles)),
        out_shapes=(jax.ShapeDtypeStruct((t, d_model), BF16),),
        out_specs=(_tile_spec(tmm, tn),))
    (x1,) = _ws_matmul(
        "out_proj", [merged], [(w_out, 0)], [(0, 0)], n_model_tiles, tm, tn,
        _residual_epilogue,
        extras=(x2,), extra_specs=(_tile_spec(tm, tn),),
        out_shapes=(jax.ShapeDtypeStruct((t, d_model), F32),),
        out_specs=(_tile_spec(tm, tn),))

    xn2 = _rmsnorm(x1, norm2_g)
    tf = 256
    assert d_ff % tf == 0 and seq % tm == 0
    n_ff_tiles = d_ff // tf
    fcw = ffn_conv_w
    fcb = ffn_conv_b.reshape(1, 2 * d_ff)
    width = fcw.shape[0]
    (act,) = _ws_matmul(
        "ffn_in", [xn2], [(w_ffn_in, 0), (w_ffn_in, n_ff_tiles)], [(0, 0), (0, 1)],
        n_ff_tiles, tm, tf,
        functools.partial(_ffn_in_epilogue, tm=tm, tiles_per_seq=seq // tm, width=width),
        extras=(fcw, fcw, fcb, fcb),
        extra_specs=(((width, tf), lambda j, i: (0, j)),
                     ((width, tf), lambda j, i: (0, j + n_ff_tiles)),
                     ((1, tf), lambda j, i: (0, j)),
                     ((1, tf), lambda j, i: (0, j + n_ff_tiles))),
        out_shapes=(jax.ShapeDtypeStruct((t, d_ff), BF16),),
        out_specs=(_tile_spec(tm, tf),),
        scratch=[pltpu.VMEM((tf // LANES, SUBLANES, LANES), F32)] * 2,
        pad=SUBLANES, lane_split=True)

    tmo = 512
    (out,) = _ws_matmul(
        "ffn_out", [act], [(w_ffn_out.astype(BF16), 0)], [(0, 0)], n_model_tiles, tmo, tn,
        _residual_epilogue,
        extras=(x1,), extra_specs=(_tile_spec(tmo, tn),),
        out_shapes=(jax.ShapeDtypeStruct((t, d_model), F32),),
        out_specs=(_tile_spec(tmo, tn),))
    return out.reshape(batch, seq, d_model)


def kernel(x, positions, norm1_g, w_in, q_norm_g, k_norm_g, w_o_attn, conv_w, conv_b,
           conv_ln_g, conv_ln_b, w_o_conv, w_out, norm2_g, w_ffn_in, ffn_conv_w,
           ffn_conv_b, w_ffn_out):
    depth = norm1_g.shape[0]
    params = (norm1_g, w_in, q_norm_g, k_norm_g, w_o_attn, conv_w, conv_b, conv_ln_g,
              conv_ln_b, w_o_conv, w_out, norm2_g, w_ffn_in, ffn_conv_w, ffn_conv_b, w_ffn_out)
    for l in range(depth):
        layer = [p.reshape(p.shape[1:]) if depth == 1 else p[l] for p in params]
        x = _layer(x, positions, *layer)
    return x
```

```python
import functools

import jax
import jax.numpy as jnp
from jax import lax
from jax.experimental import pallas as pl
from jax.experimental.pallas import tpu as pltpu

HEAD_DIM = 128
MOBA_BLOCK = 256
MOBA_TOP_K = 3
ROPE_THETA = 10000.0
EPS = 1e-6

LANES = 128
SUBLANES = 8
VMEM_LIMIT_BYTES = 56 * 1024 * 1024
FFN_ROW_CHUNK = 64
CONV_ROW_GROUP = 128
MXU_ROW_BLOCK = 128

F32 = jnp.float32
BF16 = jnp.bfloat16


def _params(n_grid_dims):
    return pltpu.CompilerParams(
        dimension_semantics=("arbitrary",) * n_grid_dims,
        vmem_limit_bytes=VMEM_LIMIT_BYTES)


def _rmsnorm_kernel(x_ref, g_ref, o_ref):
    x = x_ref[...]
    ms = jnp.mean(x * x, axis=-1, keepdims=True)
    o_ref[...] = (x * lax.rsqrt(ms + EPS) * g_ref[...]).astype(o_ref.dtype)


def _rmsnorm(x, g, tr=512):
    t, d = x.shape
    return pl.pallas_call(
        _rmsnorm_kernel,
        grid=(t // tr,),
        in_specs=[pl.BlockSpec((tr, d), lambda i: (i, 0)),
                  pl.BlockSpec((1, d), lambda i: (0, 0))],
        out_specs=pl.BlockSpec((tr, d), lambda i: (i, 0)),
        out_shape=jax.ShapeDtypeStruct((t, d), BF16),
        compiler_params=_params(1),
        name="rmsnorm",
    )(x, g.reshape(1, d))


def _rope_table_kernel(pos_ref, invf_ref, sign_ref, cos_ref, sin_ref):
    ang = pos_ref[...] * invf_ref[...]
    cos_ref[...] = jnp.cos(ang)
    sin_ref[...] = jnp.sin(ang) * sign_ref[...]


def _rope_tables(positions, tr=1024):
    t = positions.size
    half = HEAD_DIM // 2
    inv = ROPE_THETA ** (-jnp.arange(half, dtype=F32) / half)
    invf = jnp.concatenate([inv, inv]).reshape(1, HEAD_DIM)
    sign = jnp.concatenate([-jnp.ones((half,), F32), jnp.ones((half,), F32)]).reshape(1, HEAD_DIM)
    pos = positions.astype(F32).reshape(t, 1)
    return pl.pallas_call(
        _rope_table_kernel,
        grid=(t // tr,),
        in_specs=[pl.BlockSpec((tr, 1), lambda i: (i, 0)),
                  pl.BlockSpec((1, HEAD_DIM), lambda i: (0, 0)),
                  pl.BlockSpec((1, HEAD_DIM), lambda i: (0, 0))],
        out_specs=[pl.BlockSpec((tr, HEAD_DIM), lambda i: (i, 0))] * 2,
        out_shape=[jax.ShapeDtypeStruct((t, HEAD_DIM), F32)] * 2,
        compiler_params=_params(1),
        name="rope_tables",
    )(pos, invf, sign)


def _ws_kernel(*refs, n_a, n_w, dots, n_extra, n_out, epilogue, n_row_tiles, n_tiles, pad, lag,
               lane_split, m_split):
    a_refs = refs[:n_a]
    w_refs = refs[n_a:n_a + n_w]
    extra_refs = refs[n_a + n_w:n_a + n_w + n_extra]
    out_refs = refs[n_a + n_w + n_extra:n_a + n_w + n_extra + n_out]
    scratch = refs[n_a + n_w + n_extra + n_out:]
    to_cast = [w_ref for w_ref in w_refs if w_ref.dtype != BF16]
    cast_refs = scratch[:len(to_cast)]
    n_slots = 1 + lag
    n_raw = n_slots * len(dots)
    raw_refs = scratch[len(to_cast):len(to_cast) + n_raw]
    user_scratch = scratch[len(to_cast) + n_raw:]
    s = pl.program_id(0)
    tile = jnp.minimum(s, n_tiles - 1)

    if to_cast:
        @pl.when((tile % n_row_tiles == 0) & (s < n_tiles))
        def _cast_weights():
            for w_ref, wbf_ref in zip(to_cast, cast_refs):
                wbf_ref[...] = w_ref[...].astype(BF16)

    if lag or user_scratch:
        @pl.when(s == 0)
        def _init():
            if lag:
                for raw in raw_refs[1::2]:
                    raw[...] = jnp.zeros(raw.shape, F32)
            for u in user_scratch:
                u[...] = jnp.zeros(u.shape, u.dtype)

    cast_iter = iter(cast_refs)
    wbf_refs = [w_ref if w_ref.dtype == BF16 else next(cast_iter) for w_ref in w_refs]
    done = jnp.maximum(s - lag, 0)

    def step(slot):
        tm = a_refs[0].shape[0]
        rb = tm // m_split
        for k in range(m_split):
            rows = (k * rb, (k + 1) * rb)
            for d, (ai, wi) in enumerate(dots):
                res = jnp.dot(a_refs[ai][rows[0]:rows[1], :], wbf_refs[wi][...],
                              preferred_element_type=F32)
                raw = raw_refs[n_slots * d + slot]
                if lane_split:
                    for lt in range(raw.shape[0]):
                        raw[lt, pad + rows[0]:pad + rows[1], :] = res[:, lt * LANES:(lt + 1) * LANES]
                else:
                    raw[pad + rows[0]:pad + rows[1], :] = res
            epilogue(done // n_row_tiles, done % n_row_tiles,
                     [raw_refs[n_slots * d + (slot + lag) % n_slots] for d in range(len(dots))],
                     extra_refs, out_refs, user_scratch, rows)

    if lag:
        for slot in range(2):
            pl.when(s % 2 == slot)(functools.partial(step, slot))
    else:
        step(0)


def _ws_matmul(name, a_list, w_list, dots, n_col_tiles, tm, tn, epilogue,
               extras=(), extra_specs=(), out_shapes=(), out_specs=(), scratch=(), pad=0, lag=0,
               lane_split=False, m_split=1):
    t = a_list[0].shape[0]
    n_row_tiles = t // tm
    n_tiles = n_col_tiles * n_row_tiles

    def dot_tile(s):
        return jnp.minimum(s, n_tiles - 1)

    def lagged(index_map):
        def wrapped(s):
            done = jnp.maximum(s - lag, 0)
            return index_map(done // n_row_tiles, done % n_row_tiles)
        return wrapped

    in_specs = [pl.BlockSpec((tm, a.shape[1]), lambda s: (dot_tile(s) % n_row_tiles, 0))
                for a in a_list]
    for w, off in w_list:
        in_specs.append(pl.BlockSpec((w.shape[0], tn), functools.partial(
            lambda s, off: (0, dot_tile(s) // n_row_tiles + off), off=off)))
    in_specs.extend(pl.BlockSpec(shape, lagged(fn)) for shape, fn in extra_specs)
    wbf_scratch = [pltpu.VMEM((w.shape[0], tn), BF16) for w, _ in w_list if w.dtype != BF16]
    raw_shape = (tn // LANES, pad + tm, LANES) if lane_split else (pad + tm, tn)
    raw_scratch = [pltpu.VMEM(raw_shape, F32) for _ in range((1 + lag) * len(dots))]
    kernel = functools.partial(
        _ws_kernel, n_a=len(a_list), n_w=len(w_list), dots=tuple(dots),
        n_extra=len(extras), n_out=len(out_shapes), epilogue=epilogue,
        n_row_tiles=n_row_tiles, n_tiles=n_tiles, pad=pad, lag=lag, lane_split=lane_split,
        m_split=m_split)
    return pl.pallas_call(
        kernel,
        grid=(n_tiles + lag,),
        in_specs=in_specs,
        out_specs=[pl.BlockSpec(shape, lagged(fn)) for shape, fn in out_specs],
        out_shape=list(out_shapes),
        scratch_shapes=wbf_scratch + raw_scratch + list(scratch),
        compiler_params=_params(1),
        name=name,
    )(*a_list, *[w for w, _ in w_list], *extras)


def _tile_spec(tm, tn, col_offset=0):
    return ((tm, tn), lambda j, i: (i, j + col_offset))


def _qk_epilogue(j, i, acc_refs, extra_refs, out_refs, scratch, rows, *, n_q_tiles, tm, tn):
    assert rows == (0, tm)
    cos_ref, sin_ref, g_ref = extra_refs
    o_ref, mean_ref = out_refs
    acc_ref = acc_refs[0]
    is_q = j < n_q_tiles
    g = jnp.where(is_q, g_ref[0:1, :], g_ref[1:2, :])
    scale = jnp.where(is_q, jnp.float32(HEAD_DIM ** -0.5), jnp.float32(1.0))
    cos = cos_ref[...]
    sin = sin_ref[...]
    rows_per_tile = tm // MOBA_BLOCK
    for h in range(tn // HEAD_DIM):
        cols = slice(h * HEAD_DIM, (h + 1) * HEAD_DIM)
        x = acc_ref[:, cols]
        ms = jnp.mean(x * x, axis=-1, keepdims=True)
        y = x * lax.rsqrt(ms + EPS) * g
        y = y * cos + pltpu.roll(y, HEAD_DIM // 2, 1) * sin
        o_ref[:, cols] = (y * scale).astype(o_ref.dtype)
        for r in range(rows_per_tile):
            blk = y[r * MOBA_BLOCK:(r + 1) * MOBA_BLOCK]
            mean_ref[0, r:r + 1, cols] = jnp.mean(blk, axis=0, keepdims=True)


def _store_bf16_epilogue(j, i, acc_refs, extra_refs, out_refs, scratch, rows):
    r = slice(*rows)
    out_refs[0][r, :] = acc_refs[0][r, :].astype(out_refs[0].dtype)


def _glu_epilogue(j, i, acc_refs, extra_refs, out_refs, scratch, rows):
    r = slice(*rows)
    out_refs[0][r, :] = acc_refs[0][r, :] * jax.nn.sigmoid(acc_refs[1][r, :])


def _sigmoid_epilogue(j, i, acc_refs, extra_refs, out_refs, scratch, rows):
    r = slice(*rows)
    out_refs[0][r, :] = jax.nn.sigmoid(acc_refs[0][r, :])


def _with_side_cast(epilogue):
    def wrapped(j, i, acc_refs, extra_refs, out_refs, scratch, rows):
        if rows[0] == 0:
            out_refs[-1][...] = extra_refs[-1][...].astype(out_refs[-1].dtype)
        epilogue(j, i, acc_refs, extra_refs[:-1], out_refs[:-1], scratch, rows)
    return wrapped


def _merge_epilogue(j, i, acc_refs, extra_refs, out_refs, scratch, rows):
    ga_ref, gc_ref = extra_refs
    r = slice(*rows)
    merged = ga_ref[r, :] * acc_refs[0][r, :] + gc_ref[r, :] * acc_refs[1][r, :]
    out_refs[0][r, :] = merged.astype(out_refs[0].dtype)


def _residual_epilogue(j, i, acc_refs, extra_refs, out_refs, scratch, rows):
    r = slice(*rows)
    out_refs[0][r, :] = extra_refs[0][r, :] + acc_refs[0][r, :]


def _ffn_in_epilogue(j, i, acc_refs, extra_refs, out_refs, scratch, rows, *, tm, tiles_per_seq,
                     width):
    cwg_ref, cwu_ref, cbg_ref, cbu_ref = extra_refs
    pad = SUBLANES
    seq_start = i % tiles_per_seq == 0
    if rows[0] == 0:
        for hp, carry in zip(acc_refs, scratch):
            prev = carry[...]
            hp[:, 0:pad, :] = jnp.where(seq_start, jnp.zeros_like(prev), prev)
    for lt in range(acc_refs[0].shape[0]):
        lanes = slice(lt * LANES, (lt + 1) * LANES)
        for r in range(rows[0], rows[1], FFN_ROW_CHUNK):
            convs = []
            for hp, cw_ref, cb_ref in zip(acc_refs, (cwg_ref, cwu_ref), (cbg_ref, cbu_ref)):
                y = cb_ref[:, lanes]
                for w in range(width):
                    lo = pad + r - (width - 1 - w)
                    y = y + cw_ref[w:w + 1, lanes] * hp[lt, lo:lo + FFN_ROW_CHUNK, :]
                convs.append(y)
            gate, up = convs
            out_refs[0][r:r + FFN_ROW_CHUNK, lanes] = (
                jax.nn.silu(gate) * up).astype(out_refs[0].dtype)
    if rows[1] == tm:
        for hp, carry in zip(acc_refs, scratch):
            carry[...] = hp[:, tm:tm + pad, :]


def _attn_kernel(q_ref, k_ref, v_ref, km_ref, o_ref, *, n_blocks):
    L = MOBA_BLOCK
    nt = (((1,), (1,)), ((), ()))
    tn = (((0,), (0,)), ((), ()))
    km = km_ref[...].astype(BF16)
    neg = jnp.float32(-jnp.inf)
    key = lax.broadcasted_iota(jnp.int32, (L, L), 0)
    qry = lax.broadcasted_iota(jnp.int32, (L, L), 1)
    causal = key <= qry
    blk = lax.broadcasted_iota(jnp.int32, (n_blocks, L), 0)
    for i in range(n_blocks):
        qi = q_ref[i * L:(i + 1) * L, :]
        n_keys = (i + 1) * L
        st = lax.dot_general(k_ref[0:n_keys, :], qi, nt, preferred_element_type=F32)
        pen = None
        if i > MOBA_TOP_K:
            gate = lax.dot_general(km, qi, nt, preferred_element_type=F32)
            rank = jnp.zeros((n_blocks, L), F32)
            for jp in range(i):
                gb = jnp.broadcast_to(gate[jp:jp + 1, :], (n_blocks, L))
                beats = (gb > gate) | ((gb == gate) & (jp < blk))
                rank = rank + jnp.where(beats, 1.0, 0.0)
            pen = jnp.where(rank < MOBA_TOP_K, 0.0, neg)
        chunks = []
        for jb in range(i + 1):
            sj = st[jb * L:(jb + 1) * L, :]
            if jb == i:
                sj = jnp.where(causal, sj, neg)
            elif pen is not None:
                sj = sj + pen[jb:jb + 1, :]
            chunks.append(sj)
        top = chunks[0]
        for c in chunks[1:]:
            top = jnp.maximum(top, c)
        m = top.max(axis=0, keepdims=True)
        es = [jnp.exp(c - m) for c in chunks]
        tot = es[0]
        for e in es[1:]:
            tot = tot + e
        inv_l = 1.0 / tot.sum(axis=0, keepdims=True)
        pt = jnp.concatenate([(e * inv_l).astype(BF16) for e in es], axis=0)
        ot = lax.dot_general(v_ref[0:n_keys, :], pt, tn, preferred_element_type=F32)
        o_ref[i * L:(i + 1) * L, :] = ot.T.astype(o_ref.dtype)


def _moba_attention(qk, v, kmean, batch, seq, n_heads):
    t = qk.shape[0]
    n_blocks = seq // MOBA_BLOCK
    return pl.pallas_call(
        functools.partial(_attn_kernel, n_blocks=n_blocks),
        grid=(batch, n_heads),
        in_specs=[pl.BlockSpec((seq, HEAD_DIM), lambda b, h: (b, h)),
                  pl.BlockSpec((seq, HEAD_DIM), lambda b, h: (b, n_heads + h)),
                  pl.BlockSpec((seq, HEAD_DIM), lambda b, h: (b, h)),
                  pl.BlockSpec((n_blocks, HEAD_DIM), lambda b, h: (b, n_heads + h))],
        out_specs=pl.BlockSpec((seq, HEAD_DIM), lambda b, h: (b, h)),
        out_shape=jax.ShapeDtypeStruct((t, n_heads * HEAD_DIM), BF16),
        compiler_params=_params(2),
        name="moba_attention",
    )(qk, qk, v, kmean)


def _conv_kernel(h_ref, halo_ref, cw_ref, cb_ref, g_ref, b_ref, o_ref, xpad_ref, y_ref, z_ref,
                 *, ts, halo, width):
    t = pl.program_id(1)
    prev = halo_ref[...]
    xpad_ref[0:halo, :] = jnp.where(t == 0, jnp.zeros_like(prev), prev)
    xpad_ref[halo:, :] = h_ref[...]
    n_chunks = h_ref.shape[1] // LANES
    base = halo - (width - 1)

    def chunk(c, carry):
        cols = pl.ds(pl.multiple_of(c * LANES, LANES), LANES)
        groups = []
        for r in range(SUBLANES):
            taps = [w for w in range(width) if (base + w) % SUBLANES == r]
            if taps:
                span = taps[-1] - taps[0] + ts
                z_ref[r, 0:span, :] = xpad_ref[base + taps[0]:base + taps[0] + span, cols]
                groups.append((r, taps))
        for r0 in range(0, ts, CONV_ROW_GROUP):
            acc = jnp.broadcast_to(cb_ref[:, cols], (CONV_ROW_GROUP, LANES))
            for r, taps in groups:
                for w in taps:
                    off = r0 + w - taps[0]
                    acc = acc + cw_ref[w:w + 1, cols] * z_ref[r, off:off + CONV_ROW_GROUP, :]
            y_ref[r0:r0 + CONV_ROW_GROUP, cols] = acc
        return carry

    lax.fori_loop(0, n_chunks, chunk, 0)
    y = y_ref[...]
    mu = jnp.mean(y, axis=-1, keepdims=True)
    yc = y - mu
    var = jnp.mean(yc * yc, axis=-1, keepdims=True)
    z = yc * lax.rsqrt(var + EPS) * g_ref[...] + b_ref[...]
    o_ref[...] = jax.nn.silu(z).astype(o_ref.dtype)


def _conv_module(h, conv_w, conv_b, ln_g, ln_b, batch, seq, ts=256, halo=32):
    t, c = h.shape
    width = conv_w.shape[0]
    tiles = seq // ts
    halo_per_tile = ts // halo

    def halo_map(b, s):
        return (jnp.maximum((b * tiles + s) * halo_per_tile - 1, 0), 0)

    vec = lambda: pl.BlockSpec((1, c), lambda b, s: (0, 0))
    return pl.pallas_call(
        functools.partial(_conv_kernel, ts=ts, halo=halo, width=width),
        grid=(batch, tiles),
        in_specs=[pl.BlockSpec((ts, c), lambda b, s: (b * tiles + s, 0)),
                  pl.BlockSpec((halo, c), halo_map),
                  pl.BlockSpec((width, c), lambda b, s: (0, 0)),
                  vec(), vec(), vec()],
        out_specs=pl.BlockSpec((ts, c), lambda b, s: (b * tiles + s, 0)),
        out_shape=jax.ShapeDtypeStruct((t, c), BF16),
        scratch_shapes=[pltpu.VMEM((halo + ts, c), F32), pltpu.VMEM((ts, c), F32),
                        pltpu.VMEM((SUBLANES, ts + halo, LANES), F32)],
        compiler_params=_params(2),
        name="conv_module",
    )(h, h, conv_w, conv_b.reshape(1, c), ln_g.reshape(1, c), ln_b.reshape(1, c))


def _layer(x, positions, norm1_g, w_in, q_norm_g, k_norm_g, w_o_attn, conv_w, conv_b,
           conv_ln_g, conv_ln_b, w_o_conv, w_out, norm2_g, w_ffn_in, ffn_conv_w,
           ffn_conv_b, w_ffn_out):
    batch, seq, d_model = x.shape
    t = batch * seq
    attn_width = w_o_attn.shape[0]
    conv_dim = w_o_conv.shape[0]
    d_ff = w_ffn_out.shape[0]
    n_heads = attn_width // HEAD_DIM
    assert seq % MOBA_BLOCK == 0
    assert w_in.shape[1] == 3 * attn_width + 2 * conv_dim + 2 * d_model

    x2 = x.reshape(t, d_model)
    xn = _rmsnorm(x2, norm1_g)
    cos, sin = _rope_tables(positions)

    tm, tn = 1024, 512
    qk_g = jnp.stack([q_norm_g, k_norm_g])
    n_q_tiles = attn_width // tn
    qk, kmean = _ws_matmul(
        "qk_proj", [xn], [(w_in, 0)], [(0, 0)], 2 * n_q_tiles, tm, tn,
        functools.partial(_qk_epilogue, n_q_tiles=n_q_tiles, tm=tm, tn=tn), lag=1,
        extras=(cos, sin, qk_g),
        extra_specs=(((tm, HEAD_DIM), lambda j, i: (i, 0)),
                     ((tm, HEAD_DIM), lambda j, i: (i, 0)),
                     ((2, HEAD_DIM), lambda j, i: (0, 0))),
        out_shapes=(jax.ShapeDtypeStruct((t, 2 * attn_width), BF16),
                    jax.ShapeDtypeStruct((t // tm, tm // MOBA_BLOCK, 2 * attn_width), F32)),
        out_specs=(_tile_spec(tm, tn),
                   ((1, tm // MOBA_BLOCK, tn), lambda j, i: (i, 0, j))))
    kmean = kmean.reshape(t // MOBA_BLOCK, 2 * attn_width)
    (v,) = _ws_matmul(
        "v_proj", [xn], [(w_in, 2 * attn_width // tn)], [(0, 0)], attn_width // tn, tm, tn,
        _store_bf16_epilogue, lag=1, m_split=tm // MXU_ROW_BLOCK,
        out_shapes=(jax.ShapeDtypeStruct((t, attn_width), BF16),),
        out_specs=(_tile_spec(tm, tn),))
    tg = 256
    u_off = 3 * attn_width
    (h,) = _ws_matmul(
        "glu_proj", [xn], [(w_in, u_off // tg), (w_in, (u_off + conv_dim) // tg)],
        [(0, 0), (0, 1)], conv_dim // tg, tm, tg, _glu_epilogue,
        lag=1, m_split=tm // MXU_ROW_BLOCK,
        out_shapes=(jax.ShapeDtypeStruct((t, conv_dim), F32),),
        out_specs=(_tile_spec(tm, tg),))
    g_off = u_off + 2 * conv_dim
    (gates,) = _ws_matmul(
        "gate_proj", [xn], [(w_in, g_off // tn)], [(0, 0)], 2 * d_model // tn, tm, tn,
        _sigmoid_epilogue, lag=1, m_split=tm // MXU_ROW_BLOCK,
        out_shapes=(jax.ShapeDtypeStruct((t, 2 * d_model), F32),),
        out_specs=(_tile_spec(tm, tn),))

    attn = _moba_attention(qk, v, kmean, batch, seq, n_heads)
    hc = _conv_module(h, conv_w, conv_b, conv_ln_g, conv_ln_b, batch, seq)

    n_model_tiles = d_model // tn
    tmm = 1024
    (merged,) = _ws_matmul(
        "merge_proj", [attn, hc], [(w_o_attn, 0), (w_o_conv, 0)], [(0, 0), (1, 1)],
        n_model_tiles, tmm, tn, _merge_epilogue,
        extras=(gates, gates),
        extra_specs=(_tile_spec(tmm, tn), _tile_spec(tmm, tn, n_model_tiles)),
        out_shapes=(jax.ShapeDtypeStruct((t, d_model), BF16),),
        out_specs=(_tile_spec(tmm, tn),))
    (x1,) = _ws_matmul(
        "out_proj", [merged], [(w_out, 0)], [(0, 0)], n_model_tiles, tm, tn,
        _residual_epilogue, lag=1, m_split=tm // MXU_ROW_BLOCK,
        extras=(x2,), extra_specs=(_tile_spec(tm, tn),),
        out_shapes=(jax.ShapeDtypeStruct((t, d_model), F32),),
        out_specs=(_tile_spec(tm, tn),))

    xn2 = _rmsnorm(x1, norm2_g)
    tf = 256
    assert d_ff % tf == 0 and seq % tm == 0
    n_ff_tiles = d_ff // tf
    fcw = ffn_conv_w
    fcb = ffn_conv_b.reshape(1, 2 * d_ff)
    width = fcw.shape[0]
    n_ff_steps = n_ff_tiles * (t // tm)
    assert d_ff % (n_ff_steps * SUBLANES) == 0
    slab = ((d_ff // n_ff_steps, d_model), lambda j, i: (j * (t // tm) + i, 0))
    act, w_ffn_out_bf = _ws_matmul(
        "ffn_in", [xn2], [(w_ffn_in, 0), (w_ffn_in, n_ff_tiles)], [(0, 0), (0, 1)],
        n_ff_tiles, tm, tf,
        _with_side_cast(functools.partial(
            _ffn_in_epilogue, tm=tm, tiles_per_seq=seq // tm, width=width)),
        extras=(fcw, fcw, fcb, fcb, w_ffn_out),
        extra_specs=(((width, tf), lambda j, i: (0, j)),
                     ((width, tf), lambda j, i: (0, j + n_ff_tiles)),
                     ((1, tf), lambda j, i: (0, j)),
                     ((1, tf), lambda j, i: (0, j + n_ff_tiles)),
                     slab),
        out_shapes=(jax.ShapeDtypeStruct((t, d_ff), BF16),
                    jax.ShapeDtypeStruct((d_ff, d_model), BF16)),
        out_specs=(_tile_spec(tm, tf), slab),
        scratch=[pltpu.VMEM((tf // LANES, SUBLANES, LANES), F32)] * 2,
        pad=SUBLANES, lane_split=True, lag=1, m_split=tm // MXU_ROW_BLOCK)

    tmo = 512
    (out,) = _ws_matmul(
        "ffn_out", [act], [(w_ffn_out_bf, 0)], [(0, 0)], n_model_tiles, tmo, tn,
        _residual_epilogue, lag=1, m_split=tmo // MXU_ROW_BLOCK,
        extras=(x1,), extra_specs=(_tile_spec(tmo, tn),),
        out_shapes=(jax.ShapeDtypeStruct((t, d_model), F32),),
        out_specs=(_tile_spec(tmo, tn),))
    return out.reshape(batch, seq, d_model)


def kernel(x, positions, norm1_g, w_in, q_norm_g, k_norm_g, w_o_attn, conv_w, conv_b,
           conv_ln_g, conv_ln_b, w_o_conv, w_out, norm2_g, w_ffn_in, ffn_conv_w,
           ffn_conv_b, w_ffn_out):
    depth = norm1_g.shape[0]
    params = (norm1_g, w_in, q_norm_g, k_norm_g, w_o_attn, conv_w, conv_b, conv_ln_g,
              conv_ln_b, w_o_conv, w_out, norm2_g, w_ffn_in, ffn_conv_w, ffn_conv_b, w_ffn_out)
    for l in range(depth):
        layer = [p.reshape(p.shape[1:]) if depth == 1 else p[l] for p in params]
        x = _layer(x, positions, *layer)
    return x
```

```python
import functools

import jax
import jax.numpy as jnp
from jax import lax
from jax.experimental import pallas as pl
from jax.experimental.pallas import tpu as pltpu

HEAD_DIM = 128
MOBA_BLOCK = 256
MOBA_TOP_K = 3
ROPE_THETA = 10000.0
EPS = 1e-6

LANES = 128
SUBLANES = 8
VMEM_LIMIT_BYTES = 56 * 1024 * 1024
FFN_ROW_CHUNK = 64
CONV_ROW_GROUP = 128
MXU_ROW_BLOCK = 128

F32 = jnp.float32
BF16 = jnp.bfloat16


def _params(n_grid_dims):
    return pltpu.CompilerParams(
        dimension_semantics=("arbitrary",) * n_grid_dims,
        vmem_limit_bytes=VMEM_LIMIT_BYTES)


def _rmsnorm_kernel(x_ref, g_ref, o_ref):
    x = x_ref[...]
    ms = jnp.mean(x * x, axis=-1, keepdims=True)
    o_ref[...] = (x * lax.rsqrt(ms + EPS) * g_ref[...]).astype(o_ref.dtype)


def _rmsnorm(x, g, tr=512):
    t, d = x.shape
    return pl.pallas_call(
        _rmsnorm_kernel,
        grid=(t // tr,),
        in_specs=[pl.BlockSpec((tr, d), lambda i: (i, 0)),
                  pl.BlockSpec((1, d), lambda i: (0, 0))],
        out_specs=pl.BlockSpec((tr, d), lambda i: (i, 0)),
        out_shape=jax.ShapeDtypeStruct((t, d), BF16),
        compiler_params=_params(1),
        name="rmsnorm",
    )(x, g.reshape(1, d))


def _rope_table_kernel(pos_ref, invf_ref, sign_ref, cos_ref, sin_ref):
    ang = pos_ref[...] * invf_ref[...]
    cos_ref[...] = jnp.cos(ang)
    sin_ref[...] = jnp.sin(ang) * sign_ref[...]


def _rope_tables(positions, tr=1024):
    t = positions.size
    half = HEAD_DIM // 2
    inv = ROPE_THETA ** (-jnp.arange(half, dtype=F32) / half)
    invf = jnp.concatenate([inv, inv]).reshape(1, HEAD_DIM)
    sign = jnp.concatenate([-jnp.ones((half,), F32), jnp.ones((half,), F32)]).reshape(1, HEAD_DIM)
    pos = positions.astype(F32).reshape(t, 1)
    return pl.pallas_call(
        _rope_table_kernel,
        grid=(t // tr,),
        in_specs=[pl.BlockSpec((tr, 1), lambda i: (i, 0)),
                  pl.BlockSpec((1, HEAD_DIM), lambda i: (0, 0)),
                  pl.BlockSpec((1, HEAD_DIM), lambda i: (0, 0))],
        out_specs=[pl.BlockSpec((tr, HEAD_DIM), lambda i: (i, 0))] * 2,
        out_shape=[jax.ShapeDtypeStruct((t, HEAD_DIM), F32)] * 2,
        compiler_params=_params(1),
        name="rope_tables",
    )(pos, invf, sign)


def _ws_kernel(*refs, n_a, n_w, dots, n_extra, n_out, epilogue, n_row_tiles, n_tiles, pad, lag,
               lane_split, m_split):
    a_refs = refs[:n_a]
    w_refs = refs[n_a:n_a + n_w]
    extra_refs = refs[n_a + n_w:n_a + n_w + n_extra]
    out_refs = refs[n_a + n_w + n_extra:n_a + n_w + n_extra + n_out]
    scratch = refs[n_a + n_w + n_extra + n_out:]
    to_cast = [w_ref for w_ref in w_refs if w_ref.dtype != BF16]
    cast_refs = scratch[:len(to_cast)]
    n_slots = 1 + lag
    n_raw = n_slots * len(dots)
    raw_refs = scratch[len(to_cast):len(to_cast) + n_raw]
    user_scratch = scratch[len(to_cast) + n_raw:]
    s = pl.program_id(0)
    tile = jnp.minimum(s, n_tiles - 1)

    if to_cast:
        @pl.when((tile % n_row_tiles == 0) & (s < n_tiles))
        def _cast_weights():
            for w_ref, wbf_ref in zip(to_cast, cast_refs):
                wbf_ref[...] = w_ref[...].astype(BF16)

    if lag or user_scratch:
        @pl.when(s == 0)
        def _init():
            if lag:
                for raw in raw_refs[1::2]:
                    raw[...] = jnp.zeros(raw.shape, F32)
            for u in user_scratch:
                u[...] = jnp.zeros(u.shape, u.dtype)

    cast_iter = iter(cast_refs)
    wbf_refs = [w_ref if w_ref.dtype == BF16 else next(cast_iter) for w_ref in w_refs]
    done = jnp.maximum(s - lag, 0)

    def step(slot):
        tm = a_refs[0].shape[0]
        rb = tm // m_split
        for k in range(m_split):
            rows = (k * rb, (k + 1) * rb)
            for d, (ai, wi) in enumerate(dots):
                res = jnp.dot(a_refs[ai][rows[0]:rows[1], :], wbf_refs[wi][...],
                              preferred_element_type=F32)
                raw = raw_refs[n_slots * d + slot]
                if lane_split:
                    for lt in range(raw.shape[0]):
                        raw[lt, pad + rows[0]:pad + rows[1], :] = res[:, lt * LANES:(lt + 1) * LANES]
                else:
                    raw[pad + rows[0]:pad + rows[1], :] = res
            epilogue(done // n_row_tiles, done % n_row_tiles,
                     [raw_refs[n_slots * d + (slot + lag) % n_slots] for d in range(len(dots))],
                     extra_refs, out_refs, user_scratch, rows)

    if lag:
        for slot in range(2):
            pl.when(s % 2 == slot)(functools.partial(step, slot))
    else:
        step(0)


def _ws_matmul(name, a_list, w_list, dots, n_col_tiles, tm, tn, epilogue,
               extras=(), extra_specs=(), out_shapes=(), out_specs=(), scratch=(), pad=0, lag=0,
               lane_split=False, m_split=1):
    t = a_list[0].shape[0]
    n_row_tiles = t // tm
    n_tiles = n_col_tiles * n_row_tiles

    def dot_tile(s):
        return jnp.minimum(s, n_tiles - 1)

    def lagged(index_map):
        def wrapped(s):
            done = jnp.maximum(s - lag, 0)
            return index_map(done // n_row_tiles, done % n_row_tiles)
        return wrapped

    in_specs = [pl.BlockSpec((tm, a.shape[1]), lambda s: (dot_tile(s) % n_row_tiles, 0))
                for a in a_list]
    for w, off in w_list:
        in_specs.append(pl.BlockSpec((w.shape[0], tn), functools.partial(
            lambda s, off: (0, dot_tile(s) // n_row_tiles + off), off=off)))
    in_specs.extend(pl.BlockSpec(shape, lagged(fn)) for shape, fn in extra_specs)
    wbf_scratch = [pltpu.VMEM((w.shape[0], tn), BF16) for w, _ in w_list if w.dtype != BF16]
    raw_shape = (tn // LANES, pad + tm, LANES) if lane_split else (pad + tm, tn)
    raw_scratch = [pltpu.VMEM(raw_shape, F32) for _ in range((1 + lag) * len(dots))]
    kernel = functools.partial(
        _ws_kernel, n_a=len(a_list), n_w=len(w_list), dots=tuple(dots),
        n_extra=len(extras), n_out=len(out_shapes), epilogue=epilogue,
        n_row_tiles=n_row_tiles, n_tiles=n_tiles, pad=pad, lag=lag, lane_split=lane_split,
        m_split=m_split)
    return pl.pallas_call(
        kernel,
        grid=(n_tiles + lag,),
        in_specs=in_specs,
        out_specs=[pl.BlockSpec(shape, lagged(fn)) for shape, fn in out_specs],
        out_shape=list(out_shapes),
        scratch_shapes=wbf_scratch + raw_scratch + list(scratch),
        compiler_params=_params(1),
        name=name,
    )(*a_list, *[w for w, _ in w_list], *extras)


def _tile_spec(tm, tn, col_offset=0):
    return ((tm, tn), lambda j, i: (i, j + col_offset))


def _qk_epilogue(j, i, acc_refs, extra_refs, out_refs, scratch, rows, *, n_q_tiles, tm, tn):
    assert rows == (0, tm)
    cos_ref, sin_ref, g_ref = extra_refs
    o_ref, mean_ref = out_refs
    acc_ref = acc_refs[0]
    is_q = j < n_q_tiles
    g = jnp.where(is_q, g_ref[0:1, :], g_ref[1:2, :])
    scale = jnp.where(is_q, jnp.float32(HEAD_DIM ** -0.5), jnp.float32(1.0))
    cos = cos_ref[...]
    sin = sin_ref[...]
    rows_per_tile = tm // MOBA_BLOCK
    for h in range(tn // HEAD_DIM):
        cols = slice(h * HEAD_DIM, (h + 1) * HEAD_DIM)
        x = acc_ref[:, cols]
        ms = jnp.mean(x * x, axis=-1, keepdims=True)
        y = x * lax.rsqrt(ms + EPS) * g
        y = y * cos + pltpu.roll(y, HEAD_DIM // 2, 1) * sin
        o_ref[:, cols] = (y * scale).astype(o_ref.dtype)
        for r in range(rows_per_tile):
            blk = y[r * MOBA_BLOCK:(r + 1) * MOBA_BLOCK]
            mean_ref[0, r:r + 1, cols] = jnp.mean(blk, axis=0, keepdims=True)


def _store_bf16_epilogue(j, i, acc_refs, extra_refs, out_refs, scratch, rows):
    r = slice(*rows)
    out_refs[0][r, :] = acc_refs[0][r, :].astype(out_refs[0].dtype)


def _glu_epilogue(j, i, acc_refs, extra_refs, out_refs, scratch, rows):
    r = slice(*rows)
    out_refs[0][r, :] = acc_refs[0][r, :] * jax.nn.sigmoid(acc_refs[1][r, :])


def _sigmoid_epilogue(j, i, acc_refs, extra_refs, out_refs, scratch, rows):
    r = slice(*rows)
    out_refs[0][r, :] = jax.nn.sigmoid(acc_refs[0][r, :])


def _with_side_cast(epilogue):
    def wrapped(j, i, acc_refs, extra_refs, out_refs, scratch, rows):
        if rows[0] == 0:
            out_refs[-1][...] = extra_refs[-1][...].astype(out_refs[-1].dtype)
        epilogue(j, i, acc_refs, extra_refs[:-1], out_refs[:-1], scratch, rows)
    return wrapped


def _merge_epilogue(j, i, acc_refs, extra_refs, out_refs, scratch, rows):
    ga_ref, gc_ref = extra_refs
    r = slice(*rows)
    merged = ga_ref[r, :] * acc_refs[0][r, :] + gc_ref[r, :] * acc_refs[1][r, :]
    out_refs[0][r, :] = merged.astype(out_refs[0].dtype)


def _residual_epilogue(j, i, acc_refs, extra_refs, out_refs, scratch, rows):
    r = slice(*rows)
    out_refs[0][r, :] = extra_refs[0][r, :] + acc_refs[0][r, :]


def _ffn_in_epilogue(j, i, acc_refs, extra_refs, out_refs, scratch, rows, *, tm, tiles_per_seq,
                     width):
    cwg_ref, cwu_ref, cbg_ref, cbu_ref = extra_refs
    pad = SUBLANES
    seq_start = i % tiles_per_seq == 0
    if rows[0] == 0:
        for hp, carry in zip(acc_refs, scratch):
            prev = carry[...]
            hp[:, 0:pad, :] = jnp.where(seq_start, jnp.zeros_like(prev), prev)
    for lt in range(acc_refs[0].shape[0]):
        lanes = slice(lt * LANES, (lt + 1) * LANES)
        for r in range(rows[0], rows[1], FFN_ROW_CHUNK):
            convs = []
            for hp, cw_ref, cb_ref in zip(acc_refs, (cwg_ref, cwu_ref), (cbg_ref, cbu_ref)):
                y = cb_ref[:, lanes]
                for w in range(width):
                    lo = pad + r - (width - 1 - w)
                    y = y + cw_ref[w:w + 1, lanes] * hp[lt, lo:lo + FFN_ROW_CHUNK, :]
                convs.append(y)
            gate, up = convs
            out_refs[0][r:r + FFN_ROW_CHUNK, lanes] = (
                jax.nn.silu(gate) * up).astype(out_refs[0].dtype)
    if rows[1] == tm:
        for hp, carry in zip(acc_refs, scratch):
            carry[...] = hp[:, tm:tm + pad, :]


def _attn_kernel(q_ref, k_ref, v_ref, km_ref, o_ref, *, n_blocks):
    L = MOBA_BLOCK
    nt = (((1,), (1,)), ((), ()))
    tn = (((0,), (0,)), ((), ()))
    km = km_ref[...].astype(BF16)
    neg = jnp.float32(-jnp.inf)
    key = lax.broadcasted_iota(jnp.int32, (L, L), 0)
    qry = lax.broadcasted_iota(jnp.int32, (L, L), 1)
    causal = key <= qry
    blk = lax.broadcasted_iota(jnp.int32, (n_blocks, L), 0)
    for i in range(n_blocks):
        qi = q_ref[i * L:(i + 1) * L, :]
        n_keys = (i + 1) * L
        st = lax.dot_general(k_ref[0:n_keys, :], qi, nt, preferred_element_type=F32)
        pen = None
        if i > MOBA_TOP_K:
            gate = lax.dot_general(km, qi, nt, preferred_element_type=F32)
            rank = jnp.zeros((n_blocks, L), F32)
            for jp in range(i):
                gb = jnp.broadcast_to(gate[jp:jp + 1, :], (n_blocks, L))
                beats = (gb > gate) | ((gb == gate) & (jp < blk))
                rank = rank + jnp.where(beats, 1.0, 0.0)
            pen = jnp.where(rank < MOBA_TOP_K, 0.0, neg)
        chunks = []
        for jb in range(i + 1):
            sj = st[jb * L:(jb + 1) * L, :]
            if jb == i:
                sj = jnp.where(causal, sj, neg)
            elif pen is not None:
                sj = sj + pen[jb:jb + 1, :]
            chunks.append(sj)
        top = chunks[0]
        for c in chunks[1:]:
            top = jnp.maximum(top, c)
        m = top.max(axis=0, keepdims=True)
        es = [jnp.exp(c - m) for c in chunks]
        tot = es[0]
        for e in es[1:]:
            tot = tot + e
        inv_l = 1.0 / tot.sum(axis=0, keepdims=True)
        pt = jnp.concatenate([(e * inv_l).astype(BF16) for e in es], axis=0)
        ot = lax.dot_general(v_ref[0:n_keys, :], pt, tn, preferred_element_type=F32)
        o_ref[i * L:(i + 1) * L, :] = ot.T.astype(o_ref.dtype)


def _moba_attention(qk, v, kmean, batch, seq, n_heads):
    t = qk.shape[0]
    n_blocks = seq // MOBA_BLOCK
    return pl.pallas_call(
        functools.partial(_attn_kernel, n_blocks=n_blocks),
        grid=(batch, n_heads),
        in_specs=[pl.BlockSpec((seq, HEAD_DIM), lambda b, h: (b, h)),
                  pl.BlockSpec((seq, HEAD_DIM), lambda b, h: (b, n_heads + h)),
                  pl.BlockSpec((seq, HEAD_DIM), lambda b, h: (b, h)),
                  pl.BlockSpec((n_blocks, HEAD_DIM), lambda b, h: (b, n_heads + h))],
        out_specs=pl.BlockSpec((seq, HEAD_DIM), lambda b, h: (b, h)),
        out_shape=jax.ShapeDtypeStruct((t, n_heads * HEAD_DIM), BF16),
        compiler_params=_params(2),
        name="moba_attention",
    )(qk, qk, v, kmean)


def _conv_kernel(h_ref, halo_ref, cw_ref, cb_ref, g_ref, b_ref, o_ref, xpad_ref, y_ref, z_ref,
                 *, ts, halo, width):
    t = pl.program_id(1)
    prev = halo_ref[...]
    xpad_ref[0:halo, :] = jnp.where(t == 0, jnp.zeros_like(prev), prev)
    xpad_ref[halo:, :] = h_ref[...]
    n_chunks = h_ref.shape[1] // LANES
    base = halo - (width - 1)

    def chunk(c, carry):
        cols = pl.ds(pl.multiple_of(c * LANES, LANES), LANES)
        groups = []
        for r in range(SUBLANES):
            taps = [w for w in range(width) if (base + w) % SUBLANES == r]
            if taps:
                span = taps[-1] - taps[0] + ts
                z_ref[r, 0:span, :] = xpad_ref[base + taps[0]:base + taps[0] + span, cols]
                groups.append((r, taps))
        for r0 in range(0, ts, CONV_ROW_GROUP):
            acc = jnp.broadcast_to(cb_ref[:, cols], (CONV_ROW_GROUP, LANES))
            for r, taps in groups:
                for w in taps:
                    off = r0 + w - taps[0]
                    acc = acc + cw_ref[w:w + 1, cols] * z_ref[r, off:off + CONV_ROW_GROUP, :]
            y_ref[r0:r0 + CONV_ROW_GROUP, cols] = acc
        return carry

    lax.fori_loop(0, n_chunks, chunk, 0)
    y = y_ref[...]
    mu = jnp.mean(y, axis=-1, keepdims=True)
    yc = y - mu
    var = jnp.mean(yc * yc, axis=-1, keepdims=True)
    z = yc * lax.rsqrt(var + EPS) * g_ref[...] + b_ref[...]
    o_ref[...] = jax.nn.silu(z).astype(o_ref.dtype)


def _conv_module(h, conv_w, conv_b, ln_g, ln_b, batch, seq, ts=256, halo=32):
    t, c = h.shape
    width = conv_w.shape[0]
    tiles = seq // ts
    halo_per_tile = ts // halo

    def halo_map(b, s):
        return (jnp.maximum((b * tiles + s) * halo_per_tile - 1, 0), 0)

    vec = lambda: pl.BlockSpec((1, c), lambda b, s: (0, 0))
    return pl.pallas_call(
        functools.partial(_conv_kernel, ts=ts, halo=halo, width=width),
        grid=(batch, tiles),
        in_specs=[pl.BlockSpec((ts, c), lambda b, s: (b * tiles + s, 0)),
                  pl.BlockSpec((halo, c), halo_map),
                  pl.BlockSpec((width, c), lambda b, s: (0, 0)),
                  vec(), vec(), vec()],
        out_specs=pl.BlockSpec((ts, c), lambda b, s: (b * tiles + s, 0)),
        out_shape=jax.ShapeDtypeStruct((t, c), BF16),
        scratch_shapes=[pltpu.VMEM((halo + ts, c), F32), pltpu.VMEM((ts, c), F32),
                        pltpu.VMEM((SUBLANES, ts + halo, LANES), F32)],
        compiler_params=_params(2),
        name="conv_module",
    )(h, h, conv_w, conv_b.reshape(1, c), ln_g.reshape(1, c), ln_b.reshape(1, c))


def _layer(x, positions, norm1_g, w_in, q_norm_g, k_norm_g, w_o_attn, conv_w, conv_b,
           conv_ln_g, conv_ln_b, w_o_conv, w_out, norm2_g, w_ffn_in, ffn_conv_w,
           ffn_conv_b, w_ffn_out):
    batch, seq, d_model = x.shape
    t = batch * seq
    attn_width = w_o_attn.shape[0]
    conv_dim = w_o_conv.shape[0]
    d_ff = w_ffn_out.shape[0]
    n_heads = attn_width // HEAD_DIM
    assert seq % MOBA_BLOCK == 0
    assert w_in.shape[1] == 3 * attn_width + 2 * conv_dim + 2 * d_model

    x2 = x.reshape(t, d_model)
    xn = _rmsnorm(x2, norm1_g)
    cos, sin = _rope_tables(positions)

    tm, tn = 1024, 512
    qk_g = jnp.stack([q_norm_g, k_norm_g])
    n_q_tiles = attn_width // tn
    qk, kmean = _ws_matmul(
        "qk_proj", [xn], [(w_in, 0)], [(0, 0)], 2 * n_q_tiles, tm, tn,
        functools.partial(_qk_epilogue, n_q_tiles=n_q_tiles, tm=tm, tn=tn), lag=1,
        extras=(cos, sin, qk_g),
        extra_specs=(((tm, HEAD_DIM), lambda j, i: (i, 0)),
                     ((tm, HEAD_DIM), lambda j, i: (i, 0)),
                     ((2, HEAD_DIM), lambda j, i: (0, 0))),
        out_shapes=(jax.ShapeDtypeStruct((t, 2 * attn_width), BF16),
                    jax.ShapeDtypeStruct((t // tm, tm // MOBA_BLOCK, 2 * attn_width), F32)),
        out_specs=(_tile_spec(tm, tn),
                   ((1, tm // MOBA_BLOCK, tn), lambda j, i: (i, 0, j))))
    kmean = kmean.reshape(t // MOBA_BLOCK, 2 * attn_width)
    tmv, tnv = 512, 1024
    (v,) = _ws_matmul(
        "v_proj", [xn], [(w_in, 2 * attn_width // tnv)], [(0, 0)], attn_width // tnv, tmv, tnv,
        _store_bf16_epilogue,
        out_shapes=(jax.ShapeDtypeStruct((t, attn_width), BF16),),
        out_specs=(_tile_spec(tmv, tnv),))
    tg = 256
    u_off = 3 * attn_width
    (h,) = _ws_matmul(
        "glu_proj", [xn], [(w_in, u_off // tg), (w_in, (u_off + conv_dim) // tg)],
        [(0, 0), (0, 1)], conv_dim // tg, tm, tg, _glu_epilogue,
        out_shapes=(jax.ShapeDtypeStruct((t, conv_dim), F32),),
        out_specs=(_tile_spec(tm, tg),))
    g_off = u_off + 2 * conv_dim
    (gates,) = _ws_matmul(
        "gate_proj", [xn], [(w_in, g_off // tn)], [(0, 0)], 2 * d_model // tn, tm, tn,
        _sigmoid_epilogue, lag=1, m_split=tm // MXU_ROW_BLOCK,
        out_shapes=(jax.ShapeDtypeStruct((t, 2 * d_model), F32),),
        out_specs=(_tile_spec(tm, tn),))

    attn = _moba_attention(qk, v, kmean, batch, seq, n_heads)
    hc = _conv_module(h, conv_w, conv_b, conv_ln_g, conv_ln_b, batch, seq)

    n_model_tiles = d_model // tn
    tmo = 512
    tmm = 1024
    (merged,) = _ws_matmul(
        "merge_proj", [attn, hc], [(w_o_attn, 0), (w_o_conv, 0)], [(0, 0), (1, 1)],
        n_model_tiles, tmm, tn, _merge_epilogue,
        extras=(gates, gates),
        extra_specs=(_tile_spec(tmm, tn), _tile_spec(tmm, tn, n_model_tiles)),
        out_shapes=(jax.ShapeDtypeStruct((t, d_model), BF16),),
        out_specs=(_tile_spec(tmm, tn),))
    (x1,) = _ws_matmul(
        "out_proj", [merged], [(w_out, 0)], [(0, 0)], n_model_tiles, tmo, tn,
        _residual_epilogue,
        extras=(x2,), extra_specs=(_tile_spec(tmo, tn),),
        out_shapes=(jax.ShapeDtypeStruct((t, d_model), F32),),
        out_specs=(_tile_spec(tmo, tn),))

    xn2 = _rmsnorm(x1, norm2_g)
    tf = 256
    assert d_ff % tf == 0 and seq % tm == 0
    n_ff_tiles = d_ff // tf
    fcw = ffn_conv_w
    fcb = ffn_conv_b.reshape(1, 2 * d_ff)
    width = fcw.shape[0]
    n_ff_steps = n_ff_tiles * (t // tm)
    assert d_ff % (n_ff_steps * SUBLANES) == 0
    slab = ((d_ff // n_ff_steps, d_model), lambda j, i: (j * (t // tm) + i, 0))
    act, w_ffn_out_bf = _ws_matmul(
        "ffn_in", [xn2], [(w_ffn_in, 0), (w_ffn_in, n_ff_tiles)], [(0, 0), (0, 1)],
        n_ff_tiles, tm, tf,
        _with_side_cast(functools.partial(
            _ffn_in_epilogue, tm=tm, tiles_per_seq=seq // tm, width=width)),
        extras=(fcw, fcw, fcb, fcb, w_ffn_out),
        extra_specs=(((width, tf), lambda j, i: (0, j)),
                     ((width, tf), lambda j, i: (0, j + n_ff_tiles)),
                     ((1, tf), lambda j, i: (0, j)),
                     ((1, tf), lambda j, i: (0, j + n_ff_tiles)),
                     slab),
        out_shapes=(jax.ShapeDtypeStruct((t, d_ff), BF16),
                    jax.ShapeDtypeStruct((d_ff, d_model), BF16)),
        out_specs=(_tile_spec(tm, tf), slab),
        scratch=[pltpu.VMEM((tf // LANES, SUBLANES, LANES), F32)] * 2,
        pad=SUBLANES, lane_split=True, lag=1, m_split=tm // MXU_ROW_BLOCK)

    (out,) = _ws_matmul(
        "ffn_out", [act], [(w_ffn_out_bf, 0)], [(0, 0)], n_model_tiles, tmo, tn,
        _residual_epilogue,
        extras=(x1,), extra_specs=(_tile_spec(tmo, tn),),
        out_shapes=(jax.ShapeDtypeStruct((t, d_model), F32),),
        out_specs=(_tile_spec(tmo, tn),))
    return out.reshape(batch, seq, d_model)


def kernel(x, positions, norm1_g, w_in, q_norm_g, k_norm_g, w_o_attn, conv_w, conv_b,
           conv_ln_g, conv_ln_b, w_o_conv, w_out, norm2_g, w_ffn_in, ffn_conv_w,
           ffn_conv_b, w_ffn_out):
    depth = norm1_g.shape[0]
    params = (norm1_g, w_in, q_norm_g, k_norm_g, w_o_attn, conv_w, conv_b, conv_ln_g,
              conv_ln_b, w_o_conv, w_out, norm2_g, w_ffn_in, ffn_conv_w, ffn_conv_b, w_ffn_out)
    for l in range(depth):
        layer = [p.reshape(p.shape[1:]) if depth == 1 else p[l] for p in params]
        x = _layer(x, positions, *layer)
    return x
```

```python
import functools

import jax
import jax.numpy as jnp
from jax import lax
from jax.experimental import pallas as pl
from jax.experimental.pallas import tpu as pltpu

HEAD_DIM = 128
MOBA_BLOCK = 256
MOBA_TOP_K = 3
ROPE_THETA = 10000.0
EPS = 1e-6

LANES = 128
SUBLANES = 8
VMEM_LIMIT_BYTES = 56 * 1024 * 1024
FFN_ROW_CHUNK = 64
CONV_ROW_GROUP = 128
MXU_ROW_BLOCK = 128
ATTN_SCORES_AHEAD = 3
ATTN_OUTPUT_BEHIND = 2

F32 = jnp.float32
BF16 = jnp.bfloat16


def _params(n_grid_dims):
    return pltpu.CompilerParams(
        dimension_semantics=("arbitrary",) * n_grid_dims,
        vmem_limit_bytes=VMEM_LIMIT_BYTES)


def _rmsnorm_kernel(x_ref, g_ref, o_ref):
    x = x_ref[...]
    ms = jnp.mean(x * x, axis=-1, keepdims=True)
    o_ref[...] = (x * lax.rsqrt(ms + EPS) * g_ref[...]).astype(o_ref.dtype)


def _rmsnorm(x, g, tr=512):
    t, d = x.shape
    return pl.pallas_call(
        _rmsnorm_kernel,
        grid=(t // tr,),
        in_specs=[pl.BlockSpec((tr, d), lambda i: (i, 0)),
                  pl.BlockSpec((1, d), lambda i: (0, 0))],
        out_specs=pl.BlockSpec((tr, d), lambda i: (i, 0)),
        out_shape=jax.ShapeDtypeStruct((t, d), BF16),
        compiler_params=_params(1),
        name="rmsnorm",
    )(x, g.reshape(1, d))


def _rope_table_kernel(pos_ref, invf_ref, sign_ref, cos_ref, sin_ref):
    ang = pos_ref[...] * invf_ref[...]
    cos_ref[...] = jnp.cos(ang)
    sin_ref[...] = jnp.sin(ang) * sign_ref[...]


def _rope_tables(positions, tr=1024):
    t = positions.size
    half = HEAD_DIM // 2
    inv = ROPE_THETA ** (-jnp.arange(half, dtype=F32) / half)
    invf = jnp.concatenate([inv, inv]).reshape(1, HEAD_DIM)
    sign = jnp.concatenate([-jnp.ones((half,), F32), jnp.ones((half,), F32)]).reshape(1, HEAD_DIM)
    pos = positions.astype(F32).reshape(t, 1)
    return pl.pallas_call(
        _rope_table_kernel,
        grid=(t // tr,),
        in_specs=[pl.BlockSpec((tr, 1), lambda i: (i, 0)),
                  pl.BlockSpec((1, HEAD_DIM), lambda i: (0, 0)),
                  pl.BlockSpec((1, HEAD_DIM), lambda i: (0, 0))],
        out_specs=[pl.BlockSpec((tr, HEAD_DIM), lambda i: (i, 0))] * 2,
        out_shape=[jax.ShapeDtypeStruct((t, HEAD_DIM), F32)] * 2,
        compiler_params=_params(1),
        name="rope_tables",
    )(pos, invf, sign)


def _ws_kernel(*refs, n_a, n_w, dots, n_extra, n_out, epilogue, n_row_tiles, n_tiles, pad, lag,
               lane_split, m_split):
    a_refs = refs[:n_a]
    w_refs = refs[n_a:n_a + n_w]
    extra_refs = refs[n_a + n_w:n_a + n_w + n_extra]
    out_refs = refs[n_a + n_w + n_extra:n_a + n_w + n_extra + n_out]
    scratch = refs[n_a + n_w + n_extra + n_out:]
    to_cast = [w_ref for w_ref in w_refs if w_ref.dtype != BF16]
    cast_refs = scratch[:len(to_cast)]
    n_slots = 1 + lag
    n_raw = n_slots * len(dots)
    raw_refs = scratch[len(to_cast):len(to_cast) + n_raw]
    user_scratch = scratch[len(to_cast) + n_raw:]
    s = pl.program_id(0)
    tile = jnp.minimum(s, n_tiles - 1)

    if to_cast:
        @pl.when((tile % n_row_tiles == 0) & (s < n_tiles))
        def _cast_weights():
            for w_ref, wbf_ref in zip(to_cast, cast_refs):
                wbf_ref[...] = w_ref[...].astype(BF16)

    if lag or user_scratch:
        @pl.when(s == 0)
        def _init():
            if lag:
                for raw in raw_refs[1::2]:
                    raw[...] = jnp.zeros(raw.shape, F32)
            for u in user_scratch:
                u[...] = jnp.zeros(u.shape, u.dtype)

    cast_iter = iter(cast_refs)
    wbf_refs = [w_ref if w_ref.dtype == BF16 else next(cast_iter) for w_ref in w_refs]
    done = jnp.maximum(s - lag, 0)

    def step(slot):
        tm = a_refs[0].shape[0]
        rb = tm // m_split
        for k in range(m_split):
            rows = (k * rb, (k + 1) * rb)
            for d, (ai, wi) in enumerate(dots):
                res = jnp.dot(a_refs[ai][rows[0]:rows[1], :], wbf_refs[wi][...],
                              preferred_element_type=F32)
                raw = raw_refs[n_slots * d + slot]
                if lane_split:
                    for lt in range(raw.shape[0]):
                        raw[lt, pad + rows[0]:pad + rows[1], :] = res[:, lt * LANES:(lt + 1) * LANES]
                else:
                    raw[pad + rows[0]:pad + rows[1], :] = res
            epilogue(done // n_row_tiles, done % n_row_tiles,
                     [raw_refs[n_slots * d + (slot + lag) % n_slots] for d in range(len(dots))],
                     extra_refs, out_refs, user_scratch, rows)

    if lag:
        for slot in range(2):
            pl.when(s % 2 == slot)(functools.partial(step, slot))
    else:
        step(0)


def _ws_matmul(name, a_list, w_list, dots, n_col_tiles, tm, tn, epilogue,
               extras=(), extra_specs=(), out_shapes=(), out_specs=(), scratch=(), pad=0, lag=0,
               lane_split=False, m_split=1):
    t = a_list[0].shape[0]
    n_row_tiles = t // tm
    n_tiles = n_col_tiles * n_row_tiles

    def dot_tile(s):
        return jnp.minimum(s, n_tiles - 1)

    def lagged(index_map):
        def wrapped(s):
            done = jnp.maximum(s - lag, 0)
            return index_map(done // n_row_tiles, done % n_row_tiles)
        return wrapped

    in_specs = [pl.BlockSpec((tm, a.shape[1]), lambda s: (dot_tile(s) % n_row_tiles, 0))
                for a in a_list]
    for w, off in w_list:
        in_specs.append(pl.BlockSpec((w.shape[0], tn), functools.partial(
            lambda s, off: (0, dot_tile(s) // n_row_tiles + off), off=off)))
    in_specs.extend(pl.BlockSpec(shape, lagged(fn)) for shape, fn in extra_specs)
    wbf_scratch = [pltpu.VMEM((w.shape[0], tn), BF16) for w, _ in w_list if w.dtype != BF16]
    raw_shape = (tn // LANES, pad + tm, LANES) if lane_split else (pad + tm, tn)
    raw_scratch = [pltpu.VMEM(raw_shape, F32) for _ in range((1 + lag) * len(dots))]
    kernel = functools.partial(
        _ws_kernel, n_a=len(a_list), n_w=len(w_list), dots=tuple(dots),
        n_extra=len(extras), n_out=len(out_shapes), epilogue=epilogue,
        n_row_tiles=n_row_tiles, n_tiles=n_tiles, pad=pad, lag=lag, lane_split=lane_split,
        m_split=m_split)
    return pl.pallas_call(
        kernel,
        grid=(n_tiles + lag,),
        in_specs=in_specs,
        out_specs=[pl.BlockSpec(shape, lagged(fn)) for shape, fn in out_specs],
        out_shape=list(out_shapes),
        scratch_shapes=wbf_scratch + raw_scratch + list(scratch),
        compiler_params=_params(1),
        name=name,
    )(*a_list, *[w for w, _ in w_list], *extras)


def _tile_spec(tm, tn, col_offset=0):
    return ((tm, tn), lambda j, i: (i, j + col_offset))


def _qk_epilogue(j, i, acc_refs, extra_refs, out_refs, scratch, rows, *, n_q_tiles, tm, tn):
    assert rows == (0, tm)
    cos_ref, sin_ref, g_ref = extra_refs
    o_ref, mean_ref = out_refs
    acc_ref = acc_refs[0]
    is_q = j < n_q_tiles
    g = jnp.where(is_q, g_ref[0:1, :], g_ref[1:2, :])
    scale = jnp.where(is_q, jnp.float32(HEAD_DIM ** -0.5), jnp.float32(1.0))
    cos = cos_ref[...]
    sin = sin_ref[...]
    rows_per_tile = tm // MOBA_BLOCK
    for h in range(tn // HEAD_DIM):
        cols = slice(h * HEAD_DIM, (h + 1) * HEAD_DIM)
        x = acc_ref[:, cols]
        ms = jnp.mean(x * x, axis=-1, keepdims=True)
        y = x * lax.rsqrt(ms + EPS) * g
        y = y * cos + pltpu.roll(y, HEAD_DIM // 2, 1) * sin
        o_ref[:, cols] = (y * scale).astype(o_ref.dtype)
        for r in range(rows_per_tile):
            blk = y[r * MOBA_BLOCK:(r + 1) * MOBA_BLOCK]
            mean_ref[0, r:r + 1, cols] = jnp.mean(blk, axis=0, keepdims=True)


def _store_bf16_epilogue(j, i, acc_refs, extra_refs, out_refs, scratch, rows):
    r = slice(*rows)
    out_refs[0][r, :] = acc_refs[0][r, :].astype(out_refs[0].dtype)


def _glu_epilogue(j, i, acc_refs, extra_refs, out_refs, scratch, rows):
    r = slice(*rows)
    out_refs[0][r, :] = acc_refs[0][r, :] * jax.nn.sigmoid(acc_refs[1][r, :])


def _sigmoid_epilogue(j, i, acc_refs, extra_refs, out_refs, scratch, rows):
    r = slice(*rows)
    out_refs[0][r, :] = jax.nn.sigmoid(acc_refs[0][r, :])


def _with_side_cast(epilogue):
    def wrapped(j, i, acc_refs, extra_refs, out_refs, scratch, rows):
        if rows[0] == 0:
            out_refs[-1][...] = extra_refs[-1][...].astype(out_refs[-1].dtype)
        epilogue(j, i, acc_refs, extra_refs[:-1], out_refs[:-1], scratch, rows)
    return wrapped


def _merge_epilogue(j, i, acc_refs, extra_refs, out_refs, scratch, rows):
    ga_ref, gc_ref = extra_refs
    r = slice(*rows)
    merged = ga_ref[r, :] * acc_refs[0][r, :] + gc_ref[r, :] * acc_refs[1][r, :]
    out_refs[0][r, :] = merged.astype(out_refs[0].dtype)


def _residual_epilogue(j, i, acc_refs, extra_refs, out_refs, scratch, rows):
    r = slice(*rows)
    out_refs[0][r, :] = extra_refs[0][r, :] + acc_refs[0][r, :]


def _ffn_in_epilogue(j, i, acc_refs, extra_refs, out_refs, scratch, rows, *, tm, tiles_per_seq,
                     width):
    cwg_ref, cwu_ref, cbg_ref, cbu_ref = extra_refs
    pad = SUBLANES
    seq_start = i % tiles_per_seq == 0
    if rows[0] == 0:
        for hp, carry in zip(acc_refs, scratch):
            prev = carry[...]
            hp[:, 0:pad, :] = jnp.where(seq_start, jnp.zeros_like(prev), prev)
    for lt in range(acc_refs[0].shape[0]):
        lanes = slice(lt * LANES, (lt + 1) * LANES)
        for r in range(rows[0], rows[1], FFN_ROW_CHUNK):
            convs = []
            for hp, cw_ref, cb_ref in zip(acc_refs, (cwg_ref, cwu_ref), (cbg_ref, cbu_ref)):
                y = cb_ref[:, lanes]
                for w in range(width):
                    lo = pad + r - (width - 1 - w)
                    y = y + cw_ref[w:w + 1, lanes] * hp[lt, lo:lo + FFN_ROW_CHUNK, :]
                convs.append(y)
            gate, up = convs
            out_refs[0][r:r + FFN_ROW_CHUNK, lanes] = (
                jax.nn.silu(gate) * up).astype(out_refs[0].dtype)
    if rows[1] == tm:
        for hp, carry in zip(acc_refs, scratch):
            carry[...] = hp[:, tm:tm + pad, :]


def _attn_kernel(q_ref, k_ref, v_ref, km_ref, o_ref, *, n_blocks):
    L = MOBA_BLOCK
    nt = (((1,), (1,)), ((), ()))
    tn = (((0,), (0,)), ((), ()))
    km = km_ref[...].astype(BF16)
    neg = jnp.float32(-jnp.inf)
    key = lax.broadcasted_iota(jnp.int32, (L, L), 0)
    qry = lax.broadcasted_iota(jnp.int32, (L, L), 1)
    causal = key <= qry
    blk = lax.broadcasted_iota(jnp.int32, (n_blocks, L), 0)

    def scores(i):
        qi = q_ref[i * L:(i + 1) * L, :]
        n_keys = (i + 1) * L
        st = lax.dot_general(k_ref[0:n_keys, :], qi, nt, preferred_element_type=F32)
        gate = None
        if i > MOBA_TOP_K:
            gate = lax.dot_general(km, qi, nt, preferred_element_type=F32)
        return st, gate

    def probabilities(i, st, gate):
        pen = None
        if gate is not None:
            rank = jnp.zeros((n_blocks, L), F32)
            for jp in range(i):
                gb = jnp.broadcast_to(gate[jp:jp + 1, :], (n_blocks, L))
                beats = (gb > gate) | ((gb == gate) & (jp < blk))
                rank = rank + jnp.where(beats, 1.0, 0.0)
            pen = jnp.where(rank < MOBA_TOP_K, 0.0, neg)
        chunks = []
        for jb in range(i + 1):
            sj = st[jb * L:(jb + 1) * L, :]
            if jb == i:
                sj = jnp.where(causal, sj, neg)
            elif pen is not None:
                sj = sj + pen[jb:jb + 1, :]
            chunks.append(sj)
        top = chunks[0]
        for c in chunks[1:]:
            top = jnp.maximum(top, c)
        m = top.max(axis=0, keepdims=True)
        es = [jnp.exp(c - m) for c in chunks]
        tot = es[0]
        for e in es[1:]:
            tot = tot + e
        inv_l = 1.0 / tot.sum(axis=0, keepdims=True)
        return jnp.concatenate([(e * inv_l).astype(BF16) for e in es], axis=0)

    def output(i, pt):
        n_keys = (i + 1) * L
        ot = lax.dot_general(v_ref[0:n_keys, :], pt, tn, preferred_element_type=F32)
        o_ref[i * L:(i + 1) * L, :] = ot.T.astype(o_ref.dtype)

    st = {}
    pt = {}
    for i in range(min(ATTN_SCORES_AHEAD, n_blocks)):
        st[i] = scores(i)
    for i in range(n_blocks):
        pt[i] = probabilities(i, *st.pop(i))
        if i + ATTN_SCORES_AHEAD < n_blocks:
            st[i + ATTN_SCORES_AHEAD] = scores(i + ATTN_SCORES_AHEAD)
        if i >= ATTN_OUTPUT_BEHIND:
            output(i - ATTN_OUTPUT_BEHIND, pt.pop(i - ATTN_OUTPUT_BEHIND))
    for i in sorted(pt):
        output(i, pt[i])


def _moba_attention(qk, v, kmean, batch, seq, n_heads):
    t = qk.shape[0]
    n_blocks = seq // MOBA_BLOCK
    return pl.pallas_call(
        functools.partial(_attn_kernel, n_blocks=n_blocks),
        grid=(batch, n_heads),
        in_specs=[pl.BlockSpec((seq, HEAD_DIM), lambda b, h: (b, h)),
                  pl.BlockSpec((seq, HEAD_DIM), lambda b, h: (b, n_heads + h)),
                  pl.BlockSpec((seq, HEAD_DIM), lambda b, h: (b, h)),
                  pl.BlockSpec((n_blocks, HEAD_DIM), lambda b, h: (b, n_heads + h))],
        out_specs=pl.BlockSpec((seq, HEAD_DIM), lambda b, h: (b, h)),
        out_shape=jax.ShapeDtypeStruct((t, n_heads * HEAD_DIM), BF16),
        compiler_params=_params(2),
        name="moba_attention",
    )(qk, qk, v, kmean)


def _conv_kernel(h_ref, halo_ref, cw_ref, cb_ref, g_ref, b_ref, o_ref, xpad_ref, y_ref, z_ref,
                 *, ts, halo, width):
    t = pl.program_id(1)
    prev = halo_ref[...]
    xpad_ref[0:halo, :] = jnp.where(t == 0, jnp.zeros_like(prev), prev)
    xpad_ref[halo:, :] = h_ref[...]
    n_chunks = h_ref.shape[1] // LANES
    base = halo - (width - 1)

    def chunk(c, carry):
        cols = pl.ds(pl.multiple_of(c * LANES, LANES), LANES)
        groups = []
        for r in range(SUBLANES):
            taps = [w for w in range(width) if (base + w) % SUBLANES == r]
            if taps:
                span = taps[-1] - taps[0] + ts
                z_ref[r, 0:span, :] = xpad_ref[base + taps[0]:base + taps[0] + span, cols]
                groups.append((r, taps))
        for r0 in range(0, ts, CONV_ROW_GROUP):
            acc = jnp.broadcast_to(cb_ref[:, cols], (CONV_ROW_GROUP, LANES))
            for r, taps in groups:
                for w in taps:
                    off = r0 + w - taps[0]
                    acc = acc + cw_ref[w:w + 1, cols] * z_ref[r, off:off + CONV_ROW_GROUP, :]
            y_ref[r0:r0 + CONV_ROW_GROUP, cols] = acc
        return carry

    lax.fori_loop(0, n_chunks, chunk, 0)
    y = y_ref[...]
    mu = jnp.mean(y, axis=-1, keepdims=True)
    yc = y - mu
    var = jnp.mean(yc * yc, axis=-1, keepdims=True)
    z = yc * lax.rsqrt(var + EPS) * g_ref[...] + b_ref[...]
    o_ref[...] = jax.nn.silu(z).astype(o_ref.dtype)


def _conv_module(h, conv_w, conv_b, ln_g, ln_b, batch, seq, ts=256, halo=32):
    t, c = h.shape
    width = conv_w.shape[0]
    tiles = seq // ts
    halo_per_tile = ts // halo

    def halo_map(b, s):
        return (jnp.maximum((b * tiles + s) * halo_per_tile - 1, 0), 0)

    vec = lambda: pl.BlockSpec((1, c), lambda b, s: (0, 0))
    return pl.pallas_call(
        functools.partial(_conv_kernel, ts=ts, halo=halo, width=width),
        grid=(batch, tiles),
        in_specs=[pl.BlockSpec((ts, c), lambda b, s: (b * tiles + s, 0)),
                  pl.BlockSpec((halo, c), halo_map),
                  pl.BlockSpec((width, c), lambda b, s: (0, 0)),
                  vec(), vec(), vec()],
        out_specs=pl.BlockSpec((ts, c), lambda b, s: (b * tiles + s, 0)),
        out_shape=jax.ShapeDtypeStruct((t, c), BF16),
        scratch_shapes=[pltpu.VMEM((halo + ts, c), F32), pltpu.VMEM((ts, c), F32),
                        pltpu.VMEM((SUBLANES, ts + halo, LANES), F32)],
        compiler_params=_params(2),
        name="conv_module",
    )(h, h, conv_w, conv_b.reshape(1, c), ln_g.reshape(1, c), ln_b.reshape(1, c))


def _layer(x, positions, norm1_g, w_in, q_norm_g, k_norm_g, w_o_attn, conv_w, conv_b,
           conv_ln_g, conv_ln_b, w_o_conv, w_out, norm2_g, w_ffn_in, ffn_conv_w,
           ffn_conv_b, w_ffn_out):
    batch, seq, d_model = x.shape
    t = batch * seq
    attn_width = w_o_attn.shape[0]
    conv_dim = w_o_conv.shape[0]
    d_ff = w_ffn_out.shape[0]
    n_heads = attn_width // HEAD_DIM
    assert seq % MOBA_BLOCK == 0
    assert w_in.shape[1] == 3 * attn_width + 2 * conv_dim + 2 * d_model

    x2 = x.reshape(t, d_model)
    xn = _rmsnorm(x2, norm1_g)
    cos, sin = _rope_tables(positions)

    tm, tn = 1024, 512
    qk_g = jnp.stack([q_norm_g, k_norm_g])
    n_q_tiles = attn_width // tn
    qk, kmean = _ws_matmul(
        "qk_proj", [xn], [(w_in, 0)], [(0, 0)], 2 * n_q_tiles, tm, tn,
        functools.partial(_qk_epilogue, n_q_tiles=n_q_tiles, tm=tm, tn=tn), lag=1,
        extras=(cos, sin, qk_g),
        extra_specs=(((tm, HEAD_DIM), lambda j, i: (i, 0)),
                     ((tm, HEAD_DIM), lambda j, i: (i, 0)),
                     ((2, HEAD_DIM), lambda j, i: (0, 0))),
        out_shapes=(jax.ShapeDtypeStruct((t, 2 * attn_width), BF16),
                    jax.ShapeDtypeStruct((t // tm, tm // MOBA_BLOCK, 2 * attn_width), F32)),
        out_specs=(_tile_spec(tm, tn),
                   ((1, tm // MOBA_BLOCK, tn), lambda j, i: (i, 0, j))))
    kmean = kmean.reshape(t // MOBA_BLOCK, 2 * attn_width)
    tmv, tnv = 512, 1024
    (v,) = _ws_matmul(
        "v_proj", [xn], [(w_in, 2 * attn_width // tnv)], [(0, 0)], attn_width // tnv, tmv, tnv,
        _store_bf16_epilogue,
        out_shapes=(jax.ShapeDtypeStruct((t, attn_width), BF16),),
        out_specs=(_tile_spec(tmv, tnv),))
    tg = 256
    u_off = 3 * attn_width
    (h,) = _ws_matmul(
        "glu_proj", [xn], [(w_in, u_off // tg), (w_in, (u_off + conv_dim) // tg)],
        [(0, 0), (0, 1)], conv_dim // tg, tm, tg, _glu_epilogue,
        out_shapes=(jax.ShapeDtypeStruct((t, conv_dim), F32),),
        out_specs=(_tile_spec(tm, tg),))
    g_off = u_off + 2 * conv_dim
    (gates,) = _ws_matmul(
        "gate_proj", [xn], [(w_in, g_off // tn)], [(0, 0)], 2 * d_model // tn, tm, tn,
        _sigmoid_epilogue, lag=1, m_split=tm // MXU_ROW_BLOCK,
        out_shapes=(jax.ShapeDtypeStruct((t, 2 * d_model), F32),),
        out_specs=(_tile_spec(tm, tn),))

    attn = _moba_attention(qk, v, kmean, batch, seq, n_heads)
    hc = _conv_module(h, conv_w, conv_b, conv_ln_g, conv_ln_b, batch, seq)

    n_model_tiles = d_model // tn
    tmo = 512
    tmm = 1024
    (merged,) = _ws_matmul(
        "merge_proj", [attn, hc], [(w_o_attn, 0), (w_o_conv, 0)], [(0, 0), (1, 1)],
        n_model_tiles, tmm, tn, _merge_epilogue,
        extras=(gates, gates),
        extra_specs=(_tile_spec(tmm, tn), _tile_spec(tmm, tn, n_model_tiles)),
        out_shapes=(jax.ShapeDtypeStruct((t, d_model), BF16),),
        out_specs=(_tile_spec(tmm, tn),))
    (x1,) = _ws_matmul(
        "out_proj", [merged], [(w_out, 0)], [(0, 0)], n_model_tiles, tm, tn,
        _residual_epilogue,
        extras=(x2,), extra_specs=(_tile_spec(tm, tn),),
        out_shapes=(jax.ShapeDtypeStruct((t, d_model), F32),),
        out_specs=(_tile_spec(tm, tn),))

    xn2 = _rmsnorm(x1, norm2_g)
    tf = 256
    assert d_ff % tf == 0 and seq % tm == 0
    n_ff_tiles = d_ff // tf
    fcw = ffn_conv_w
    fcb = ffn_conv_b.reshape(1, 2 * d_ff)
    width = fcw.shape[0]
    n_ff_steps = n_ff_tiles * (t // tm)
    assert d_ff % (n_ff_steps * SUBLANES) == 0
    slab = ((d_ff // n_ff_steps, d_model), lambda j, i: (j * (t // tm) + i, 0))
    act, w_ffn_out_bf = _ws_matmul(
        "ffn_in", [xn2], [(w_ffn_in, 0), (w_ffn_in, n_ff_tiles)], [(0, 0), (0, 1)],
        n_ff_tiles, tm, tf,
        _with_side_cast(functools.partial(
            _ffn_in_epilogue, tm=tm, tiles_per_seq=seq // tm, width=width)),
        extras=(fcw, fcw, fcb, fcb, w_ffn_out),
        extra_specs=(((width, tf), lambda j, i: (0, j)),
                     ((width, tf), lambda j, i: (0, j + n_ff_tiles)),
                     ((1, tf), lambda j, i: (0, j)),
                     ((1, tf), lambda j, i: (0, j + n_ff_tiles)),
                     slab),
        out_shapes=(jax.ShapeDtypeStruct((t, d_ff), BF16),
                    jax.ShapeDtypeStruct((d_ff, d_model), BF16)),
        out_specs=(_tile_spec(tm, tf), slab),
        scratch=[pltpu.VMEM((tf // LANES, SUBLANES, LANES), F32)] * 2,
        pad=SUBLANES, lane_split=True, lag=1, m_split=tm // MXU_ROW_BLOCK)

    (out,) = _ws_matmul(
        "ffn_out", [act], [(w_ffn_out_bf, 0)], [(0, 0)], n_model_tiles, tmo, tn,
        _residual_epilogue,
        extras=(x1,), extra_specs=(_tile_spec(tmo, tn),),
        out_shapes=(jax.ShapeDtypeStruct((t, d_model), F32),),
        out_specs=(_tile_spec(tmo, tn),))
    return out.reshape(batch, seq, d_model)


def kernel(x, positions, norm1_g, w_in, q_norm_g, k_norm_g, w_o_attn, conv_w, conv_b,
           conv_ln_g, conv_ln_b, w_o_conv, w_out, norm2_g, w_ffn_in, ffn_conv_w,
           ffn_conv_b, w_ffn_out):
    depth = norm1_g.shape[0]
    params = (norm1_g, w_in, q_norm_g, k_norm_g, w_o_attn, conv_w, conv_b, conv_ln_g,
              conv_ln_b, w_o_conv, w_out, norm2_g, w_ffn_in, ffn_conv_w, ffn_conv_b, w_ffn_out)
    for l in range(depth):
        layer = [p.reshape(p.shape[1:]) if depth == 1 else p[l] for p in params]
        x = _layer(x, positions, *layer)
    return x
```

```python
import functools

import jax
import jax.numpy as jnp
from jax import lax
from jax.experimental import pallas as pl
from jax.experimental.pallas import tpu as pltpu

HEAD_DIM = 128
MOBA_BLOCK = 256
MOBA_TOP_K = 3
ROPE_THETA = 10000.0
EPS = 1e-6

LANES = 128
SUBLANES = 8
VMEM_LIMIT_BYTES = 56 * 1024 * 1024
FFN_ROW_CHUNK = 64
CONV_ROW_GROUP = 128
MXU_ROW_BLOCK = 128
A_RING_AHEAD = 2
ATTN_SCORES_AHEAD = 3
ATTN_OUTPUT_BEHIND = 2

F32 = jnp.float32
BF16 = jnp.bfloat16


def _params(n_grid_dims):
    return pltpu.CompilerParams(
        dimension_semantics=("arbitrary",) * n_grid_dims,
        vmem_limit_bytes=VMEM_LIMIT_BYTES)


def _rmsnorm_kernel(x_ref, g_ref, o_ref):
    x = x_ref[...]
    ms = jnp.mean(x * x, axis=-1, keepdims=True)
    o_ref[...] = (x * lax.rsqrt(ms + EPS) * g_ref[...]).astype(o_ref.dtype)


def _rmsnorm(x, g, tr=512):
    t, d = x.shape
    return pl.pallas_call(
        _rmsnorm_kernel,
        grid=(t // tr,),
        in_specs=[pl.BlockSpec((tr, d), lambda i: (i, 0)),
                  pl.BlockSpec((1, d), lambda i: (0, 0))],
        out_specs=pl.BlockSpec((tr, d), lambda i: (i, 0)),
        out_shape=jax.ShapeDtypeStruct((t, d), BF16),
        compiler_params=_params(1),
        name="rmsnorm",
    )(x, g.reshape(1, d))


def _rope_table_kernel(pos_ref, invf_ref, sign_ref, cos_ref, sin_ref):
    ang = pos_ref[...] * invf_ref[...]
    cos_ref[...] = jnp.cos(ang)
    sin_ref[...] = jnp.sin(ang) * sign_ref[...]


def _rope_tables(positions, tr=1024):
    t = positions.size
    half = HEAD_DIM // 2
    inv = ROPE_THETA ** (-jnp.arange(half, dtype=F32) / half)
    invf = jnp.concatenate([inv, inv]).reshape(1, HEAD_DIM)
    sign = jnp.concatenate([-jnp.ones((half,), F32), jnp.ones((half,), F32)]).reshape(1, HEAD_DIM)
    pos = positions.astype(F32).reshape(t, 1)
    return pl.pallas_call(
        _rope_table_kernel,
        grid=(t // tr,),
        in_specs=[pl.BlockSpec((tr, 1), lambda i: (i, 0)),
                  pl.BlockSpec((1, HEAD_DIM), lambda i: (0, 0)),
                  pl.BlockSpec((1, HEAD_DIM), lambda i: (0, 0))],
        out_specs=[pl.BlockSpec((tr, HEAD_DIM), lambda i: (i, 0))] * 2,
        out_shape=[jax.ShapeDtypeStruct((t, HEAD_DIM), F32)] * 2,
        compiler_params=_params(1),
        name="rope_tables",
    )(pos, invf, sign)


def _ws_kernel(*refs, n_a, n_w, dots, n_extra, n_out, epilogue, n_row_tiles, n_tiles, pad, lag,
               lane_split, m_split, a_ring):
    a_refs = refs[:n_a]
    w_refs = refs[n_a:n_a + n_w]
    extra_refs = refs[n_a + n_w:n_a + n_w + n_extra]
    out_refs = refs[n_a + n_w + n_extra:n_a + n_w + n_extra + n_out]
    scratch = refs[n_a + n_w + n_extra + n_out:]
    if a_ring:
        a_buf, a_sem = scratch[-2:]
        scratch = scratch[:-2]
    to_cast = [w_ref for w_ref in w_refs if w_ref.dtype != BF16]
    cast_refs = scratch[:len(to_cast)]
    n_slots = 1 + lag
    n_raw = n_slots * len(dots)
    raw_refs = scratch[len(to_cast):len(to_cast) + n_raw]
    user_scratch = scratch[len(to_cast) + n_raw:]
    s = pl.program_id(0)
    tile = jnp.minimum(s, n_tiles - 1)

    if a_ring:
        depth = a_buf.shape[0]
        tm_a = a_buf.shape[1]

        def a_copy(t):
            rows = pl.ds(pl.multiple_of((t % n_row_tiles) * tm_a, tm_a), tm_a)
            slot = t % depth
            return pltpu.make_async_copy(a_refs[0].at[rows, :], a_buf.at[slot], a_sem.at[slot])

        @pl.when(s == 0)
        def _prime():
            for t in range(min(A_RING_AHEAD, n_tiles)):
                a_copy(t).start()

        @pl.when(s + A_RING_AHEAD < n_tiles)
        def _prefetch():
            a_copy(s + A_RING_AHEAD).start()

        @pl.when(s < n_tiles)
        def _arrive():
            a_copy(s).wait()

        a_tiles = [a_buf.at[tile % depth]]
    else:
        a_tiles = a_refs

    if to_cast:
        @pl.when((tile % n_row_tiles == 0) & (s < n_tiles))
        def _cast_weights():
            for w_ref, wbf_ref in zip(to_cast, cast_refs):
                wbf_ref[...] = w_ref[...].astype(BF16)

    if lag or user_scratch:
        @pl.when(s == 0)
        def _init():
            if lag:
                for raw in raw_refs[1::2]:
                    raw[...] = jnp.zeros(raw.shape, F32)
            for u in user_scratch:
                u[...] = jnp.zeros(u.shape, u.dtype)

    cast_iter = iter(cast_refs)
    wbf_refs = [w_ref if w_ref.dtype == BF16 else next(cast_iter) for w_ref in w_refs]
    done = jnp.maximum(s - lag, 0)

    def step(slot):
        tm = a_tiles[0].shape[0]
        rb = tm // m_split
        for k in range(m_split):
            rows = (k * rb, (k + 1) * rb)
            for d, (ai, wi) in enumerate(dots):
                res = jnp.dot(a_tiles[ai][rows[0]:rows[1], :], wbf_refs[wi][...],
                              preferred_element_type=F32)
                raw = raw_refs[n_slots * d + slot]
                if lane_split:
                    for lt in range(raw.shape[0]):
                        raw[lt, pad + rows[0]:pad + rows[1], :] = res[:, lt * LANES:(lt + 1) * LANES]
                else:
                    raw[pad + rows[0]:pad + rows[1], :] = res
            epilogue(done // n_row_tiles, done % n_row_tiles,
                     [raw_refs[n_slots * d + (slot + lag) % n_slots] for d in range(len(dots))],
                     extra_refs, out_refs, user_scratch, rows)

    if lag:
        for slot in range(2):
            pl.when(s % 2 == slot)(functools.partial(step, slot))
    else:
        step(0)


def _ws_matmul(name, a_list, w_list, dots, n_col_tiles, tm, tn, epilogue,
               extras=(), extra_specs=(), out_shapes=(), out_specs=(), scratch=(), pad=0, lag=0,
               lane_split=False, m_split=1, a_ring=False):
    t = a_list[0].shape[0]
    n_row_tiles = t // tm
    n_tiles = n_col_tiles * n_row_tiles

    def dot_tile(s):
        return jnp.minimum(s, n_tiles - 1)

    def lagged(index_map):
        def wrapped(s):
            done = jnp.maximum(s - lag, 0)
            return index_map(done // n_row_tiles, done % n_row_tiles)
        return wrapped

    ring_scratch = []
    if a_ring:
        (a,) = a_list
        in_specs = [pl.BlockSpec(memory_space=pl.ANY)]
        ring_scratch = [pltpu.VMEM((A_RING_AHEAD + 1, tm, a.shape[1]), a.dtype),
                        pltpu.SemaphoreType.DMA((A_RING_AHEAD + 1,))]
    else:
        in_specs = [pl.BlockSpec((tm, a.shape[1]), lambda s: (dot_tile(s) % n_row_tiles, 0))
                    for a in a_list]
    for w, off in w_list:
        in_specs.append(pl.BlockSpec((w.shape[0], tn), functools.partial(
            lambda s, off: (0, dot_tile(s) // n_row_tiles + off), off=off)))
    in_specs.extend(pl.BlockSpec(shape, lagged(fn)) for shape, fn in extra_specs)
    wbf_scratch = [pltpu.VMEM((w.shape[0], tn), BF16) for w, _ in w_list if w.dtype != BF16]
    raw_shape = (tn // LANES, pad + tm, LANES) if lane_split else (pad + tm, tn)
    raw_scratch = [pltpu.VMEM(raw_shape, F32) for _ in range((1 + lag) * len(dots))]
    kernel = functools.partial(
        _ws_kernel, n_a=len(a_list), n_w=len(w_list), dots=tuple(dots),
        n_extra=len(extras), n_out=len(out_shapes), epilogue=epilogue,
        n_row_tiles=n_row_tiles, n_tiles=n_tiles, pad=pad, lag=lag, lane_split=lane_split,
        m_split=m_split, a_ring=a_ring)
    return pl.pallas_call(
        kernel,
        grid=(n_tiles + lag,),
        in_specs=in_specs,
        out_specs=[pl.BlockSpec(shape, lagged(fn)) for shape, fn in out_specs],
        out_shape=list(out_shapes),
        scratch_shapes=wbf_scratch + raw_scratch + list(scratch) + ring_scratch,
        compiler_params=_params(1),
        name=name,
    )(*a_list, *[w for w, _ in w_list], *extras)


def _tile_spec(tm, tn, col_offset=0):
    return ((tm, tn), lambda j, i: (i, j + col_offset))


def _qk_epilogue(j, i, acc_refs, extra_refs, out_refs, scratch, rows, *, n_q_tiles, tm, tn):
    assert rows == (0, tm)
    cos_ref, sin_ref, g_ref = extra_refs
    o_ref, mean_ref = out_refs
    acc_ref = acc_refs[0]
    is_q = j < n_q_tiles
    g = jnp.where(is_q, g_ref[0:1, :], g_ref[1:2, :])
    scale = jnp.where(is_q, jnp.float32(HEAD_DIM ** -0.5), jnp.float32(1.0))
    cos = cos_ref[...]
    sin = sin_ref[...]
    rows_per_tile = tm // MOBA_BLOCK
    for h in range(tn // HEAD_DIM):
        cols = slice(h * HEAD_DIM, (h + 1) * HEAD_DIM)
        x = acc_ref[:, cols]
        ms = jnp.mean(x * x, axis=-1, keepdims=True)
        y = x * lax.rsqrt(ms + EPS) * g
        y = y * cos + pltpu.roll(y, HEAD_DIM // 2, 1) * sin
        o_ref[:, cols] = (y * scale).astype(o_ref.dtype)
        for r in range(rows_per_tile):
            blk = y[r * MOBA_BLOCK:(r + 1) * MOBA_BLOCK]
            mean_ref[0, r:r + 1, cols] = jnp.mean(blk, axis=0, keepdims=True)


def _store_bf16_epilogue(j, i, acc_refs, extra_refs, out_refs, scratch, rows):
    r = slice(*rows)
    out_refs[0][r, :] = acc_refs[0][r, :].astype(out_refs[0].dtype)


def _glu_epilogue(j, i, acc_refs, extra_refs, out_refs, scratch, rows):
    r = slice(*rows)
    out_refs[0][r, :] = acc_refs[0][r, :] * jax.nn.sigmoid(acc_refs[1][r, :])


def _sigmoid_epilogue(j, i, acc_refs, extra_refs, out_refs, scratch, rows):
    r = slice(*rows)
    out_refs[0][r, :] = jax.nn.sigmoid(acc_refs[0][r, :])


def _with_side_cast(epilogue):
    def wrapped(j, i, acc_refs, extra_refs, out_refs, scratch, rows):
        if rows[0] == 0:
            out_refs[-1][...] = extra_refs[-1][...].astype(out_refs[-1].dtype)
        epilogue(j, i, acc_refs, extra_refs[:-1], out_refs[:-1], scratch, rows)
    return wrapped


def _merge_epilogue(j, i, acc_refs, extra_refs, out_refs, scratch, rows):
    ga_ref, gc_ref = extra_refs
    r = slice(*rows)
    merged = ga_ref[r, :] * acc_refs[0][r, :] + gc_ref[r, :] * acc_refs[1][r, :]
    out_refs[0][r, :] = merged.astype(out_refs[0].dtype)


def _residual_epilogue(j, i, acc_refs, extra_refs, out_refs, scratch, rows):
    r = slice(*rows)
    out_refs[0][r, :] = extra_refs[0][r, :] + acc_refs[0][r, :]


def _ffn_in_epilogue(j, i, acc_refs, extra_refs, out_refs, scratch, rows, *, tm, tiles_per_seq,
                     width):
    cwg_ref, cwu_ref, cbg_ref, cbu_ref = extra_refs
    pad = SUBLANES
    seq_start = i % tiles_per_seq == 0
    if rows[0] == 0:
        for hp, carry in zip(acc_refs, scratch):
            prev = carry[...]
            hp[:, 0:pad, :] = jnp.where(seq_start, jnp.zeros_like(prev), prev)
    for lt in range(acc_refs[0].shape[0]):
        lanes = slice(lt * LANES, (lt + 1) * LANES)
        for r in range(rows[0], rows[1], FFN_ROW_CHUNK):
            convs = []
            for hp, cw_ref, cb_ref in zip(acc_refs, (cwg_ref, cwu_ref), (cbg_ref, cbu_ref)):
                y = cb_ref[:, lanes]
                for w in range(width):
                    lo = pad + r - (width - 1 - w)
                    y = y + cw_ref[w:w + 1, lanes] * hp[lt, lo:lo + FFN_ROW_CHUNK, :]
                convs.append(y)
            gate, up = convs
            out_refs[0][r:r + FFN_ROW_CHUNK, lanes] = (
                jax.nn.silu(gate) * up).astype(out_refs[0].dtype)
    if rows[1] == tm:
        for hp, carry in zip(acc_refs, scratch):
            carry[...] = hp[:, tm:tm + pad, :]


def _attn_kernel(q_ref, k_ref, v_ref, km_ref, o_ref, *, n_blocks):
    L = MOBA_BLOCK
    nt = (((1,), (1,)), ((), ()))
    tn = (((0,), (0,)), ((), ()))
    km = km_ref[...].astype(BF16)
    neg = jnp.float32(-jnp.inf)
    key = lax.broadcasted_iota(jnp.int32, (L, L), 0)
    qry = lax.broadcasted_iota(jnp.int32, (L, L), 1)
    causal = key <= qry
    blk = lax.broadcasted_iota(jnp.int32, (n_blocks, L), 0)

    def scores(i):
        qi = q_ref[i * L:(i + 1) * L, :]
        n_keys = (i + 1) * L
        st = lax.dot_general(k_ref[0:n_keys, :], qi, nt, preferred_element_type=F32)
        gate = None
        if i > MOBA_TOP_K:
            gate = lax.dot_general(km, qi, nt, preferred_element_type=F32)
        return st, gate

    def probabilities(i, st, gate):
        pen = None
        if gate is not None:
            rank = jnp.zeros((n_blocks, L), F32)
            for jp in range(i):
                gb = jnp.broadcast_to(gate[jp:jp + 1, :], (n_blocks, L))
                beats = (gb > gate) | ((gb == gate) & (jp < blk))
                rank = rank + jnp.where(beats, 1.0, 0.0)
            pen = jnp.where(rank < MOBA_TOP_K, 0.0, neg)
        chunks = []
        for jb in range(i + 1):
            sj = st[jb * L:(jb + 1) * L, :]
            if jb == i:
                sj = jnp.where(causal, sj, neg)
            elif pen is not None:
                sj = sj + pen[jb:jb + 1, :]
            chunks.append(sj)
        top = chunks[0]
        for c in chunks[1:]:
            top = jnp.maximum(top, c)
        m = top.max(axis=0, keepdims=True)
        es = [jnp.exp(c - m) for c in chunks]
        tot = es[0]
        for e in es[1:]:
            tot = tot + e
        inv_l = 1.0 / tot.sum(axis=0, keepdims=True)
        return jnp.concatenate([(e * inv_l).astype(BF16) for e in es], axis=0)

    def output(i, pt):
        n_keys = (i + 1) * L
        ot = lax.dot_general(v_ref[0:n_keys, :], pt, tn, preferred_element_type=F32)
        o_ref[i * L:(i + 1) * L, :] = ot.T.astype(o_ref.dtype)

    st = {}
    pt = {}
    for i in range(min(ATTN_SCORES_AHEAD, n_blocks)):
        st[i] = scores(i)
    for i in range(n_blocks):
        pt[i] = probabilities(i, *st.pop(i))
        if i + ATTN_SCORES_AHEAD < n_blocks:
            st[i + ATTN_SCORES_AHEAD] = scores(i + ATTN_SCORES_AHEAD)
        if i >= ATTN_OUTPUT_BEHIND:
            output(i - ATTN_OUTPUT_BEHIND, pt.pop(i - ATTN_OUTPUT_BEHIND))
    for i in sorted(pt):
        output(i, pt[i])


def _moba_attention(qk, v, kmean, batch, seq, n_heads):
    t = qk.shape[0]
    n_blocks = seq // MOBA_BLOCK
    return pl.pallas_call(
        functools.partial(_attn_kernel, n_blocks=n_blocks),
        grid=(batch, n_heads),
        in_specs=[pl.BlockSpec((seq, HEAD_DIM), lambda b, h: (b, h)),
                  pl.BlockSpec((seq, HEAD_DIM), lambda b, h: (b, n_heads + h)),
                  pl.BlockSpec((seq, HEAD_DIM), lambda b, h: (b, h)),
                  pl.BlockSpec((n_blocks, HEAD_DIM), lambda b, h: (b, n_heads + h))],
        out_specs=pl.BlockSpec((seq, HEAD_DIM), lambda b, h: (b, h)),
        out_shape=jax.ShapeDtypeStruct((t, n_heads * HEAD_DIM), BF16),
        compiler_params=_params(2),
        name="moba_attention",
    )(qk, qk, v, kmean)


def _conv_kernel(h_ref, halo_ref, cw_ref, cb_ref, g_ref, b_ref, o_ref, xpad_ref, y_ref, z_ref,
                 *, ts, halo, width):
    t = pl.program_id(1)
    prev = halo_ref[...]
    xpad_ref[0:halo, :] = jnp.where(t == 0, jnp.zeros_like(prev), prev)
    xpad_ref[halo:, :] = h_ref[...]
    n_chunks = h_ref.shape[1] // LANES
    base = halo - (width - 1)

    def chunk(c, carry):
        cols = pl.ds(pl.multiple_of(c * LANES, LANES), LANES)
        groups = []
        for r in range(SUBLANES):
            taps = [w for w in range(width) if (base + w) % SUBLANES == r]
            if taps:
                span = taps[-1] - taps[0] + ts
                z_ref[r, 0:span, :] = xpad_ref[base + taps[0]:base + taps[0] + span, cols]
                groups.append((r, taps))
        for r0 in range(0, ts, CONV_ROW_GROUP):
            acc = jnp.broadcast_to(cb_ref[:, cols], (CONV_ROW_GROUP, LANES))
            for r, taps in groups:
                for w in taps:
                    off = r0 + w - taps[0]
                    acc = acc + cw_ref[w:w + 1, cols] * z_ref[r, off:off + CONV_ROW_GROUP, :]
            y_ref[r0:r0 + CONV_ROW_GROUP, cols] = acc
        return carry

    lax.fori_loop(0, n_chunks, chunk, 0)
    y = y_ref[...]
    mu = jnp.mean(y, axis=-1, keepdims=True)
    yc = y - mu
    var = jnp.mean(yc * yc, axis=-1, keepdims=True)
    z = yc * lax.rsqrt(var + EPS) * g_ref[...] + b_ref[...]
    o_ref[...] = jax.nn.silu(z).astype(o_ref.dtype)


def _conv_module(h, conv_w, conv_b, ln_g, ln_b, batch, seq, ts=256, halo=32):
    t, c = h.shape
    width = conv_w.shape[0]
    tiles = seq // ts
    halo_per_tile = ts // halo

    def halo_map(b, s):
        return (jnp.maximum((b * tiles + s) * halo_per_tile - 1, 0), 0)

    vec = lambda: pl.BlockSpec((1, c), lambda b, s: (0, 0))
    return pl.pallas_call(
        functools.partial(_conv_kernel, ts=ts, halo=halo, width=width),
        grid=(batch, tiles),
        in_specs=[pl.BlockSpec((ts, c), lambda b, s: (b * tiles + s, 0)),
                  pl.BlockSpec((halo, c), halo_map),
                  pl.BlockSpec((width, c), lambda b, s: (0, 0)),
                  vec(), vec(), vec()],
        out_specs=pl.BlockSpec((ts, c), lambda b, s: (b * tiles + s, 0)),
        out_shape=jax.ShapeDtypeStruct((t, c), BF16),
        scratch_shapes=[pltpu.VMEM((halo + ts, c), F32), pltpu.VMEM((ts, c), F32),
                        pltpu.VMEM((SUBLANES, ts + halo, LANES), F32)],
        compiler_params=_params(2),
        name="conv_module",
    )(h, h, conv_w, conv_b.reshape(1, c), ln_g.reshape(1, c), ln_b.reshape(1, c))


def _layer(x, positions, norm1_g, w_in, q_norm_g, k_norm_g, w_o_attn, conv_w, conv_b,
           conv_ln_g, conv_ln_b, w_o_conv, w_out, norm2_g, w_ffn_in, ffn_conv_w,
           ffn_conv_b, w_ffn_out):
    batch, seq, d_model = x.shape
    t = batch * seq
    attn_width = w_o_attn.shape[0]
    conv_dim = w_o_conv.shape[0]
    d_ff = w_ffn_out.shape[0]
    n_heads = attn_width // HEAD_DIM
    assert seq % MOBA_BLOCK == 0
    assert w_in.shape[1] == 3 * attn_width + 2 * conv_dim + 2 * d_model

    x2 = x.reshape(t, d_model)
    xn = _rmsnorm(x2, norm1_g)
    cos, sin = _rope_tables(positions)

    tm, tn = 1024, 512
    qk_g = jnp.stack([q_norm_g, k_norm_g])
    n_q_tiles = attn_width // tn
    qk, kmean = _ws_matmul(
        "qk_proj", [xn], [(w_in, 0)], [(0, 0)], 2 * n_q_tiles, tm, tn,
        functools.partial(_qk_epilogue, n_q_tiles=n_q_tiles, tm=tm, tn=tn), lag=1, a_ring=True,
        extras=(cos, sin, qk_g),
        extra_specs=(((tm, HEAD_DIM), lambda j, i: (i, 0)),
                     ((tm, HEAD_DIM), lambda j, i: (i, 0)),
                     ((2, HEAD_DIM), lambda j, i: (0, 0))),
        out_shapes=(jax.ShapeDtypeStruct((t, 2 * attn_width), BF16),
                    jax.ShapeDtypeStruct((t // tm, tm // MOBA_BLOCK, 2 * attn_width), F32)),
        out_specs=(_tile_spec(tm, tn),
                   ((1, tm // MOBA_BLOCK, tn), lambda j, i: (i, 0, j))))
    kmean = kmean.reshape(t // MOBA_BLOCK, 2 * attn_width)
    tmv, tnv = tm, tn
    (v,) = _ws_matmul(
        "v_proj", [xn], [(w_in, 2 * attn_width // tnv)], [(0, 0)], attn_width // tnv, tmv, tnv,
        _store_bf16_epilogue, a_ring=True,
        out_shapes=(jax.ShapeDtypeStruct((t, attn_width), BF16),),
        out_specs=(_tile_spec(tmv, tnv),))
    tg = 256
    u_off = 3 * attn_width
    (h,) = _ws_matmul(
        "glu_proj", [xn], [(w_in, u_off // tg), (w_in, (u_off + conv_dim) // tg)],
        [(0, 0), (0, 1)], conv_dim // tg, tm, tg, _glu_epilogue, a_ring=True,
        out_shapes=(jax.ShapeDtypeStruct((t, conv_dim), F32),),
        out_specs=(_tile_spec(tm, tg),))
    g_off = u_off + 2 * conv_dim
    (gates,) = _ws_matmul(
        "gate_proj", [xn], [(w_in, g_off // tn)], [(0, 0)], 2 * d_model // tn, tm, tn,
        _sigmoid_epilogue, lag=1, m_split=tm // MXU_ROW_BLOCK, a_ring=True,
        out_shapes=(jax.ShapeDtypeStruct((t, 2 * d_model), F32),),
        out_specs=(_tile_spec(tm, tn),))

    attn = _moba_attention(qk, v, kmean, batch, seq, n_heads)
    hc = _conv_module(h, conv_w, conv_b, conv_ln_g, conv_ln_b, batch, seq)

    n_model_tiles = d_model // tn
    tmo = 512
    tmm = 1024
    (merged,) = _ws_matmul(
        "merge_proj", [attn, hc], [(w_o_attn, 0), (w_o_conv, 0)], [(0, 0), (1, 1)],
        n_model_tiles, tmm, tn, _merge_epilogue,
        extras=(gates, gates),
        extra_specs=(_tile_spec(tmm, tn), _tile_spec(tmm, tn, n_model_tiles)),
        out_shapes=(jax.ShapeDtypeStruct((t, d_model), BF16),),
        out_specs=(_tile_spec(tmm, tn),))
    (x1,) = _ws_matmul(
        "out_proj", [merged], [(w_out, 0)], [(0, 0)], n_model_tiles, tm, tn,
        _residual_epilogue, a_ring=True,
        extras=(x2,), extra_specs=(_tile_spec(tm, tn),),
        out_shapes=(jax.ShapeDtypeStruct((t, d_model), F32),),
        out_specs=(_tile_spec(tm, tn),))

    xn2 = _rmsnorm(x1, norm2_g)
    tf = 256
    assert d_ff % tf == 0 and seq % tm == 0
    n_ff_tiles = d_ff // tf
    fcw = ffn_conv_w
    fcb = ffn_conv_b.reshape(1, 2 * d_ff)
    width = fcw.shape[0]
    n_ff_steps = n_ff_tiles * (t // tm)
    assert d_ff % (n_ff_steps * SUBLANES) == 0
    slab = ((d_ff // n_ff_steps, d_model), lambda j, i: (j * (t // tm) + i, 0))
    act, w_ffn_out_bf = _ws_matmul(
        "ffn_in", [xn2], [(w_ffn_in, 0), (w_ffn_in, n_ff_tiles)], [(0, 0), (0, 1)],
        n_ff_tiles, tm, tf,
        _with_side_cast(functools.partial(
            _ffn_in_epilogue, tm=tm, tiles_per_seq=seq // tm, width=width)),
        extras=(fcw, fcw, fcb, fcb, w_ffn_out),
        extra_specs=(((width, tf), lambda j, i: (0, j)),
                     ((width, tf), lambda j, i: (0, j + n_ff_tiles)),
                     ((1, tf), lambda j, i: (0, j)),
                     ((1, tf), lambda j, i: (0, j + n_ff_tiles)),
                     slab),
        out_shapes=(jax.ShapeDtypeStruct((t, d_ff), BF16),
                    jax.ShapeDtypeStruct((d_ff, d_model), BF16)),
        out_specs=(_tile_spec(tm, tf), slab),
        scratch=[pltpu.VMEM((tf // LANES, SUBLANES, LANES), F32)] * 2,
        pad=SUBLANES, lane_split=True, lag=1, m_split=tm // MXU_ROW_BLOCK, a_ring=True)

    (out,) = _ws_matmul(
        "ffn_out", [act], [(w_ffn_out_bf, 0)], [(0, 0)], n_model_tiles, tmo, tn,
        _residual_epilogue,
        extras=(x1,), extra_specs=(_tile_spec(tmo, tn),),
        out_shapes=(jax.ShapeDtypeStruct((t, d_model), F32),),
        out_specs=(_tile_spec(tmo, tn),))
    return out.reshape(batch, seq, d_model)


def kernel(x, positions, norm1_g, w_in, q_norm_g, k_norm_g, w_o_attn, conv_w, conv_b,
           conv_ln_g, conv_ln_b, w_o_conv, w_out, norm2_g, w_ffn_in, ffn_conv_w,
           ffn_conv_b, w_ffn_out):
    depth = norm1_g.shape[0]
    params = (norm1_g, w_in, q_norm_g, k_norm_g, w_o_attn, conv_w, conv_b, conv_ln_g,
              conv_ln_b, w_o_conv, w_out, norm2_g, w_ffn_in, ffn_conv_w, ffn_conv_b, w_ffn_out)
    for l in range(depth):
        layer = [p.reshape(p.shape[1:]) if depth == 1 else p[l] for p in params]
        x = _layer(x, positions, *layer)
    return x
```

```python
import functools

import jax
import jax.numpy as jnp
from jax import lax
from jax.experimental import pallas as pl
from jax.experimental.pallas import tpu as pltpu

HEAD_DIM = 128
MOBA_BLOCK = 256
MOBA_TOP_K = 3
ROPE_THETA = 10000.0
EPS = 1e-6

LANES = 128
SUBLANES = 8
VMEM_LIMIT_BYTES = 56 * 1024 * 1024
NORM_ROW_TILE = 512
FFN_ROW_CHUNK = 64
CONV_ROW_GROUP = 128
MXU_ROW_BLOCK = 128
A_RING_AHEAD = 2
ATTN_SCORES_AHEAD = 3
ATTN_OUTPUT_BEHIND = 2

F32 = jnp.float32
BF16 = jnp.bfloat16


def _params(n_grid_dims):
    return pltpu.CompilerParams(
        dimension_semantics=("arbitrary",) * n_grid_dims,
        vmem_limit_bytes=VMEM_LIMIT_BYTES)


def _rmsnorm_kernel(x_ref, g_ref, o_ref):
    x = x_ref[...]
    ms = jnp.mean(x * x, axis=-1, keepdims=True)
    o_ref[...] = (x * lax.rsqrt(ms + EPS) * g_ref[...]).astype(o_ref.dtype)


def _rmsnorm_rope_kernel(x_ref, g_ref, pos_ref, invf_ref, sign_ref, o_ref, cos_ref, sin_ref):
    _rmsnorm_kernel(x_ref, g_ref, o_ref)
    ang = pos_ref[...] * invf_ref[...]
    cos_ref[...] = jnp.cos(ang)
    sin_ref[...] = jnp.sin(ang) * sign_ref[...]


def _rmsnorm(x, g, positions=None, tr=NORM_ROW_TILE):
    t, d = x.shape
    row_spec = pl.BlockSpec((tr, d), lambda i: (i, 0))
    const = lambda cols: pl.BlockSpec((1, cols), lambda i: (0, 0))
    if positions is None:
        return pl.pallas_call(
            _rmsnorm_kernel,
            grid=(t // tr,),
            in_specs=[row_spec, const(d)],
            out_specs=row_spec,
            out_shape=jax.ShapeDtypeStruct((t, d), BF16),
            compiler_params=_params(1),
            name="rmsnorm",
        )(x, g.reshape(1, d))
    half = HEAD_DIM // 2
    inv = ROPE_THETA ** (-jnp.arange(half, dtype=F32) / half)
    invf = jnp.concatenate([inv, inv]).reshape(1, HEAD_DIM)
    sign = jnp.concatenate([-jnp.ones((half,), F32), jnp.ones((half,), F32)]).reshape(1, HEAD_DIM)
    pos = positions.astype(F32).reshape(t, 1)
    table_spec = pl.BlockSpec((tr, HEAD_DIM), lambda i: (i, 0))
    return pl.pallas_call(
        _rmsnorm_rope_kernel,
        grid=(t // tr,),
        in_specs=[row_spec, const(d), pl.BlockSpec((tr, 1), lambda i: (i, 0)),
                  const(HEAD_DIM), const(HEAD_DIM)],
        out_specs=[row_spec, table_spec, table_spec],
        out_shape=[jax.ShapeDtypeStruct((t, d), BF16),
                   jax.ShapeDtypeStruct((t, HEAD_DIM), F32),
                   jax.ShapeDtypeStruct((t, HEAD_DIM), F32)],
        compiler_params=_params(1),
        name="rmsnorm_rope",
    )(x, g.reshape(1, d), pos, invf, sign)


def _ws_kernel(*refs, n_a, n_w, dots, n_extra, n_out, epilogue, n_row_tiles, n_tiles, pad, lag,
               lane_split, m_split, a_ring):
    a_refs = refs[:n_a]
    w_refs = refs[n_a:n_a + n_w]
    extra_refs = refs[n_a + n_w:n_a + n_w + n_extra]
    out_refs = refs[n_a + n_w + n_extra:n_a + n_w + n_extra + n_out]
    scratch = refs[n_a + n_w + n_extra + n_out:]
    if a_ring:
        a_buf, a_sem = scratch[-2:]
        scratch = scratch[:-2]
    to_cast = [w_ref for w_ref in w_refs if w_ref.dtype != BF16]
    cast_refs = scratch[:len(to_cast)]
    n_slots = 1 + lag
    n_raw = n_slots * len(dots)
    raw_refs = scratch[len(to_cast):len(to_cast) + n_raw]
    user_scratch = scratch[len(to_cast) + n_raw:]
    s = pl.program_id(0)
    tile = jnp.minimum(s, n_tiles - 1)

    if a_ring:
        depth = a_buf.shape[0]
        tm_a = a_buf.shape[1]

        def a_copy(t):
            rows = pl.ds(pl.multiple_of((t % n_row_tiles) * tm_a, tm_a), tm_a)
            slot = t % depth
            return pltpu.make_async_copy(a_refs[0].at[rows, :], a_buf.at[slot], a_sem.at[slot])

        @pl.when(s == 0)
        def _prime():
            for t in range(min(A_RING_AHEAD, n_tiles)):
                a_copy(t).start()

        @pl.when(s + A_RING_AHEAD < n_tiles)
        def _prefetch():
            a_copy(s + A_RING_AHEAD).start()

        @pl.when(s < n_tiles)
        def _arrive():
            a_copy(s).wait()

        a_tiles = [a_buf.at[tile % depth]]
    else:
        a_tiles = a_refs

    if to_cast:
        @pl.when((tile % n_row_tiles == 0) & (s < n_tiles))
        def _cast_weights():
            for w_ref, wbf_ref in zip(to_cast, cast_refs):
                wbf_ref[...] = w_ref[...].astype(BF16)

    if lag or user_scratch:
        @pl.when(s == 0)
        def _init():
            if lag:
                for raw in raw_refs[1::2]:
                    raw[...] = jnp.zeros(raw.shape, F32)
            for u in user_scratch:
                u[...] = jnp.zeros(u.shape, u.dtype)

    cast_iter = iter(cast_refs)
    wbf_refs = [w_ref if w_ref.dtype == BF16 else next(cast_iter) for w_ref in w_refs]
    done = jnp.maximum(s - lag, 0)

    def step(slot):
        tm = a_tiles[0].shape[0]
        rb = tm // m_split
        for k in range(m_split):
            rows = (k * rb, (k + 1) * rb)
            for d, (ai, wi) in enumerate(dots):
                res = jnp.dot(a_tiles[ai][rows[0]:rows[1], :], wbf_refs[wi][...],
                              preferred_element_type=F32)
                raw = raw_refs[n_slots * d + slot]
                if lane_split:
                    for lt in range(raw.shape[0]):
                        raw[lt, pad + rows[0]:pad + rows[1], :] = res[:, lt * LANES:(lt + 1) * LANES]
                else:
                    raw[pad + rows[0]:pad + rows[1], :] = res
            epilogue(done // n_row_tiles, done % n_row_tiles,
                     [raw_refs[n_slots * d + (slot + lag) % n_slots] for d in range(len(dots))],
                     extra_refs, out_refs, user_scratch, rows)

    if lag:
        for slot in range(2):
            pl.when(s % 2 == slot)(functools.partial(step, slot))
    else:
        step(0)


def _ws_matmul(name, a_list, w_list, dots, n_col_tiles, tm, tn, epilogue,
               extras=(), extra_specs=(), out_shapes=(), out_specs=(), scratch=(), pad=0, lag=0,
               lane_split=False, m_split=1, a_ring=False):
    t = a_list[0].shape[0]
    n_row_tiles = t // tm
    n_tiles = n_col_tiles * n_row_tiles

    def dot_tile(s):
        return jnp.minimum(s, n_tiles - 1)

    def lagged(index_map):
        def wrapped(s):
            done = jnp.maximum(s - lag, 0)
            return index_map(done // n_row_tiles, done % n_row_tiles)
        return wrapped

    ring_scratch = []
    if a_ring:
        (a,) = a_list
        in_specs = [pl.BlockSpec(memory_space=pl.ANY)]
        ring_scratch = [pltpu.VMEM((A_RING_AHEAD + 1, tm, a.shape[1]), a.dtype),
                        pltpu.SemaphoreType.DMA((A_RING_AHEAD + 1,))]
    else:
        in_specs = [pl.BlockSpec((tm, a.shape[1]), lambda s: (dot_tile(s) % n_row_tiles, 0))
                    for a in a_list]
    for w, off in w_list:
        in_specs.append(pl.BlockSpec((w.shape[0], tn), functools.partial(
            lambda s, off: (0, dot_tile(s) // n_row_tiles + off), off=off)))
    in_specs.extend(pl.BlockSpec(shape, lagged(fn)) for shape, fn in extra_specs)
    wbf_scratch = [pltpu.VMEM((w.shape[0], tn), BF16) for w, _ in w_list if w.dtype != BF16]
    raw_shape = (tn // LANES, pad + tm, LANES) if lane_split else (pad + tm, tn)
    raw_scratch = [pltpu.VMEM(raw_shape, F32) for _ in range((1 + lag) * len(dots))]
    kernel = functools.partial(
        _ws_kernel, n_a=len(a_list), n_w=len(w_list), dots=tuple(dots),
        n_extra=len(extras), n_out=len(out_shapes), epilogue=epilogue,
        n_row_tiles=n_row_tiles, n_tiles=n_tiles, pad=pad, lag=lag, lane_split=lane_split,
        m_split=m_split, a_ring=a_ring)
    return pl.pallas_call(
        kernel,
        grid=(n_tiles + lag,),
        in_specs=in_specs,
        out_specs=[pl.BlockSpec(shape, lagged(fn)) for shape, fn in out_specs],
        out_shape=list(out_shapes),
        scratch_shapes=wbf_scratch + raw_scratch + list(scratch) + ring_scratch,
        compiler_params=_params(1),
        name=name,
    )(*a_list, *[w for w, _ in w_list], *extras)


def _tile_spec(tm, tn, col_offset=0):
    return ((tm, tn), lambda j, i: (i, j + col_offset))


def _qk_epilogue(j, i, acc_refs, extra_refs, out_refs, scratch, rows, *, n_q_tiles, tm, tn):
    assert rows == (0, tm)
    cos_ref, sin_ref, g_ref = extra_refs
    o_ref, mean_ref = out_refs
    acc_ref = acc_refs[0]
    is_q = j < n_q_tiles
    g = jnp.where(is_q, g_ref[0:1, :], g_ref[1:2, :])
    scale = jnp.where(is_q, jnp.float32(HEAD_DIM ** -0.5), jnp.float32(1.0))
    cos = cos_ref[...]
    sin = sin_ref[...]
    rows_per_tile = tm // MOBA_BLOCK
    for h in range(tn // HEAD_DIM):
        cols = slice(h * HEAD_DIM, (h + 1) * HEAD_DIM)
        x = acc_ref[:, cols]
        ms = jnp.mean(x * x, axis=-1, keepdims=True)
        y = x * lax.rsqrt(ms + EPS) * g
        y = y * cos + pltpu.roll(y, HEAD_DIM // 2, 1) * sin
        o_ref[:, cols] = (y * scale).astype(o_ref.dtype)
        for r in range(rows_per_tile):
            blk = y[r * MOBA_BLOCK:(r + 1) * MOBA_BLOCK]
            mean_ref[0, r:r + 1, cols] = jnp.mean(blk, axis=0, keepdims=True)


def _store_bf16_epilogue(j, i, acc_refs, extra_refs, out_refs, scratch, rows):
    r = slice(*rows)
    out_refs[0][r, :] = acc_refs[0][r, :].astype(out_refs[0].dtype)


def _glu_epilogue(j, i, acc_refs, extra_refs, out_refs, scratch, rows):
    r = slice(*rows)
    out_refs[0][r, :] = acc_refs[0][r, :] * jax.nn.sigmoid(acc_refs[1][r, :])


def _sigmoid_epilogue(j, i, acc_refs, extra_refs, out_refs, scratch, rows):
    r = slice(*rows)
    out_refs[0][r, :] = jax.nn.sigmoid(acc_refs[0][r, :])


def _with_side_cast(epilogue):
    def wrapped(j, i, acc_refs, extra_refs, out_refs, scratch, rows):
        if rows[0] == 0:
            out_refs[-1][...] = extra_refs[-1][...].astype(out_refs[-1].dtype)
        epilogue(j, i, acc_refs, extra_refs[:-1], out_refs[:-1], scratch, rows)
    return wrapped


def _merge_epilogue(j, i, acc_refs, extra_refs, out_refs, scratch, rows):
    ga_ref, gc_ref = extra_refs
    r = slice(*rows)
    merged = ga_ref[r, :] * acc_refs[0][r, :] + gc_ref[r, :] * acc_refs[1][r, :]
    out_refs[0][r, :] = merged.astype(out_refs[0].dtype)


def _residual_epilogue(j, i, acc_refs, extra_refs, out_refs, scratch, rows):
    r = slice(*rows)
    out_refs[0][r, :] = extra_refs[0][r, :] + acc_refs[0][r, :]


def _ffn_in_epilogue(j, i, acc_refs, extra_refs, out_refs, scratch, rows, *, tm, tiles_per_seq,
                     width):
    cwg_ref, cwu_ref, cbg_ref, cbu_ref = extra_refs
    pad = SUBLANES
    seq_start = i % tiles_per_seq == 0
    if rows[0] == 0:
        for hp, carry in zip(acc_refs, scratch):
            prev = carry[...]
            hp[:, 0:pad, :] = jnp.where(seq_start, jnp.zeros_like(prev), prev)
    for lt in range(acc_refs[0].shape[0]):
        lanes = slice(lt * LANES, (lt + 1) * LANES)
        for r in range(rows[0], rows[1], FFN_ROW_CHUNK):
            convs = []
            for hp, cw_ref, cb_ref in zip(acc_refs, (cwg_ref, cwu_ref), (cbg_ref, cbu_ref)):
                y = cb_ref[:, lanes]
                for w in range(width):
                    lo = pad + r - (width - 1 - w)
                    y = y + cw_ref[w:w + 1, lanes] * hp[lt, lo:lo + FFN_ROW_CHUNK, :]
                convs.append(y)
            gate, up = convs
            out_refs[0][r:r + FFN_ROW_CHUNK, lanes] = (
                jax.nn.silu(gate) * up).astype(out_refs[0].dtype)
    if rows[1] == tm:
        for hp, carry in zip(acc_refs, scratch):
            carry[...] = hp[:, tm:tm + pad, :]


def _attn_kernel(q_ref, k_ref, v_ref, km_ref, o_ref, *, n_blocks):
    L = MOBA_BLOCK
    nt = (((1,), (1,)), ((), ()))
    tn = (((0,), (0,)), ((), ()))
    km = km_ref[...].astype(BF16)
    neg = jnp.float32(-jnp.inf)
    key = lax.broadcasted_iota(jnp.int32, (L, L), 0)
    qry = lax.broadcasted_iota(jnp.int32, (L, L), 1)
    causal = key <= qry
    blk = lax.broadcasted_iota(jnp.int32, (n_blocks, L), 0)

    def scores(i):
        qi = q_ref[i * L:(i + 1) * L, :]
        n_keys = (i + 1) * L
        st = lax.dot_general(k_ref[0:n_keys, :], qi, nt, preferred_element_type=F32)
        gate = None
        if i > MOBA_TOP_K:
            gate = lax.dot_general(km, qi, nt, preferred_element_type=F32)
        return st, gate

    def probabilities(i, st, gate):
        pen = None
        if gate is not None:
            rank = jnp.zeros((n_blocks, L), F32)
            for jp in range(i):
                gb = jnp.broadcast_to(gate[jp:jp + 1, :], (n_blocks, L))
                beats = (gb > gate) | ((gb == gate) & (jp < blk))
                rank = rank + jnp.where(beats, 1.0, 0.0)
            pen = jnp.where(rank < MOBA_TOP_K, 0.0, neg)
        chunks = []
        for jb in range(i + 1):
            sj = st[jb * L:(jb + 1) * L, :]
            if jb == i:
                sj = jnp.where(causal, sj, neg)
            elif pen is not None:
                sj = sj + pen[jb:jb + 1, :]
            chunks.append(sj)
        top = chunks[0]
        for c in chunks[1:]:
            top = jnp.maximum(top, c)
        m = top.max(axis=0, keepdims=True)
        es = [jnp.exp(c - m) for c in chunks]
        tot = es[0]
        for e in es[1:]:
            tot = tot + e
        inv_l = 1.0 / tot.sum(axis=0, keepdims=True)
        return jnp.concatenate([(e * inv_l).astype(BF16) for e in es], axis=0)

    def output(i, pt):
        n_keys = (i + 1) * L
        ot = lax.dot_general(v_ref[0:n_keys, :], pt, tn, preferred_element_type=F32)
        o_ref[i * L:(i + 1) * L, :] = ot.T.astype(o_ref.dtype)

    st = {}
    pt = {}
    for i in range(min(ATTN_SCORES_AHEAD, n_blocks)):
        st[i] = scores(i)
    for i in range(n_blocks):
        pt[i] = probabilities(i, *st.pop(i))
        if i + ATTN_SCORES_AHEAD < n_blocks:
            st[i + ATTN_SCORES_AHEAD] = scores(i + ATTN_SCORES_AHEAD)
        if i >= ATTN_OUTPUT_BEHIND:
            output(i - ATTN_OUTPUT_BEHIND, pt.pop(i - ATTN_OUTPUT_BEHIND))
    for i in sorted(pt):
        output(i, pt[i])


def _moba_attention(qk, v, kmean, batch, seq, n_heads):
    t = qk.shape[0]
    n_blocks = seq // MOBA_BLOCK
    return pl.pallas_call(
        functools.partial(_attn_kernel, n_blocks=n_blocks),
        grid=(batch, n_heads),
        in_specs=[pl.BlockSpec((seq, HEAD_DIM), lambda b, h: (b, h)),
                  pl.BlockSpec((seq, HEAD_DIM), lambda b, h: (b, n_heads + h)),
                  pl.BlockSpec((seq, HEAD_DIM), lambda b, h: (b, h)),
                  pl.BlockSpec((n_blocks, HEAD_DIM), lambda b, h: (b, n_heads + h))],
        out_specs=pl.BlockSpec((seq, HEAD_DIM), lambda b, h: (b, h)),
        out_shape=jax.ShapeDtypeStruct((t, n_heads * HEAD_DIM), BF16),
        compiler_params=_params(2),
        name="moba_attention",
    )(qk, qk, v, kmean)


def _conv_kernel(h_ref, halo_ref, cw_ref, cb_ref, g_ref, b_ref, o_ref, xpad_ref, y_ref, z_ref,
                 *, ts, halo, width):
    t = pl.program_id(1)
    prev = halo_ref[...]
    xpad_ref[0:halo, :] = jnp.where(t == 0, jnp.zeros_like(prev), prev)
    xpad_ref[halo:, :] = h_ref[...]
    n_chunks = h_ref.shape[1] // LANES
    base = halo - (width - 1)

    def chunk(c, carry):
        cols = pl.ds(pl.multiple_of(c * LANES, LANES), LANES)
        groups = []
        for r in range(SUBLANES):
            taps = [w for w in range(width) if (base + w) % SUBLANES == r]
            if taps:
                span = taps[-1] - taps[0] + ts
                z_ref[r, 0:span, :] = xpad_ref[base + taps[0]:base + taps[0] + span, cols]
                groups.append((r, taps))
        for r0 in range(0, ts, CONV_ROW_GROUP):
            acc = jnp.broadcast_to(cb_ref[:, cols], (CONV_ROW_GROUP, LANES))
            for r, taps in groups:
                for w in taps:
                    off = r0 + w - taps[0]
                    acc = acc + cw_ref[w:w + 1, cols] * z_ref[r, off:off + CONV_ROW_GROUP, :]
            y_ref[r0:r0 + CONV_ROW_GROUP, cols] = acc
        return carry

    lax.fori_loop(0, n_chunks, chunk, 0)
    y = y_ref[...]
    mu = jnp.mean(y, axis=-1, keepdims=True)
    yc = y - mu
    var = jnp.mean(yc * yc, axis=-1, keepdims=True)
    z = yc * lax.rsqrt(var + EPS) * g_ref[...] + b_ref[...]
    o_ref[...] = jax.nn.silu(z).astype(o_ref.dtype)


def _conv_module(h, conv_w, conv_b, ln_g, ln_b, batch, seq, ts=256, halo=32):
    t, c = h.shape
    width = conv_w.shape[0]
    tiles = seq // ts
    halo_per_tile = ts // halo

    def halo_map(b, s):
        return (jnp.maximum((b * tiles + s) * halo_per_tile - 1, 0), 0)

    vec = lambda: pl.BlockSpec((1, c), lambda b, s: (0, 0))
    return pl.pallas_call(
        functools.partial(_conv_kernel, ts=ts, halo=halo, width=width),
        grid=(batch, tiles),
        in_specs=[pl.BlockSpec((ts, c), lambda b, s: (b * tiles + s, 0)),
                  pl.BlockSpec((halo, c), halo_map),
                  pl.BlockSpec((width, c), lambda b, s: (0, 0)),
                  vec(), vec(), vec()],
        out_specs=pl.BlockSpec((ts, c), lambda b, s: (b * tiles + s, 0)),
        out_shape=jax.ShapeDtypeStruct((t, c), BF16),
        scratch_shapes=[pltpu.VMEM((halo + ts, c), F32), pltpu.VMEM((ts, c), F32),
                        pltpu.VMEM((SUBLANES, ts + halo, LANES), F32)],
        compiler_params=_params(2),
        name="conv_module",
    )(h, h, conv_w, conv_b.reshape(1, c), ln_g.reshape(1, c), ln_b.reshape(1, c))


def _layer(x, positions, norm1_g, w_in, q_norm_g, k_norm_g, w_o_attn, conv_w, conv_b,
           conv_ln_g, conv_ln_b, w_o_conv, w_out, norm2_g, w_ffn_in, ffn_conv_w,
           ffn_conv_b, w_ffn_out):
    batch, seq, d_model = x.shape
    t = batch * seq
    attn_width = w_o_attn.shape[0]
    conv_dim = w_o_conv.shape[0]
    d_ff = w_ffn_out.shape[0]
    n_heads = attn_width // HEAD_DIM
    assert seq % MOBA_BLOCK == 0
    assert w_in.shape[1] == 3 * attn_width + 2 * conv_dim + 2 * d_model

    x2 = x.reshape(t, d_model)
    xn, cos, sin = _rmsnorm(x2, norm1_g, positions)

    tm, tn = 1024, 512
    qk_g = jnp.stack([q_norm_g, k_norm_g])
    n_q_tiles = attn_width // tn
    qk, kmean = _ws_matmul(
        "qk_proj", [xn], [(w_in, 0)], [(0, 0)], 2 * n_q_tiles, tm, tn,
        functools.partial(_qk_epilogue, n_q_tiles=n_q_tiles, tm=tm, tn=tn), lag=1, a_ring=True,
        extras=(cos, sin, qk_g),
        extra_specs=(((tm, HEAD_DIM), lambda j, i: (i, 0)),
                     ((tm, HEAD_DIM), lambda j, i: (i, 0)),
                     ((2, HEAD_DIM), lambda j, i: (0, 0))),
        out_shapes=(jax.ShapeDtypeStruct((t, 2 * attn_width), BF16),
                    jax.ShapeDtypeStruct((t // tm, tm // MOBA_BLOCK, 2 * attn_width), F32)),
        out_specs=(_tile_spec(tm, tn),
                   ((1, tm // MOBA_BLOCK, tn), lambda j, i: (i, 0, j))))
    kmean = kmean.reshape(t // MOBA_BLOCK, 2 * attn_width)
    tmv, tnv = tm, tn
    (v,) = _ws_matmul(
        "v_proj", [xn], [(w_in, 2 * attn_width // tnv)], [(0, 0)], attn_width // tnv, tmv, tnv,
        _store_bf16_epilogue, a_ring=True,
        out_shapes=(jax.ShapeDtypeStruct((t, attn_width), BF16),),
        out_specs=(_tile_spec(tmv, tnv),))
    tg = 256
    u_off = 3 * attn_width
    (h,) = _ws_matmul(
        "glu_proj", [xn], [(w_in, u_off // tg), (w_in, (u_off + conv_dim) // tg)],
        [(0, 0), (0, 1)], conv_dim // tg, tm, tg, _glu_epilogue, a_ring=True,
        out_shapes=(jax.ShapeDtypeStruct((t, conv_dim), F32),),
        out_specs=(_tile_spec(tm, tg),))
    g_off = u_off + 2 * conv_dim
    (gates,) = _ws_matmul(
        "gate_proj", [xn], [(w_in, g_off // tn)], [(0, 0)], 2 * d_model // tn, tm, tn,
        _sigmoid_epilogue, lag=1, m_split=tm // MXU_ROW_BLOCK, a_ring=True,
        out_shapes=(jax.ShapeDtypeStruct((t, 2 * d_model), F32),),
        out_specs=(_tile_spec(tm, tn),))

    attn = _moba_attention(qk, v, kmean, batch, seq, n_heads)
    hc = _conv_module(h, conv_w, conv_b, conv_ln_g, conv_ln_b, batch, seq)

    n_model_tiles = d_model // tn
    tmo = 512
    tmm = 1024
    (merged,) = _ws_matmul(
        "merge_proj", [attn, hc], [(w_o_attn, 0), (w_o_conv, 0)], [(0, 0), (1, 1)],
        n_model_tiles, tmm, tn, _merge_epilogue,
        extras=(gates, gates),
        extra_specs=(_tile_spec(tmm, tn), _tile_spec(tmm, tn, n_model_tiles)),
        out_shapes=(jax.ShapeDtypeStruct((t, d_model), BF16),),
        out_specs=(_tile_spec(tmm, tn),))
    (x1,) = _ws_matmul(
        "out_proj", [merged], [(w_out, 0)], [(0, 0)], n_model_tiles, tm, tn,
        _residual_epilogue, a_ring=True,
        extras=(x2,), extra_specs=(_tile_spec(tm, tn),),
        out_shapes=(jax.ShapeDtypeStruct((t, d_model), F32),),
        out_specs=(_tile_spec(tm, tn),))

    xn2 = _rmsnorm(x1, norm2_g)
    tf = 256
    assert d_ff % tf == 0 and seq % tm == 0
    n_ff_tiles = d_ff // tf
    fcw = ffn_conv_w
    fcb = ffn_conv_b.reshape(1, 2 * d_ff)
    width = fcw.shape[0]
    n_ff_steps = n_ff_tiles * (t // tm)
    assert d_ff % (n_ff_steps * SUBLANES) == 0
    slab = ((d_ff // n_ff_steps, d_model), lambda j, i: (j * (t // tm) + i, 0))
    act, w_ffn_out_bf = _ws_matmul(
        "ffn_in", [xn2], [(w_ffn_in, 0), (w_ffn_in, n_ff_tiles)], [(0, 0), (0, 1)],
        n_ff_tiles, tm, tf,
        _with_side_cast(functools.partial(
            _ffn_in_epilogue, tm=tm, tiles_per_seq=seq // tm, width=width)),
        extras=(fcw, fcw, fcb, fcb, w_ffn_out),
        extra_specs=(((width, tf), lambda j, i: (0, j)),
                     ((width, tf), lambda j, i: (0, j + n_ff_tiles)),
                     ((1, tf), lambda j, i: (0, j)),
                     ((1, tf), lambda j, i: (0, j + n_ff_tiles)),
                     slab),
        out_shapes=(jax.ShapeDtypeStruct((t, d_ff), BF16),
                    jax.ShapeDtypeStruct((d_ff, d_model), BF16)),
        out_specs=(_tile_spec(tm, tf), slab),
        scratch=[pltpu.VMEM((tf // LANES, SUBLANES, LANES), F32)] * 2,
        pad=SUBLANES, lane_split=True, lag=1, m_split=tm // MXU_ROW_BLOCK, a_ring=True)

    (out,) = _ws_matmul(
        "ffn_out", [act], [(w_ffn_out_bf, 0)], [(0, 0)], n_model_tiles, tmo, tn,
        _residual_epilogue,
        extras=(x1,), extra_specs=(_tile_spec(tmo, tn),),
        out_shapes=(jax.ShapeDtypeStruct((t, d_model), F32),),
        out_specs=(_tile_spec(tmo, tn),))
    return out.reshape(batch, seq, d_model)


def kernel(x, positions, norm1_g, w_in, q_norm_g, k_norm_g, w_o_attn, conv_w, conv_b,
           conv_ln_g, conv_ln_b, w_o_conv, w_out, norm2_g, w_ffn_in, ffn_conv_w,
           ffn_conv_b, w_ffn_out):
    depth = norm1_g.shape[0]
    params = (norm1_g, w_in, q_norm_g, k_norm_g, w_o_attn, conv_w, conv_b, conv_ln_g,
              conv_ln_b, w_o_conv, w_out, norm2_g, w_ffn_in, ffn_conv_w, ffn_conv_b, w_ffn_out)
    for l in range(depth):
        layer = [p.reshape(p.shape[1:]) if depth == 1 else p[l] for p in params]
        x = _layer(x, positions, *layer)
    return x
```

```python
import functools

import jax
import jax.numpy as jnp
from jax import lax
from jax.experimental import pallas as pl
from jax.experimental.pallas import tpu as pltpu

HEAD_DIM = 128
MOBA_BLOCK = 256
MOBA_TOP_K = 3
ROPE_THETA = 10000.0
EPS = 1e-6

LANES = 128
SUBLANES = 8
VMEM_LIMIT_BYTES = 56 * 1024 * 1024
NORM_ROW_TILE = 512
FFN_ROW_CHUNK = 64
CONV_ROW_GROUP = 128
MXU_ROW_BLOCK = 128
A_RING_AHEAD = 2
ATTN_SCORES_AHEAD = 3
ATTN_OUTPUT_BEHIND = 2

F32 = jnp.float32
BF16 = jnp.bfloat16


def _params(n_grid_dims):
    return pltpu.CompilerParams(
        dimension_semantics=("arbitrary",) * n_grid_dims,
        vmem_limit_bytes=VMEM_LIMIT_BYTES)


def _rmsnorm_kernel(x_ref, g_ref, o_ref):
    x = x_ref[...]
    ms = jnp.mean(x * x, axis=-1, keepdims=True)
    o_ref[...] = (x * lax.rsqrt(ms + EPS) * g_ref[...]).astype(o_ref.dtype)


def _rmsnorm_rope_kernel(x_ref, g_ref, pos_ref, invf_ref, sign_ref, o_ref, cos_ref, sin_ref):
    _rmsnorm_kernel(x_ref, g_ref, o_ref)
    ang = pos_ref[...] * invf_ref[...]
    cos_ref[...] = jnp.cos(ang)
    sin_ref[...] = jnp.sin(ang) * sign_ref[...]


def _rmsnorm(x, g, positions=None, tr=NORM_ROW_TILE):
    t, d = x.shape
    row_spec = pl.BlockSpec((tr, d), lambda i: (i, 0))
    const = lambda cols: pl.BlockSpec((1, cols), lambda i: (0, 0))
    if positions is None:
        return pl.pallas_call(
            _rmsnorm_kernel,
            grid=(t // tr,),
            in_specs=[row_spec, const(d)],
            out_specs=row_spec,
            out_shape=jax.ShapeDtypeStruct((t, d), BF16),
            compiler_params=_params(1),
            name="rmsnorm",
        )(x, g.reshape(1, d))
    half = HEAD_DIM // 2
    inv = ROPE_THETA ** (-jnp.arange(half, dtype=F32) / half)
    invf = jnp.concatenate([inv, inv]).reshape(1, HEAD_DIM)
    sign = jnp.concatenate([-jnp.ones((half,), F32), jnp.ones((half,), F32)]).reshape(1, HEAD_DIM)
    pos = positions.astype(F32).reshape(t, 1)
    table_spec = pl.BlockSpec((tr, HEAD_DIM), lambda i: (i, 0))
    return pl.pallas_call(
        _rmsnorm_rope_kernel,
        grid=(t // tr,),
        in_specs=[row_spec, const(d), pl.BlockSpec((tr, 1), lambda i: (i, 0)),
                  const(HEAD_DIM), const(HEAD_DIM)],
        out_specs=[row_spec, table_spec, table_spec],
        out_shape=[jax.ShapeDtypeStruct((t, d), BF16),
                   jax.ShapeDtypeStruct((t, HEAD_DIM), F32),
                   jax.ShapeDtypeStruct((t, HEAD_DIM), F32)],
        compiler_params=_params(1),
        name="rmsnorm_rope",
    )(x, g.reshape(1, d), pos, invf, sign)


def _ws_kernel(*refs, n_a, n_w, dots, n_extra, n_out, epilogue, n_row_tiles, n_tiles, pad, lag,
               lane_split, m_split, a_ring):
    a_refs = refs[:n_a]
    w_refs = refs[n_a:n_a + n_w]
    extra_refs = refs[n_a + n_w:n_a + n_w + n_extra]
    out_refs = refs[n_a + n_w + n_extra:n_a + n_w + n_extra + n_out]
    scratch = refs[n_a + n_w + n_extra + n_out:]
    if a_ring:
        a_buf, a_sem = scratch[-2:]
        scratch = scratch[:-2]
    to_cast = [w_ref for w_ref in w_refs if w_ref.dtype != BF16]
    cast_refs = scratch[:len(to_cast)]
    n_slots = 1 + lag
    n_raw = n_slots * len(dots)
    raw_refs = scratch[len(to_cast):len(to_cast) + n_raw]
    user_scratch = scratch[len(to_cast) + n_raw:]
    s = pl.program_id(0)
    tile = jnp.minimum(s, n_tiles - 1)

    if a_ring:
        depth = a_buf.shape[0]
        tm_a = a_buf.shape[1]

        def a_copy(t):
            rows = pl.ds(pl.multiple_of((t % n_row_tiles) * tm_a, tm_a), tm_a)
            slot = t % depth
            return pltpu.make_async_copy(a_refs[0].at[rows, :], a_buf.at[slot], a_sem.at[slot])

        @pl.when(s == 0)
        def _prime():
            for t in range(min(A_RING_AHEAD, n_tiles)):
                a_copy(t).start()

        @pl.when(s + A_RING_AHEAD < n_tiles)
        def _prefetch():
            a_copy(s + A_RING_AHEAD).start()

        @pl.when(s < n_tiles)
        def _arrive():
            a_copy(s).wait()

        a_tiles = [a_buf.at[tile % depth]]
    else:
        a_tiles = a_refs

    if to_cast:
        @pl.when((tile % n_row_tiles == 0) & (s < n_tiles))
        def _cast_weights():
            for w_ref, wbf_ref in zip(to_cast, cast_refs):
                wbf_ref[...] = w_ref[...].astype(BF16)

    if lag or user_scratch:
        @pl.when(s == 0)
        def _init():
            if lag:
                for raw in raw_refs[1::2]:
                    raw[...] = jnp.zeros(raw.shape, F32)
            for u in user_scratch:
                u[...] = jnp.zeros(u.shape, u.dtype)

    cast_iter = iter(cast_refs)
    wbf_refs = [w_ref if w_ref.dtype == BF16 else next(cast_iter) for w_ref in w_refs]
    done = jnp.maximum(s - lag, 0)

    def step(slot):
        tm = a_tiles[0].shape[0]
        rb = tm // m_split
        for k in range(m_split):
            rows = (k * rb, (k + 1) * rb)
            for d, (ai, wi) in enumerate(dots):
                res = jnp.dot(a_tiles[ai][rows[0]:rows[1], :], wbf_refs[wi][...],
                              preferred_element_type=F32)
                raw = raw_refs[n_slots * d + slot]
                if lane_split:
                    for lt in range(raw.shape[0]):
                        raw[lt, pad + rows[0]:pad + rows[1], :] = res[:, lt * LANES:(lt + 1) * LANES]
                else:
                    raw[pad + rows[0]:pad + rows[1], :] = res
            epilogue(done // n_row_tiles, done % n_row_tiles,
                     [raw_refs[n_slots * d + (slot + lag) % n_slots] for d in range(len(dots))],
                     extra_refs, out_refs, user_scratch, rows)

    if lag:
        for slot in range(2):
            pl.when(s % 2 == slot)(functools.partial(step, slot))
    else:
        step(0)


def _ws_matmul(name, a_list, w_list, dots, n_col_tiles, tm, tn, epilogue,
               extras=(), extra_specs=(), out_shapes=(), out_specs=(), scratch=(), pad=0, lag=0,
               lane_split=False, m_split=1, a_ring=False):
    t = a_list[0].shape[0]
    n_row_tiles = t // tm
    n_tiles = n_col_tiles * n_row_tiles

    def dot_tile(s):
        return jnp.minimum(s, n_tiles - 1)

    def lagged(index_map):
        def wrapped(s):
            done = jnp.maximum(s - lag, 0)
            return index_map(done // n_row_tiles, done % n_row_tiles)
        return wrapped

    ring_scratch = []
    if a_ring:
        (a,) = a_list
        in_specs = [pl.BlockSpec(memory_space=pl.ANY)]
        ring_scratch = [pltpu.VMEM((A_RING_AHEAD + 1, tm, a.shape[1]), a.dtype),
                        pltpu.SemaphoreType.DMA((A_RING_AHEAD + 1,))]
    else:
        in_specs = [pl.BlockSpec((tm, a.shape[1]), lambda s: (dot_tile(s) % n_row_tiles, 0))
                    for a in a_list]
    for w, off in w_list:
        in_specs.append(pl.BlockSpec((w.shape[0], tn), functools.partial(
            lambda s, off: (0, dot_tile(s) // n_row_tiles + off), off=off)))
    in_specs.extend(pl.BlockSpec(shape, lagged(fn)) for shape, fn in extra_specs)
    wbf_scratch = [pltpu.VMEM((w.shape[0], tn), BF16) for w, _ in w_list if w.dtype != BF16]
    raw_shape = (tn // LANES, pad + tm, LANES) if lane_split else (pad + tm, tn)
    raw_scratch = [pltpu.VMEM(raw_shape, F32) for _ in range((1 + lag) * len(dots))]
    kernel = functools.partial(
        _ws_kernel, n_a=len(a_list), n_w=len(w_list), dots=tuple(dots),
        n_extra=len(extras), n_out=len(out_shapes), epilogue=epilogue,
        n_row_tiles=n_row_tiles, n_tiles=n_tiles, pad=pad, lag=lag, lane_split=lane_split,
        m_split=m_split, a_ring=a_ring)
    return pl.pallas_call(
        kernel,
        grid=(n_tiles + lag,),
        in_specs=in_specs,
        out_specs=[pl.BlockSpec(shape, lagged(fn)) for shape, fn in out_specs],
        out_shape=list(out_shapes),
        scratch_shapes=wbf_scratch + raw_scratch + list(scratch) + ring_scratch,
        compiler_params=_params(1),
        name=name,
    )(*a_list, *[w for w, _ in w_list], *extras)


def _tile_spec(tm, tn, col_offset=0):
    return ((tm, tn), lambda j, i: (i, j + col_offset))


def _qk_epilogue(j, i, acc_refs, extra_refs, out_refs, scratch, rows, *, n_q_tiles, tm, tn):
    assert rows == (0, tm)
    cos_ref, sin_ref, g_ref = extra_refs
    o_ref, mean_ref = out_refs
    acc_ref = acc_refs[0]
    is_q = j < n_q_tiles
    g = jnp.where(is_q, g_ref[0:1, :], g_ref[1:2, :])
    scale = jnp.where(is_q, jnp.float32(HEAD_DIM ** -0.5), jnp.float32(1.0))
    cos = cos_ref[...]
    sin = sin_ref[...]
    rows_per_tile = tm // MOBA_BLOCK
    for h in range(tn // HEAD_DIM):
        cols = slice(h * HEAD_DIM, (h + 1) * HEAD_DIM)
        x = acc_ref[:, cols]
        ms = jnp.mean(x * x, axis=-1, keepdims=True)
        y = x * lax.rsqrt(ms + EPS) * g
        y = y * cos + pltpu.roll(y, HEAD_DIM // 2, 1) * sin
        o_ref[:, cols] = (y * scale).astype(o_ref.dtype)
        for r in range(rows_per_tile):
            blk = y[r * MOBA_BLOCK:(r + 1) * MOBA_BLOCK]
            mean_ref[0, r:r + 1, cols] = jnp.mean(blk, axis=0, keepdims=True)


def _store_bf16_epilogue(j, i, acc_refs, extra_refs, out_refs, scratch, rows):
    r = slice(*rows)
    out_refs[0][r, :] = acc_refs[0][r, :].astype(out_refs[0].dtype)


def _glu_epilogue(j, i, acc_refs, extra_refs, out_refs, scratch, rows):
    r = slice(*rows)
    out_refs[0][r, :] = acc_refs[0][r, :] * jax.nn.sigmoid(acc_refs[1][r, :])


def _sigmoid_epilogue(j, i, acc_refs, extra_refs, out_refs, scratch, rows):
    r = slice(*rows)
    out_refs[0][r, :] = jax.nn.sigmoid(acc_refs[0][r, :])


def _with_side_cast(epilogue):
    def wrapped(j, i, acc_refs, extra_refs, out_refs, scratch, rows):
        if rows[0] == 0:
            out_refs[-1][...] = extra_refs[-1][...].astype(out_refs[-1].dtype)
        epilogue(j, i, acc_refs, extra_refs[:-1], out_refs[:-1], scratch, rows)
    return wrapped


def _merge_epilogue(j, i, acc_refs, extra_refs, out_refs, scratch, rows):
    ga_ref, gc_ref = extra_refs
    r = slice(*rows)
    merged = ga_ref[r, :] * acc_refs[0][r, :] + gc_ref[r, :] * acc_refs[1][r, :]
    out_refs[0][r, :] = merged.astype(out_refs[0].dtype)


def _residual_epilogue(j, i, acc_refs, extra_refs, out_refs, scratch, rows):
    r = slice(*rows)
    out_refs[0][r, :] = extra_refs[0][r, :] + acc_refs[0][r, :]


def _ffn_in_epilogue(j, i, acc_refs, extra_refs, out_refs, scratch, rows, *, tm, tiles_per_seq,
                     width):
    cwg_ref, cwu_ref, cbg_ref, cbu_ref = extra_refs
    pad = SUBLANES
    seq_start = i % tiles_per_seq == 0
    if rows[0] == 0:
        for hp, carry in zip(acc_refs, scratch):
            prev = carry[...]
            hp[:, 0:pad, :] = jnp.where(seq_start, jnp.zeros_like(prev), prev)
    for lt in range(acc_refs[0].shape[0]):
        lanes = slice(lt * LANES, (lt + 1) * LANES)
        for r in range(rows[0], rows[1], FFN_ROW_CHUNK):
            convs = []
            for hp, cw_ref, cb_ref in zip(acc_refs, (cwg_ref, cwu_ref), (cbg_ref, cbu_ref)):
                y = cb_ref[:, lanes]
                for w in range(width):
                    lo = pad + r - (width - 1 - w)
                    y = y + cw_ref[w:w + 1, lanes] * hp[lt, lo:lo + FFN_ROW_CHUNK, :]
                convs.append(y)
            gate, up = convs
            out_refs[0][r:r + FFN_ROW_CHUNK, lanes] = (
                jax.nn.silu(gate) * up).astype(out_refs[0].dtype)
    if rows[1] == tm:
        for hp, carry in zip(acc_refs, scratch):
            carry[...] = hp[:, tm:tm + pad, :]


def _attn_kernel(q_ref, k_ref, v_ref, km_ref, o_ref, *, n_blocks):
    L = MOBA_BLOCK
    nt = (((1,), (1,)), ((), ()))
    tn = (((0,), (0,)), ((), ()))
    km = km_ref[...].astype(BF16)
    neg = jnp.float32(-jnp.inf)
    key = lax.broadcasted_iota(jnp.int32, (L, L), 0)
    qry = lax.broadcasted_iota(jnp.int32, (L, L), 1)
    causal = key <= qry
    blk = lax.broadcasted_iota(jnp.int32, (n_blocks, L), 0)

    def scores(i):
        qi = q_ref[i * L:(i + 1) * L, :]
        n_keys = (i + 1) * L
        st = lax.dot_general(k_ref[0:n_keys, :], qi, nt, preferred_element_type=F32)
        gate = None
        if i > MOBA_TOP_K:
            gate = lax.dot_general(km, qi, nt, preferred_element_type=F32)
        return st, gate

    def probabilities(i, st, gate):
        pen = None
        if gate is not None:
            rank = jnp.zeros((n_blocks, L), F32)
            for jp in range(i):
                gb = jnp.broadcast_to(gate[jp:jp + 1, :], (n_blocks, L))
                beats = (gb > gate) | ((gb == gate) & (jp < blk))
                rank = rank + jnp.where(beats, 1.0, 0.0)
            pen = jnp.where(rank < MOBA_TOP_K, 0.0, neg)
        chunks = []
        for jb in range(i + 1):
            sj = st[jb * L:(jb + 1) * L, :]
            if jb == i:
                sj = jnp.where(causal, sj, neg)
            elif pen is not None:
                sj = sj + pen[jb:jb + 1, :]
            chunks.append(sj)
        top = chunks[0]
        for c in chunks[1:]:
            top = jnp.maximum(top, c)
        m = top.max(axis=0, keepdims=True)
        es = [jnp.exp(c - m) for c in chunks]
        tot = es[0]
        for e in es[1:]:
            tot = tot + e
        inv_l = 1.0 / tot.sum(axis=0, keepdims=True)
        return jnp.concatenate([(e * inv_l).astype(BF16) for e in es], axis=0)

    def output(i, pt):
        n_keys = (i + 1) * L
        ot = lax.dot_general(v_ref[0:n_keys, :], pt, tn, preferred_element_type=F32)
        o_ref[i * L:(i + 1) * L, :] = ot.T.astype(o_ref.dtype)

    st = {}
    pt = {}
    for i in range(min(ATTN_SCORES_AHEAD, n_blocks)):
        st[i] = scores(i)
    for i in range(n_blocks):
        pt[i] = probabilities(i, *st.pop(i))
        if i + ATTN_SCORES_AHEAD < n_blocks:
            st[i + ATTN_SCORES_AHEAD] = scores(i + ATTN_SCORES_AHEAD)
        if i >= ATTN_OUTPUT_BEHIND:
            output(i - ATTN_OUTPUT_BEHIND, pt.pop(i - ATTN_OUTPUT_BEHIND))
    for i in sorted(pt):
        output(i, pt[i])


def _moba_attention(qk, v, kmean, batch, seq, n_heads):
    t = qk.shape[0]
    n_blocks = seq // MOBA_BLOCK
    return pl.pallas_call(
        functools.partial(_attn_kernel, n_blocks=n_blocks),
        grid=(batch, n_heads),
        in_specs=[pl.BlockSpec((seq, HEAD_DIM), lambda b, h: (b, h)),
                  pl.BlockSpec((seq, HEAD_DIM), lambda b, h: (b, n_heads + h)),
                  pl.BlockSpec((seq, HEAD_DIM), lambda b, h: (b, h)),
                  pl.BlockSpec((n_blocks, HEAD_DIM), lambda b, h: (b, n_heads + h))],
        out_specs=pl.BlockSpec((seq, HEAD_DIM), lambda b, h: (b, h)),
        out_shape=jax.ShapeDtypeStruct((t, n_heads * HEAD_DIM), BF16),
        compiler_params=_params(2),
        name="moba_attention",
    )(qk, qk, v, kmean)


def _conv_kernel(h_ref, halo_ref, cw_ref, cb_ref, g_ref, b_ref, o_ref, xpad_ref, y_ref, z_ref,
                 *, ts, halo, width):
    t = pl.program_id(1)
    prev = halo_ref[...]
    xpad_ref[0:halo, :] = jnp.where(t == 0, jnp.zeros_like(prev), prev)
    xpad_ref[halo:, :] = h_ref[...]
    n_chunks = h_ref.shape[1] // LANES
    base = halo - (width - 1)

    def chunk(c, carry):
        cols = pl.ds(pl.multiple_of(c * LANES, LANES), LANES)
        groups = []
        for r in range(SUBLANES):
            taps = [w for w in range(width) if (base + w) % SUBLANES == r]
            if taps:
                span = taps[-1] - taps[0] + ts
                z_ref[r, 0:span, :] = xpad_ref[base + taps[0]:base + taps[0] + span, cols]
                groups.append((r, taps))
        for r0 in range(0, ts, CONV_ROW_GROUP):
            acc = jnp.broadcast_to(cb_ref[:, cols], (CONV_ROW_GROUP, LANES))
            for r, taps in groups:
                for w in taps:
                    off = r0 + w - taps[0]
                    acc = acc + cw_ref[w:w + 1, cols] * z_ref[r, off:off + CONV_ROW_GROUP, :]
            y_ref[r0:r0 + CONV_ROW_GROUP, cols] = acc
        return carry

    lax.fori_loop(0, n_chunks, chunk, 0)
    y = y_ref[...]
    mu = jnp.mean(y, axis=-1, keepdims=True)
    yc = y - mu
    var = jnp.mean(yc * yc, axis=-1, keepdims=True)
    z = yc * lax.rsqrt(var + EPS) * g_ref[...] + b_ref[...]
    o_ref[...] = jax.nn.silu(z).astype(o_ref.dtype)


def _conv_module(h, conv_w, conv_b, ln_g, ln_b, batch, seq, ts=256, halo=32):
    t, c = h.shape
    width = conv_w.shape[0]
    tiles = seq // ts
    halo_per_tile = ts // halo

    def halo_map(b, s):
        return (jnp.maximum((b * tiles + s) * halo_per_tile - 1, 0), 0)

    vec = lambda: pl.BlockSpec((1, c), lambda b, s: (0, 0))
    return pl.pallas_call(
        functools.partial(_conv_kernel, ts=ts, halo=halo, width=width),
        grid=(batch, tiles),
        in_specs=[pl.BlockSpec((ts, c), lambda b, s: (b * tiles + s, 0)),
                  pl.BlockSpec((halo, c), halo_map),
                  pl.BlockSpec((width, c), lambda b, s: (0, 0)),
                  vec(), vec(), vec()],
        out_specs=pl.BlockSpec((ts, c), lambda b, s: (b * tiles + s, 0)),
        out_shape=jax.ShapeDtypeStruct((t, c), BF16),
        scratch_shapes=[pltpu.VMEM((halo + ts, c), F32), pltpu.VMEM((ts, c), F32),
                        pltpu.VMEM((SUBLANES, ts + halo, LANES), F32)],
        compiler_params=_params(2),
        name="conv_module",
    )(h, h, conv_w, conv_b.reshape(1, c), ln_g.reshape(1, c), ln_b.reshape(1, c))


def _layer(x, positions, norm1_g, w_in, q_norm_g, k_norm_g, w_o_attn, conv_w, conv_b,
           conv_ln_g, conv_ln_b, w_o_conv, w_out, norm2_g, w_ffn_in, ffn_conv_w,
           ffn_conv_b, w_ffn_out):
    batch, seq, d_model = x.shape
    t = batch * seq
    attn_width = w_o_attn.shape[0]
    conv_dim = w_o_conv.shape[0]
    d_ff = w_ffn_out.shape[0]
    n_heads = attn_width // HEAD_DIM
    assert seq % MOBA_BLOCK == 0
    assert w_in.shape[1] == 3 * attn_width + 2 * conv_dim + 2 * d_model

    x2 = x.reshape(t, d_model)
    xn, cos, sin = _rmsnorm(x2, norm1_g, positions)

    tm, tn = 1024, 512
    qk_g = jnp.stack([q_norm_g, k_norm_g])
    n_q_tiles = attn_width // tn
    qk, kmean = _ws_matmul(
        "qk_proj", [xn], [(w_in, 0)], [(0, 0)], 2 * n_q_tiles, tm, tn,
        functools.partial(_qk_epilogue, n_q_tiles=n_q_tiles, tm=tm, tn=tn), lag=1, a_ring=True,
        extras=(cos, sin, qk_g),
        extra_specs=(((tm, HEAD_DIM), lambda j, i: (i, 0)),
                     ((tm, HEAD_DIM), lambda j, i: (i, 0)),
                     ((2, HEAD_DIM), lambda j, i: (0, 0))),
        out_shapes=(jax.ShapeDtypeStruct((t, 2 * attn_width), BF16),
                    jax.ShapeDtypeStruct((t // tm, tm // MOBA_BLOCK, 2 * attn_width), F32)),
        out_specs=(_tile_spec(tm, tn),
                   ((1, tm // MOBA_BLOCK, tn), lambda j, i: (i, 0, j))))
    kmean = kmean.reshape(t // MOBA_BLOCK, 2 * attn_width)
    tmv, tnv = tm, tn
    (v,) = _ws_matmul(
        "v_proj", [xn], [(w_in, 2 * attn_width // tnv)], [(0, 0)], attn_width // tnv, tmv, tnv,
        _store_bf16_epilogue, lag=1, m_split=tm // MXU_ROW_BLOCK, a_ring=True,
        out_shapes=(jax.ShapeDtypeStruct((t, attn_width), BF16),),
        out_specs=(_tile_spec(tmv, tnv),))
    tg = 256
    u_off = 3 * attn_width
    (h,) = _ws_matmul(
        "glu_proj", [xn], [(w_in, u_off // tg), (w_in, (u_off + conv_dim) // tg)],
        [(0, 0), (0, 1)], conv_dim // tg, tm, tg, _glu_epilogue,
        lag=1, m_split=tm // MXU_ROW_BLOCK, a_ring=True,
        out_shapes=(jax.ShapeDtypeStruct((t, conv_dim), F32),),
        out_specs=(_tile_spec(tm, tg),))
    g_off = u_off + 2 * conv_dim
    (gates,) = _ws_matmul(
        "gate_proj", [xn], [(w_in, g_off // tn)], [(0, 0)], 2 * d_model // tn, tm, tn,
        _sigmoid_epilogue, lag=1, m_split=tm // MXU_ROW_BLOCK, a_ring=True,
        out_shapes=(jax.ShapeDtypeStruct((t, 2 * d_model), F32),),
        out_specs=(_tile_spec(tm, tn),))

    attn = _moba_attention(qk, v, kmean, batch, seq, n_heads)
    hc = _conv_module(h, conv_w, conv_b, conv_ln_g, conv_ln_b, batch, seq)

    n_model_tiles = d_model // tn
    tmo = 512
    tmm = 1024
    (merged,) = _ws_matmul(
        "merge_proj", [attn, hc], [(w_o_attn, 0), (w_o_conv, 0)], [(0, 0), (1, 1)],
        n_model_tiles, tmm, tn, _merge_epilogue,
        extras=(gates, gates),
        extra_specs=(_tile_spec(tmm, tn), _tile_spec(tmm, tn, n_model_tiles)),
        out_shapes=(jax.ShapeDtypeStruct((t, d_model), BF16),),
        out_specs=(_tile_spec(tmm, tn),))
    (x1,) = _ws_matmul(
        "out_proj", [merged], [(w_out, 0)], [(0, 0)], n_model_tiles, tm, tn,
        _residual_epilogue, lag=1, m_split=tm // MXU_ROW_BLOCK, a_ring=True,
        extras=(x2,), extra_specs=(_tile_spec(tm, tn),),
        out_shapes=(jax.ShapeDtypeStruct((t, d_model), F32),),
        out_specs=(_tile_spec(tm, tn),))

    xn2 = _rmsnorm(x1, norm2_g)
    tf = 256
    assert d_ff % tf == 0 and seq % tm == 0
    n_ff_tiles = d_ff // tf
    fcw = ffn_conv_w
    fcb = ffn_conv_b.reshape(1, 2 * d_ff)
    width = fcw.shape[0]
    n_ff_steps = n_ff_tiles * (t // tm)
    assert d_ff % (n_ff_steps * SUBLANES) == 0
    slab = ((d_ff // n_ff_steps, d_model), lambda j, i: (j * (t // tm) + i, 0))
    act, w_ffn_out_bf = _ws_matmul(
        "ffn_in", [xn2], [(w_ffn_in, 0), (w_ffn_in, n_ff_tiles)], [(0, 0), (0, 1)],
        n_ff_tiles, tm, tf,
        _with_side_cast(functools.partial(
            _ffn_in_epilogue, tm=tm, tiles_per_seq=seq // tm, width=width)),
        extras=(fcw, fcw, fcb, fcb, w_ffn_out),
        extra_specs=(((width, tf), lambda j, i: (0, j)),
                     ((width, tf), lambda j, i: (0, j + n_ff_tiles)),
                     ((1, tf), lambda j, i: (0, j)),
                     ((1, tf), lambda j, i: (0, j + n_ff_tiles)),
                     slab),
        out_shapes=(jax.ShapeDtypeStruct((t, d_ff), BF16),
                    jax.ShapeDtypeStruct((d_ff, d_model), BF16)),
        out_specs=(_tile_spec(tm, tf), slab),
        scratch=[pltpu.VMEM((tf // LANES, SUBLANES, LANES), F32)] * 2,
        pad=SUBLANES, lane_split=True, lag=1, m_split=tm // (2 * MXU_ROW_BLOCK), a_ring=True)

    (out,) = _ws_matmul(
        "ffn_out", [act], [(w_ffn_out_bf, 0)], [(0, 0)], n_model_tiles, tmo, tn,
        _residual_epilogue,
        extras=(x1,), extra_specs=(_tile_spec(tmo, tn),),
        out_shapes=(jax.ShapeDtypeStruct((t, d_model), F32),),
        out_specs=(_tile_spec(tmo, tn),))
    return out.reshape(batch, seq, d_model)


def kernel(x, positions, norm1_g, w_in, q_norm_g, k_norm_g, w_o_attn, conv_w, conv_b,
           conv_ln_g, conv_ln_b, w_o_conv, w_out, norm2_g, w_ffn_in, ffn_conv_w,
           ffn_conv_b, w_ffn_out):
    depth = norm1_g.shape[0]
    params = (norm1_g, w_in, q_norm_g, k_norm_g, w_o_attn, conv_w, conv_b, conv_ln_g,
              conv_ln_b, w_o_conv, w_out, norm2_g, w_ffn_in, ffn_conv_w, ffn_conv_b, w_ffn_out)
    for l in range(depth):
        layer = [p.reshape(p.shape[1:]) if depth == 1 else p[l] for p in params]
        x = _layer(x, positions, *layer)
    return x
```

```python
import functools

import jax
import jax.numpy as jnp
from jax import lax
from jax.experimental import pallas as pl
from jax.experimental.pallas import tpu as pltpu

HEAD_DIM = 128
MOBA_BLOCK = 256
MOBA_TOP_K = 3
ROPE_THETA = 10000.0
EPS = 1e-6

LANES = 128
SUBLANES = 8
VMEM_LIMIT_BYTES = 56 * 1024 * 1024
ROW_TILE = 1024
COL_TILE = 512
PAIR_COL_TILE = 256
FFN_OUT_ROW_TILE = 512
NORM_ROW_TILE = 512
FFN_ROW_CHUNK = 64
CONV_ROW_GROUP = 128
MXU_ROW_BLOCK = 128
A_RING_AHEAD = 2
ATTN_SCORES_AHEAD = 3
ATTN_OUTPUT_BEHIND = 2

F32 = jnp.float32
BF16 = jnp.bfloat16


def _params(n_grid_dims):
    return pltpu.CompilerParams(
        dimension_semantics=("arbitrary",) * n_grid_dims,
        vmem_limit_bytes=VMEM_LIMIT_BYTES)


def _rmsnorm_kernel(x_ref, g_ref, o_ref):
    x = x_ref[...]
    ms = jnp.mean(x * x, axis=-1, keepdims=True)
    o_ref[...] = (x * lax.rsqrt(ms + EPS) * g_ref[...]).astype(o_ref.dtype)


def _rmsnorm_rope_kernel(x_ref, g_ref, pos_ref, invf_ref, sign_ref, o_ref, cos_ref, sin_ref):
    _rmsnorm_kernel(x_ref, g_ref, o_ref)
    ang = pos_ref[...] * invf_ref[...]
    cos_ref[...] = jnp.cos(ang)
    sin_ref[...] = jnp.sin(ang) * sign_ref[...]


def _rmsnorm(x, g, positions=None, tr=NORM_ROW_TILE):
    t, d = x.shape
    row_spec = pl.BlockSpec((tr, d), lambda i: (i, 0))
    const = lambda cols: pl.BlockSpec((1, cols), lambda i: (0, 0))
    if positions is None:
        return pl.pallas_call(
            _rmsnorm_kernel,
            grid=(t // tr,),
            in_specs=[row_spec, const(d)],
            out_specs=row_spec,
            out_shape=jax.ShapeDtypeStruct((t, d), BF16),
            compiler_params=_params(1),
            name="rmsnorm",
        )(x, g.reshape(1, d))
    half = HEAD_DIM // 2
    inv = ROPE_THETA ** (-jnp.arange(half, dtype=F32) / half)
    invf = jnp.concatenate([inv, inv]).reshape(1, HEAD_DIM)
    sign = jnp.concatenate([-jnp.ones((half,), F32), jnp.ones((half,), F32)]).reshape(1, HEAD_DIM)
    pos = positions.astype(F32).reshape(t, 1)
    table_spec = pl.BlockSpec((tr, HEAD_DIM), lambda i: (i, 0))
    return pl.pallas_call(
        _rmsnorm_rope_kernel,
        grid=(t // tr,),
        in_specs=[row_spec, const(d), pl.BlockSpec((tr, 1), lambda i: (i, 0)),
                  const(HEAD_DIM), const(HEAD_DIM)],
        out_specs=[row_spec, table_spec, table_spec],
        out_shape=[jax.ShapeDtypeStruct((t, d), BF16),
                   jax.ShapeDtypeStruct((t, HEAD_DIM), F32),
                   jax.ShapeDtypeStruct((t, HEAD_DIM), F32)],
        compiler_params=_params(1),
        name="rmsnorm_rope",
    )(x, g.reshape(1, d), pos, invf, sign)


def _ws_kernel(*refs, n_a, n_w, dots, n_extra, n_out, epilogue, n_row_tiles, n_tiles, pad, lag,
               lane_split, m_split, a_ring):
    a_refs = refs[:n_a]
    w_refs = refs[n_a:n_a + n_w]
    extra_refs = refs[n_a + n_w:n_a + n_w + n_extra]
    out_refs = refs[n_a + n_w + n_extra:n_a + n_w + n_extra + n_out]
    scratch = refs[n_a + n_w + n_extra + n_out:]
    if a_ring:
        a_bufs = scratch[-2 * n_a::2]
        a_sems = scratch[-2 * n_a + 1::2]
        scratch = scratch[:-2 * n_a]
    to_cast = [w_ref for w_ref in w_refs if w_ref.dtype != BF16]
    cast_refs = scratch[:len(to_cast)]
    n_slots = 1 + lag
    n_raw = n_slots * len(dots)
    raw_refs = scratch[len(to_cast):len(to_cast) + n_raw]
    user_scratch = scratch[len(to_cast) + n_raw:]
    s = pl.program_id(0)
    tile = jnp.minimum(s, n_tiles - 1)

    if a_ring:
        depth = a_bufs[0].shape[0]
        tm_a = a_bufs[0].shape[1]

        def a_copies(t):
            rows = pl.ds(pl.multiple_of((t % n_row_tiles) * tm_a, tm_a), tm_a)
            slot = t % depth
            return [pltpu.make_async_copy(a_ref.at[rows, :], a_buf.at[slot], a_sem.at[slot])
                    for a_ref, a_buf, a_sem in zip(a_refs, a_bufs, a_sems)]

        @pl.when(s == 0)
        def _prime():
            for t in range(min(A_RING_AHEAD, n_tiles)):
                for copy in a_copies(t):
                    copy.start()

        @pl.when(s + A_RING_AHEAD < n_tiles)
        def _prefetch():
            for copy in a_copies(s + A_RING_AHEAD):
                copy.start()

        @pl.when(s < n_tiles)
        def _arrive():
            for copy in a_copies(s):
                copy.wait()

        a_tiles = [a_buf.at[tile % depth] for a_buf in a_bufs]
    else:
        a_tiles = a_refs

    if to_cast:
        @pl.when((tile % n_row_tiles == 0) & (s < n_tiles))
        def _cast_weights():
            for w_ref, wbf_ref in zip(to_cast, cast_refs):
                wbf_ref[...] = w_ref[...].astype(BF16)

    if lag or user_scratch:
        @pl.when(s == 0)
        def _init():
            if lag:
                for raw in raw_refs[1::2]:
                    raw[...] = jnp.zeros(raw.shape, F32)
            for u in user_scratch:
                u[...] = jnp.zeros(u.shape, u.dtype)

    cast_iter = iter(cast_refs)
    wbf_refs = [w_ref if w_ref.dtype == BF16 else next(cast_iter) for w_ref in w_refs]
    done = jnp.maximum(s - lag, 0)

    def step(slot):
        tm = a_tiles[0].shape[0]
        rb = tm // m_split
        for k in range(m_split):
            rows = (k * rb, (k + 1) * rb)
            for d, (ai, wi) in enumerate(dots):
                res = jnp.dot(a_tiles[ai][rows[0]:rows[1], :], wbf_refs[wi][...],
                              preferred_element_type=F32)
                raw = raw_refs[n_slots * d + slot]
                if lane_split:
                    for lt in range(raw.shape[0]):
                        raw[lt, pad + rows[0]:pad + rows[1], :] = res[:, lt * LANES:(lt + 1) * LANES]
                else:
                    raw[pad + rows[0]:pad + rows[1], :] = res
            epilogue(done // n_row_tiles, done % n_row_tiles,
                     [raw_refs[n_slots * d + (slot + lag) % n_slots] for d in range(len(dots))],
                     extra_refs, out_refs, user_scratch, rows)

    if lag:
        for slot in range(2):
            pl.when(s % 2 == slot)(functools.partial(step, slot))
    else:
        step(0)


def _ws_matmul(name, a_list, w_list, dots, n_col_tiles, tm, tn, epilogue,
               extras=(), extra_specs=(), out_shapes=(), out_specs=(), scratch=(), pad=0, lag=0,
               lane_split=False, m_split=1, a_ring=False):
    t = a_list[0].shape[0]
    n_row_tiles = t // tm
    n_tiles = n_col_tiles * n_row_tiles

    def dot_tile(s):
        return jnp.minimum(s, n_tiles - 1)

    def lagged(index_map):
        def wrapped(s):
            done = jnp.maximum(s - lag, 0)
            return index_map(done // n_row_tiles, done % n_row_tiles)
        return wrapped

    ring_scratch = []
    if a_ring:
        in_specs = [pl.BlockSpec(memory_space=pl.ANY) for _ in a_list]
        for a in a_list:
            ring_scratch += [pltpu.VMEM((A_RING_AHEAD + 1, tm, a.shape[1]), a.dtype),
                             pltpu.SemaphoreType.DMA((A_RING_AHEAD + 1,))]
    else:
        in_specs = [pl.BlockSpec((tm, a.shape[1]), lambda s: (dot_tile(s) % n_row_tiles, 0))
                    for a in a_list]
    for w, off in w_list:
        in_specs.append(pl.BlockSpec((w.shape[0], tn), functools.partial(
            lambda s, off: (0, dot_tile(s) // n_row_tiles + off), off=off)))
    in_specs.extend(pl.BlockSpec(shape, lagged(fn)) for shape, fn in extra_specs)
    wbf_scratch = [pltpu.VMEM((w.shape[0], tn), BF16) for w, _ in w_list if w.dtype != BF16]
    raw_shape = (tn // LANES, pad + tm, LANES) if lane_split else (pad + tm, tn)
    raw_scratch = [pltpu.VMEM(raw_shape, F32) for _ in range((1 + lag) * len(dots))]
    kernel = functools.partial(
        _ws_kernel, n_a=len(a_list), n_w=len(w_list), dots=tuple(dots),
        n_extra=len(extras), n_out=len(out_shapes), epilogue=epilogue,
        n_row_tiles=n_row_tiles, n_tiles=n_tiles, pad=pad, lag=lag, lane_split=lane_split,
        m_split=m_split, a_ring=a_ring)
    return pl.pallas_call(
        kernel,
        grid=(n_tiles + lag,),
        in_specs=in_specs,
        out_specs=[pl.BlockSpec(shape, lagged(fn)) for shape, fn in out_specs],
        out_shape=list(out_shapes),
        scratch_shapes=wbf_scratch + raw_scratch + list(scratch) + ring_scratch,
        compiler_params=_params(1),
        name=name,
    )(*a_list, *[w for w, _ in w_list], *extras)


def _tile_spec(tm, tn, col_offset=0):
    return ((tm, tn), lambda j, i: (i, j + col_offset))


def _qk_epilogue(j, i, acc_refs, extra_refs, out_refs, scratch, rows, *, n_q_tiles, tm, tn):
    assert rows == (0, tm)
    cos_ref, sin_ref, g_ref = extra_refs
    o_ref, mean_ref = out_refs
    acc_ref = acc_refs[0]
    is_q = j < n_q_tiles
    g = jnp.where(is_q, g_ref[0:1, :], g_ref[1:2, :])
    scale = jnp.where(is_q, jnp.float32(HEAD_DIM ** -0.5), jnp.float32(1.0))
    cos = cos_ref[...]
    sin = sin_ref[...]
    rows_per_tile = tm // MOBA_BLOCK
    for h in range(tn // HEAD_DIM):
        cols = slice(h * HEAD_DIM, (h + 1) * HEAD_DIM)
        x = acc_ref[:, cols]
        ms = jnp.mean(x * x, axis=-1, keepdims=True)
        y = x * lax.rsqrt(ms + EPS) * g
        y = y * cos + pltpu.roll(y, HEAD_DIM // 2, 1) * sin
        o_ref[:, cols] = (y * scale).astype(o_ref.dtype)
        for r in range(rows_per_tile):
            blk = y[r * MOBA_BLOCK:(r + 1) * MOBA_BLOCK]
            mean_ref[0, r:r + 1, cols] = jnp.mean(blk, axis=0, keepdims=True)


def _store_bf16_epilogue(j, i, acc_refs, extra_refs, out_refs, scratch, rows):
    r = slice(*rows)
    out_refs[0][r, :] = acc_refs[0][r, :].astype(out_refs[0].dtype)


def _glu_epilogue(j, i, acc_refs, extra_refs, out_refs, scratch, rows):
    r = slice(*rows)
    out_refs[0][r, :] = acc_refs[0][r, :] * jax.nn.sigmoid(acc_refs[1][r, :])


def _sigmoid_epilogue(j, i, acc_refs, extra_refs, out_refs, scratch, rows):
    r = slice(*rows)
    out_refs[0][r, :] = jax.nn.sigmoid(acc_refs[0][r, :])


def _with_side_cast(epilogue):
    def wrapped(j, i, acc_refs, extra_refs, out_refs, scratch, rows):
        if rows[0] == 0:
            out_refs[-1][...] = extra_refs[-1][...].astype(out_refs[-1].dtype)
        epilogue(j, i, acc_refs, extra_refs[:-1], out_refs[:-1], scratch, rows)
    return wrapped


def _merge_epilogue(j, i, acc_refs, extra_refs, out_refs, scratch, rows):
    ga_ref, gc_ref = extra_refs
    r = slice(*rows)
    merged = ga_ref[r, :] * acc_refs[0][r, :] + gc_ref[r, :] * acc_refs[1][r, :]
    out_refs[0][r, :] = merged.astype(out_refs[0].dtype)


def _residual_epilogue(j, i, acc_refs, extra_refs, out_refs, scratch, rows):
    r = slice(*rows)
    out_refs[0][r, :] = extra_refs[0][r, :] + acc_refs[0][r, :]


def _ffn_in_epilogue(j, i, acc_refs, extra_refs, out_refs, scratch, rows, *, tm, tiles_per_seq,
                     width):
    cwg_ref, cwu_ref, cbg_ref, cbu_ref = extra_refs
    pad = SUBLANES
    seq_start = i % tiles_per_seq == 0
    if rows[0] == 0:
        for hp, carry in zip(acc_refs, scratch):
            prev = carry[...]
            hp[:, 0:pad, :] = jnp.where(seq_start, jnp.zeros_like(prev), prev)
    for lt in range(acc_refs[0].shape[0]):
        lanes = slice(lt * LANES, (lt + 1) * LANES)
        for r in range(rows[0], rows[1], FFN_ROW_CHUNK):
            convs = []
            for hp, cw_ref, cb_ref in zip(acc_refs, (cwg_ref, cwu_ref), (cbg_ref, cbu_ref)):
                y = cb_ref[:, lanes]
                for w in range(width):
                    lo = pad + r - (width - 1 - w)
                    y = y + cw_ref[w:w + 1, lanes] * hp[lt, lo:lo + FFN_ROW_CHUNK, :]
                convs.append(y)
            gate, up = convs
            out_refs[0][r:r + FFN_ROW_CHUNK, lanes] = (
                jax.nn.silu(gate) * up).astype(out_refs[0].dtype)
    if rows[1] == tm:
        for hp, carry in zip(acc_refs, scratch):
            carry[...] = hp[:, tm:tm + pad, :]


def _attn_kernel(q_ref, k_ref, v_ref, km_ref, o_ref, *, n_blocks):
    L = MOBA_BLOCK
    nt = (((1,), (1,)), ((), ()))
    tn = (((0,), (0,)), ((), ()))
    km = km_ref[...].astype(BF16)
    neg = jnp.float32(-jnp.inf)
    key = lax.broadcasted_iota(jnp.int32, (L, L), 0)
    qry = lax.broadcasted_iota(jnp.int32, (L, L), 1)
    causal = key <= qry
    blk = lax.broadcasted_iota(jnp.int32, (n_blocks, L), 0)

    def scores(i):
        qi = q_ref[i * L:(i + 1) * L, :]
        n_keys = (i + 1) * L
        st = lax.dot_general(k_ref[0:n_keys, :], qi, nt, preferred_element_type=F32)
        gate = None
        if i > MOBA_TOP_K:
            gate = lax.dot_general(km, qi, nt, preferred_element_type=F32)
        return st, gate

    def probabilities(i, st, gate):
        pen = None
        if gate is not None:
            rank = jnp.zeros((n_blocks, L), F32)
            for jp in range(i):
                gb = jnp.broadcast_to(gate[jp:jp + 1, :], (n_blocks, L))
                beats = (gb > gate) | ((gb == gate) & (jp < blk))
                rank = rank + jnp.where(beats, 1.0, 0.0)
            pen = jnp.where(rank < MOBA_TOP_K, 0.0, neg)
        chunks = []
        for jb in range(i + 1):
            sj = st[jb * L:(jb + 1) * L, :]
            if jb == i:
                sj = jnp.where(causal, sj, neg)
            elif pen is not None:
                sj = sj + pen[jb:jb + 1, :]
            chunks.append(sj)
        top = chunks[0]
        for c in chunks[1:]:
            top = jnp.maximum(top, c)
        m = top.max(axis=0, keepdims=True)
        es = [jnp.exp(c - m) for c in chunks]
        tot = es[0]
        for e in es[1:]:
            tot = tot + e
        inv_l = 1.0 / tot.sum(axis=0, keepdims=True)
        return jnp.concatenate([(e * inv_l).astype(BF16) for e in es], axis=0)

    def output(i, pt):
        n_keys = (i + 1) * L
        ot = lax.dot_general(v_ref[0:n_keys, :], pt, tn, preferred_element_type=F32)
        o_ref[i * L:(i + 1) * L, :] = ot.T.astype(o_ref.dtype)

    st = {}
    pt = {}
    for i in range(min(ATTN_SCORES_AHEAD, n_blocks)):
        st[i] = scores(i)
    for i in range(n_blocks):
        pt[i] = probabilities(i, *st.pop(i))
        if i + ATTN_SCORES_AHEAD < n_blocks:
            st[i + ATTN_SCORES_AHEAD] = scores(i + ATTN_SCORES_AHEAD)
        if i >= ATTN_OUTPUT_BEHIND:
            output(i - ATTN_OUTPUT_BEHIND, pt.pop(i - ATTN_OUTPUT_BEHIND))
    for i in sorted(pt):
        output(i, pt[i])


def _moba_attention(qk, v, kmean, batch, seq, n_heads):
    t = qk.shape[0]
    n_blocks = seq // MOBA_BLOCK
    return pl.pallas_call(
        functools.partial(_attn_kernel, n_blocks=n_blocks),
        grid=(batch, n_heads),
        in_specs=[pl.BlockSpec((seq, HEAD_DIM), lambda b, h: (b, h)),
                  pl.BlockSpec((seq, HEAD_DIM), lambda b, h: (b, n_heads + h)),
                  pl.BlockSpec((seq, HEAD_DIM), lambda b, h: (b, h)),
                  pl.BlockSpec((n_blocks, HEAD_DIM), lambda b, h: (b, n_heads + h))],
        out_specs=pl.BlockSpec((seq, HEAD_DIM), lambda b, h: (b, h)),
        out_shape=jax.ShapeDtypeStruct((t, n_heads * HEAD_DIM), BF16),
        compiler_params=_params(2),
        name="moba_attention",
    )(qk, qk, v, kmean)


def _conv_kernel(h_ref, halo_ref, cw_ref, cb_ref, g_ref, b_ref, o_ref, xpad_ref, y_ref, z_ref,
                 *, ts, halo, width):
    t = pl.program_id(1)
    prev = halo_ref[...]
    xpad_ref[0:halo, :] = jnp.where(t == 0, jnp.zeros_like(prev), prev)
    xpad_ref[halo:, :] = h_ref[...]
    n_chunks = h_ref.shape[1] // LANES
    base = halo - (width - 1)

    def chunk(c, carry):
        cols = pl.ds(pl.multiple_of(c * LANES, LANES), LANES)
        groups = []
        for r in range(SUBLANES):
            taps = [w for w in range(width) if (base + w) % SUBLANES == r]
            if taps:
                span = taps[-1] - taps[0] + ts
                z_ref[r, 0:span, :] = xpad_ref[base + taps[0]:base + taps[0] + span, cols]
                groups.append((r, taps))
        for r0 in range(0, ts, CONV_ROW_GROUP):
            acc = jnp.broadcast_to(cb_ref[:, cols], (CONV_ROW_GROUP, LANES))
            for r, taps in groups:
                for w in taps:
                    off = r0 + w - taps[0]
                    acc = acc + cw_ref[w:w + 1, cols] * z_ref[r, off:off + CONV_ROW_GROUP, :]
            y_ref[r0:r0 + CONV_ROW_GROUP, cols] = acc
        return carry

    lax.fori_loop(0, n_chunks, chunk, 0)
    y = y_ref[...]
    mu = jnp.mean(y, axis=-1, keepdims=True)
    yc = y - mu
    var = jnp.mean(yc * yc, axis=-1, keepdims=True)
    z = yc * lax.rsqrt(var + EPS) * g_ref[...] + b_ref[...]
    o_ref[...] = jax.nn.silu(z).astype(o_ref.dtype)


def _conv_module(h, conv_w, conv_b, ln_g, ln_b, batch, seq, ts=256, halo=32):
    t, c = h.shape
    width = conv_w.shape[0]
    tiles = seq // ts
    halo_per_tile = ts // halo

    def halo_map(b, s):
        return (jnp.maximum((b * tiles + s) * halo_per_tile - 1, 0), 0)

    vec = lambda: pl.BlockSpec((1, c), lambda b, s: (0, 0))
    return pl.pallas_call(
        functools.partial(_conv_kernel, ts=ts, halo=halo, width=width),
        grid=(batch, tiles),
        in_specs=[pl.BlockSpec((ts, c), lambda b, s: (b * tiles + s, 0)),
                  pl.BlockSpec((halo, c), halo_map),
                  pl.BlockSpec((width, c), lambda b, s: (0, 0)),
                  vec(), vec(), vec()],
        out_specs=pl.BlockSpec((ts, c), lambda b, s: (b * tiles + s, 0)),
        out_shape=jax.ShapeDtypeStruct((t, c), BF16),
        scratch_shapes=[pltpu.VMEM((halo + ts, c), F32), pltpu.VMEM((ts, c), F32),
                        pltpu.VMEM((SUBLANES, ts + halo, LANES), F32)],
        compiler_params=_params(2),
        name="conv_module",
    )(h, h, conv_w, conv_b.reshape(1, c), ln_g.reshape(1, c), ln_b.reshape(1, c))


def _layer(x, positions, norm1_g, w_in, q_norm_g, k_norm_g, w_o_attn, conv_w, conv_b,
           conv_ln_g, conv_ln_b, w_o_conv, w_out, norm2_g, w_ffn_in, ffn_conv_w,
           ffn_conv_b, w_ffn_out):
    batch, seq, d_model = x.shape
    t = batch * seq
    attn_width = w_o_attn.shape[0]
    conv_dim = w_o_conv.shape[0]
    d_ff = w_ffn_out.shape[0]
    n_heads = attn_width // HEAD_DIM
    assert seq % MOBA_BLOCK == 0
    assert w_in.shape[1] == 3 * attn_width + 2 * conv_dim + 2 * d_model

    x2 = x.reshape(t, d_model)
    xn, cos, sin = _rmsnorm(x2, norm1_g, positions)

    tm, tn = ROW_TILE, COL_TILE
    qk_g = jnp.stack([q_norm_g, k_norm_g])
    n_q_tiles = attn_width // tn
    qk, kmean = _ws_matmul(
        "qk_proj", [xn], [(w_in, 0)], [(0, 0)], 2 * n_q_tiles, tm, tn,
        functools.partial(_qk_epilogue, n_q_tiles=n_q_tiles, tm=tm, tn=tn), lag=1, a_ring=True,
        extras=(cos, sin, qk_g),
        extra_specs=(((tm, HEAD_DIM), lambda j, i: (i, 0)),
                     ((tm, HEAD_DIM), lambda j, i: (i, 0)),
                     ((2, HEAD_DIM), lambda j, i: (0, 0))),
        out_shapes=(jax.ShapeDtypeStruct((t, 2 * attn_width), BF16),
                    jax.ShapeDtypeStruct((t // tm, tm // MOBA_BLOCK, 2 * attn_width), F32)),
        out_specs=(_tile_spec(tm, tn),
                   ((1, tm // MOBA_BLOCK, tn), lambda j, i: (i, 0, j))))
    kmean = kmean.reshape(t // MOBA_BLOCK, 2 * attn_width)
    (v,) = _ws_matmul(
        "v_proj", [xn], [(w_in, 2 * attn_width // tn)], [(0, 0)], attn_width // tn, tm, tn,
        _store_bf16_epilogue, a_ring=True,
        out_shapes=(jax.ShapeDtypeStruct((t, attn_width), BF16),),
        out_specs=(_tile_spec(tm, tn),))
    tg = PAIR_COL_TILE
    u_off = 3 * attn_width
    (h,) = _ws_matmul(
        "glu_proj", [xn], [(w_in, u_off // tg), (w_in, (u_off + conv_dim) // tg)],
        [(0, 0), (0, 1)], conv_dim // tg, tm, tg, _glu_epilogue, a_ring=True,
        out_shapes=(jax.ShapeDtypeStruct((t, conv_dim), F32),),
        out_specs=(_tile_spec(tm, tg),))
    g_off = u_off + 2 * conv_dim
    (gates,) = _ws_matmul(
        "gate_proj", [xn], [(w_in, g_off // tn)], [(0, 0)], 2 * d_model // tn, tm, tn,
        _sigmoid_epilogue, lag=1, m_split=tm // MXU_ROW_BLOCK, a_ring=True,
        out_shapes=(jax.ShapeDtypeStruct((t, 2 * d_model), F32),),
        out_specs=(_tile_spec(tm, tn),))

    attn = _moba_attention(qk, v, kmean, batch, seq, n_heads)
    hc = _conv_module(h, conv_w, conv_b, conv_ln_g, conv_ln_b, batch, seq)

    n_model_tiles = d_model // tn
    tmo = FFN_OUT_ROW_TILE
    (merged,) = _ws_matmul(
        "merge_proj", [attn, hc], [(w_o_attn, 0), (w_o_conv, 0)], [(0, 0), (1, 1)],
        n_model_tiles, tm, tn, _merge_epilogue, a_ring=True,
        extras=(gates, gates),
        extra_specs=(_tile_spec(tm, tn), _tile_spec(tm, tn, n_model_tiles)),
        out_shapes=(jax.ShapeDtypeStruct((t, d_model), BF16),),
        out_specs=(_tile_spec(tm, tn),))
    (x1,) = _ws_matmul(
        "out_proj", [merged], [(w_out, 0)], [(0, 0)], n_model_tiles, tm, tn,
        _residual_epilogue, a_ring=True,
        extras=(x2,), extra_specs=(_tile_spec(tm, tn),),
        out_shapes=(jax.ShapeDtypeStruct((t, d_model), F32),),
        out_specs=(_tile_spec(tm, tn),))

    xn2 = _rmsnorm(x1, norm2_g)
    tf = PAIR_COL_TILE
    assert d_ff % tf == 0 and seq % tm == 0
    n_ff_tiles = d_ff // tf
    fcw = ffn_conv_w
    fcb = ffn_conv_b.reshape(1, 2 * d_ff)
    width = fcw.shape[0]
    n_ff_steps = n_ff_tiles * (t // tm)
    assert d_ff % (n_ff_steps * SUBLANES) == 0
    slab = ((d_ff // n_ff_steps, d_model), lambda j, i: (j * (t // tm) + i, 0))
    act, w_ffn_out_bf = _ws_matmul(
        "ffn_in", [xn2], [(w_ffn_in, 0), (w_ffn_in, n_ff_tiles)], [(0, 0), (0, 1)],
        n_ff_tiles, tm, tf,
        _with_side_cast(functools.partial(
            _ffn_in_epilogue, tm=tm, tiles_per_seq=seq // tm, width=width)),
        extras=(fcw, fcw, fcb, fcb, w_ffn_out),
        extra_specs=(((width, tf), lambda j, i: (0, j)),
                     ((width, tf), lambda j, i: (0, j + n_ff_tiles)),
                     ((1, tf), lambda j, i: (0, j)),
                     ((1, tf), lambda j, i: (0, j + n_ff_tiles)),
                     slab),
        out_shapes=(jax.ShapeDtypeStruct((t, d_ff), BF16),
                    jax.ShapeDtypeStruct((d_ff, d_model), BF16)),
        out_specs=(_tile_spec(tm, tf), slab),
        scratch=[pltpu.VMEM((tf // LANES, SUBLANES, LANES), F32)] * 2,
        pad=SUBLANES, lane_split=True, lag=1, m_split=tm // MXU_ROW_BLOCK, a_ring=True)

    (out,) = _ws_matmul(
        "ffn_out", [act], [(w_ffn_out_bf, 0)], [(0, 0)], n_model_tiles, tmo, tn,
        _residual_epilogue,
        extras=(x1,), extra_specs=(_tile_spec(tmo, tn),),
        out_shapes=(jax.ShapeDtypeStruct((t, d_model), F32),),
        out_specs=(_tile_spec(tmo, tn),))
    return out.reshape(batch, seq, d_model)


def kernel(x, positions, norm1_g, w_in, q_norm_g, k_norm_g, w_o_attn, conv_w, conv_b,
           conv_ln_g, conv_ln_b, w_o_conv, w_out, norm2_g, w_ffn_in, ffn_conv_w,
           ffn_conv_b, w_ffn_out):
    depth = norm1_g.shape[0]
    params = (norm1_g, w_in, q_norm_g, k_norm_g, w_o_attn, conv_w, conv_b, conv_ln_g,
              conv_ln_b, w_o_conv, w_out, norm2_g, w_ffn_in, ffn_conv_w, ffn_conv_b, w_ffn_out)
    for l in range(depth):
        layer = [p.reshape(p.shape[1:]) if depth == 1 else p[l] for p in params]
        x = _layer(x, positions, *layer)
    return x
```

```python
import functools

import jax
import jax.numpy as jnp
from jax import lax
from jax.experimental import pallas as pl
from jax.experimental.pallas import tpu as pltpu

HEAD_DIM = 128
MOBA_BLOCK = 256
MOBA_TOP_K = 3
ROPE_THETA = 10000.0
EPS = 1e-6

LANES = 128
SUBLANES = 8
VMEM_LIMIT_BYTES = 56 * 1024 * 1024
ROW_TILE = 1024
COL_TILE = 512
PAIR_COL_TILE = 256
FFN_OUT_ROW_TILE = 512
NORM_ROW_TILE = 512
FFN_ROW_CHUNK = 64
CONV_ROW_GROUP = 128
MXU_ROW_BLOCK = 128
A_RING_AHEAD = 2
ATTN_SCORES_AHEAD = 3
ATTN_OUTPUT_BEHIND = 2

F32 = jnp.float32
BF16 = jnp.bfloat16


def _params(n_grid_dims):
    return pltpu.CompilerParams(
        dimension_semantics=("arbitrary",) * n_grid_dims,
        vmem_limit_bytes=VMEM_LIMIT_BYTES)


def _rmsnorm_kernel(x_ref, g_ref, o_ref):
    x = x_ref[...]
    ms = jnp.mean(x * x, axis=-1, keepdims=True)
    o_ref[...] = (x * lax.rsqrt(ms + EPS) * g_ref[...]).astype(o_ref.dtype)


def _rmsnorm_rope_kernel(x_ref, g_ref, pos_ref, invf_ref, sign_ref, o_ref, cos_ref, sin_ref):
    _rmsnorm_kernel(x_ref, g_ref, o_ref)
    ang = pos_ref[...] * invf_ref[...]
    cos_ref[...] = jnp.cos(ang)
    sin_ref[...] = jnp.sin(ang) * sign_ref[...]


def _rmsnorm(x, g, positions=None, tr=NORM_ROW_TILE):
    t, d = x.shape
    row_spec = pl.BlockSpec((tr, d), lambda i: (i, 0))
    const = lambda cols: pl.BlockSpec((1, cols), lambda i: (0, 0))
    if positions is None:
        return pl.pallas_call(
            _rmsnorm_kernel,
            grid=(t // tr,),
            in_specs=[row_spec, const(d)],
            out_specs=row_spec,
            out_shape=jax.ShapeDtypeStruct((t, d), BF16),
            compiler_params=_params(1),
            name="rmsnorm",
        )(x, g.reshape(1, d))
    half = HEAD_DIM // 2
    inv = ROPE_THETA ** (-jnp.arange(half, dtype=F32) / half)
    invf = jnp.concatenate([inv, inv]).reshape(1, HEAD_DIM)
    sign = jnp.concatenate([-jnp.ones((half,), F32), jnp.ones((half,), F32)]).reshape(1, HEAD_DIM)
    pos = positions.astype(F32).reshape(t, 1)
    table_spec = pl.BlockSpec((tr, HEAD_DIM), lambda i: (i, 0))
    return pl.pallas_call(
        _rmsnorm_rope_kernel,
        grid=(t // tr,),
        in_specs=[row_spec, const(d), pl.BlockSpec((tr, 1), lambda i: (i, 0)),
                  const(HEAD_DIM), const(HEAD_DIM)],
        out_specs=[row_spec, table_spec, table_spec],
        out_shape=[jax.ShapeDtypeStruct((t, d), BF16),
                   jax.ShapeDtypeStruct((t, HEAD_DIM), F32),
                   jax.ShapeDtypeStruct((t, HEAD_DIM), F32)],
        compiler_params=_params(1),
        name="rmsnorm_rope",
    )(x, g.reshape(1, d), pos, invf, sign)


def _ws_kernel(*refs, n_a, n_w, dots, n_extra, n_out, epilogue, n_row_tiles, n_tiles, pad, lag,
               lane_split, m_split, a_ring, w_offsets):
    a_refs = refs[:n_a]
    w_refs = refs[n_a:n_a + n_w]
    extra_refs = refs[n_a + n_w:n_a + n_w + n_extra]
    out_refs = refs[n_a + n_w + n_extra:n_a + n_w + n_extra + n_out]
    scratch = refs[n_a + n_w + n_extra + n_out:]
    if a_ring:
        a_bufs = scratch[-2 * n_a::2]
        a_sems = scratch[-2 * n_a + 1::2]
        scratch = scratch[:-2 * n_a]
    w_prefetch = w_offsets is not None
    if w_prefetch:
        w_bufs = scratch[-2 * n_w::2]
        w_sems = scratch[-2 * n_w + 1::2]
        scratch = scratch[:-2 * n_w]
    to_cast = [w_ref for w_ref in w_refs if w_ref.dtype != BF16]
    cast_refs = scratch[:len(to_cast)]
    n_slots = 1 + lag
    n_raw = n_slots * len(dots)
    raw_refs = scratch[len(to_cast):len(to_cast) + n_raw]
    user_scratch = scratch[len(to_cast) + n_raw:]
    s = pl.program_id(0)
    tile = jnp.minimum(s, n_tiles - 1)

    if a_ring:
        depth = a_bufs[0].shape[0]
        tm_a = a_bufs[0].shape[1]

        def a_copies(t):
            rows = pl.ds(pl.multiple_of((t % n_row_tiles) * tm_a, tm_a), tm_a)
            slot = t % depth
            return [pltpu.make_async_copy(a_ref.at[rows, :], a_buf.at[slot], a_sem.at[slot])
                    for a_ref, a_buf, a_sem in zip(a_refs, a_bufs, a_sems)]

        @pl.when(s == 0)
        def _prime():
            for t in range(min(A_RING_AHEAD, n_tiles)):
                for copy in a_copies(t):
                    copy.start()

        @pl.when(s + A_RING_AHEAD < n_tiles)
        def _prefetch():
            for copy in a_copies(s + A_RING_AHEAD):
                copy.start()

        @pl.when(s < n_tiles)
        def _arrive():
            for copy in a_copies(s):
                copy.wait()

        a_tiles = [a_buf.at[tile % depth] for a_buf in a_bufs]
    else:
        a_tiles = a_refs

    if w_prefetch:
        col = tile // n_row_tiles
        n_col_tiles = n_tiles // n_row_tiles

        def w_copies(j):
            copies = []
            for w_ref, w_buf, w_sem, off in zip(w_refs, w_bufs, w_sems, w_offsets):
                tn_w = w_buf.shape[1]
                cols = pl.ds(pl.multiple_of((j + off) * tn_w, tn_w), tn_w)
                copies.append(pltpu.make_async_copy(w_ref.at[:, cols], w_buf, w_sem.at[0]))
            return copies

        @pl.when(s == 0)
        def _first_weights():
            for copy in w_copies(0):
                copy.start()

        @pl.when((tile % n_row_tiles == 0) & (s < n_tiles))
        def _cast_weights():
            for copy in w_copies(col):
                copy.wait()
            for w_buf, wbf_ref in zip(w_bufs, cast_refs):
                wbf_ref[...] = w_buf[...].astype(BF16)

            @pl.when(col + 1 < n_col_tiles)
            def _next_weights():
                for copy in w_copies(col + 1):
                    copy.start()
    elif to_cast:
        @pl.when((tile % n_row_tiles == 0) & (s < n_tiles))
        def _cast_weights():
            for w_ref, wbf_ref in zip(to_cast, cast_refs):
                wbf_ref[...] = w_ref[...].astype(BF16)

    if lag or user_scratch:
        @pl.when(s == 0)
        def _init():
            if lag:
                for raw in raw_refs[1::2]:
                    raw[...] = jnp.zeros(raw.shape, F32)
            for u in user_scratch:
                u[...] = jnp.zeros(u.shape, u.dtype)

    cast_iter = iter(cast_refs)
    wbf_refs = [w_ref if w_ref.dtype == BF16 else next(cast_iter) for w_ref in w_refs]
    done = jnp.maximum(s - lag, 0)

    def step(slot):
        tm = a_tiles[0].shape[0]
        rb = tm // m_split
        for k in range(m_split):
            rows = (k * rb, (k + 1) * rb)
            for d, (ai, wi) in enumerate(dots):
                res = jnp.dot(a_tiles[ai][rows[0]:rows[1], :], wbf_refs[wi][...],
                              preferred_element_type=F32)
                raw = raw_refs[n_slots * d + slot]
                if lane_split:
                    for lt in range(raw.shape[0]):
                        raw[lt, pad + rows[0]:pad + rows[1], :] = res[:, lt * LANES:(lt + 1) * LANES]
                else:
                    raw[pad + rows[0]:pad + rows[1], :] = res
            epilogue(done // n_row_tiles, done % n_row_tiles,
                     [raw_refs[n_slots * d + (slot + lag) % n_slots] for d in range(len(dots))],
                     extra_refs, out_refs, user_scratch, rows)

    if lag:
        for slot in range(2):
            pl.when(s % 2 == slot)(functools.partial(step, slot))
    else:
        step(0)


def _ws_matmul(name, a_list, w_list, dots, n_col_tiles, tm, tn, epilogue,
               extras=(), extra_specs=(), out_shapes=(), out_specs=(), scratch=(), pad=0, lag=0,
               lane_split=False, m_split=1, a_ring=False):
    t = a_list[0].shape[0]
    n_row_tiles = t // tm
    n_tiles = n_col_tiles * n_row_tiles

    def dot_tile(s):
        return jnp.minimum(s, n_tiles - 1)

    def lagged(index_map):
        def wrapped(s):
            done = jnp.maximum(s - lag, 0)
            return index_map(done // n_row_tiles, done % n_row_tiles)
        return wrapped

    ring_scratch = []
    if a_ring:
        in_specs = [pl.BlockSpec(memory_space=pl.ANY) for _ in a_list]
        for a in a_list:
            ring_scratch += [pltpu.VMEM((A_RING_AHEAD + 1, tm, a.shape[1]), a.dtype),
                             pltpu.SemaphoreType.DMA((A_RING_AHEAD + 1,))]
    else:
        in_specs = [pl.BlockSpec((tm, a.shape[1]), lambda s: (dot_tile(s) % n_row_tiles, 0))
                    for a in a_list]
    w_prefetch = all(w.dtype != BF16 for w, _ in w_list)
    w_scratch = []
    for w, off in w_list:
        if w_prefetch:
            in_specs.append(pl.BlockSpec(memory_space=pl.ANY))
            w_scratch += [pltpu.VMEM((w.shape[0], tn), w.dtype), pltpu.SemaphoreType.DMA((1,))]
        else:
            in_specs.append(pl.BlockSpec((w.shape[0], tn), functools.partial(
                lambda s, off: (0, dot_tile(s) // n_row_tiles + off), off=off)))
    in_specs.extend(pl.BlockSpec(shape, lagged(fn)) for shape, fn in extra_specs)
    wbf_scratch = [pltpu.VMEM((w.shape[0], tn), BF16) for w, _ in w_list if w.dtype != BF16]
    raw_shape = (tn // LANES, pad + tm, LANES) if lane_split else (pad + tm, tn)
    raw_scratch = [pltpu.VMEM(raw_shape, F32) for _ in range((1 + lag) * len(dots))]
    kernel = functools.partial(
        _ws_kernel, n_a=len(a_list), n_w=len(w_list), dots=tuple(dots),
        n_extra=len(extras), n_out=len(out_shapes), epilogue=epilogue,
        n_row_tiles=n_row_tiles, n_tiles=n_tiles, pad=pad, lag=lag, lane_split=lane_split,
        m_split=m_split, a_ring=a_ring,
        w_offsets=tuple(off for _, off in w_list) if w_prefetch else None)
    return pl.pallas_call(
        kernel,
        grid=(n_tiles + lag,),
        in_specs=in_specs,
        out_specs=[pl.BlockSpec(shape, lagged(fn)) for shape, fn in out_specs],
        out_shape=list(out_shapes),
        scratch_shapes=wbf_scratch + raw_scratch + list(scratch) + w_scratch + ring_scratch,
        compiler_params=_params(1),
        name=name,
    )(*a_list, *[w for w, _ in w_list], *extras)


def _tile_spec(tm, tn, col_offset=0):
    return ((tm, tn), lambda j, i: (i, j + col_offset))


def _qk_epilogue(j, i, acc_refs, extra_refs, out_refs, scratch, rows, *, n_q_tiles, tm, tn):
    assert rows == (0, tm)
    cos_ref, sin_ref, g_ref = extra_refs
    o_ref, mean_ref = out_refs
    acc_ref = acc_refs[0]
    is_q = j < n_q_tiles
    g = jnp.where(is_q, g_ref[0:1, :], g_ref[1:2, :])
    scale = jnp.where(is_q, jnp.float32(HEAD_DIM ** -0.5), jnp.float32(1.0))
    cos = cos_ref[...]
    sin = sin_ref[...]
    rows_per_tile = tm // MOBA_BLOCK
    for h in range(tn // HEAD_DIM):
        cols = slice(h * HEAD_DIM, (h + 1) * HEAD_DIM)
        x = acc_ref[:, cols]
        ms = jnp.mean(x * x, axis=-1, keepdims=True)
        y = x * lax.rsqrt(ms + EPS) * g
        y = y * cos + pltpu.roll(y, HEAD_DIM // 2, 1) * sin
        o_ref[:, cols] = (y * scale).astype(o_ref.dtype)
        for r in range(rows_per_tile):
            blk = y[r * MOBA_BLOCK:(r + 1) * MOBA_BLOCK]
            mean_ref[0, r:r + 1, cols] = jnp.mean(blk, axis=0, keepdims=True)


def _store_bf16_epilogue(j, i, acc_refs, extra_refs, out_refs, scratch, rows):
    r = slice(*rows)
    out_refs[0][r, :] = acc_refs[0][r, :].astype(out_refs[0].dtype)


def _glu_epilogue(j, i, acc_refs, extra_refs, out_refs, scratch, rows):
    r = slice(*rows)
    out_refs[0][r, :] = acc_refs[0][r, :] * jax.nn.sigmoid(acc_refs[1][r, :])


def _sigmoid_epilogue(j, i, acc_refs, extra_refs, out_refs, scratch, rows):
    r = slice(*rows)
    out_refs[0][r, :] = jax.nn.sigmoid(acc_refs[0][r, :])


def _with_side_cast(epilogue):
    def wrapped(j, i, acc_refs, extra_refs, out_refs, scratch, rows):
        if rows[0] == 0:
            out_refs[-1][...] = extra_refs[-1][...].astype(out_refs[-1].dtype)
        epilogue(j, i, acc_refs, extra_refs[:-1], out_refs[:-1], scratch, rows)
    return wrapped


def _merge_epilogue(j, i, acc_refs, extra_refs, out_refs, scratch, rows):
    ga_ref, gc_ref = extra_refs
    r = slice(*rows)
    merged = ga_ref[r, :] * acc_refs[0][r, :] + gc_ref[r, :] * acc_refs[1][r, :]
    out_refs[0][r, :] = merged.astype(out_refs[0].dtype)


def _residual_epilogue(j, i, acc_refs, extra_refs, out_refs, scratch, rows):
    r = slice(*rows)
    out_refs[0][r, :] = extra_refs[0][r, :] + acc_refs[0][r, :]


def _ffn_in_epilogue(j, i, acc_refs, extra_refs, out_refs, scratch, rows, *, tm, tiles_per_seq,
                     width):
    cwg_ref, cwu_ref, cbg_ref, cbu_ref = extra_refs
    pad = SUBLANES
    seq_start = i % tiles_per_seq == 0
    if rows[0] == 0:
        for hp, carry in zip(acc_refs, scratch):
            prev = carry[...]
            hp[:, 0:pad, :] = jnp.where(seq_start, jnp.zeros_like(prev), prev)
    for lt in range(acc_refs[0].shape[0]):
        lanes = slice(lt * LANES, (lt + 1) * LANES)
        for r in range(rows[0], rows[1], FFN_ROW_CHUNK):
            convs = []
            for hp, cw_ref, cb_ref in zip(acc_refs, (cwg_ref, cwu_ref), (cbg_ref, cbu_ref)):
                y = cb_ref[:, lanes]
                for w in range(width):
                    lo = pad + r - (width - 1 - w)
                    y = y + cw_ref[w:w + 1, lanes] * hp[lt, lo:lo + FFN_ROW_CHUNK, :]
                convs.append(y)
            gate, up = convs
            out_refs[0][r:r + FFN_ROW_CHUNK, lanes] = (
                jax.nn.silu(gate) * up).astype(out_refs[0].dtype)
    if rows[1] == tm:
        for hp, carry in zip(acc_refs, scratch):
            carry[...] = hp[:, tm:tm + pad, :]


def _attn_kernel(q_ref, k_ref, v_ref, km_ref, o_ref, *, n_blocks):
    L = MOBA_BLOCK
    nt = (((1,), (1,)), ((), ()))
    tn = (((0,), (0,)), ((), ()))
    km = km_ref[...].astype(BF16)
    neg = jnp.float32(-jnp.inf)
    key = lax.broadcasted_iota(jnp.int32, (L, L), 0)
    qry = lax.broadcasted_iota(jnp.int32, (L, L), 1)
    causal = key <= qry
    blk = lax.broadcasted_iota(jnp.int32, (n_blocks, L), 0)

    def scores(i):
        qi = q_ref[i * L:(i + 1) * L, :]
        n_keys = (i + 1) * L
        st = lax.dot_general(k_ref[0:n_keys, :], qi, nt, preferred_element_type=F32)
        gate = None
        if i > MOBA_TOP_K:
            gate = lax.dot_general(km, qi, nt, preferred_element_type=F32)
        return st, gate

    def probabilities(i, st, gate):
        pen = None
        if gate is not None:
            rank = jnp.zeros((n_blocks, L), F32)
            for jp in range(i):
                gb = jnp.broadcast_to(gate[jp:jp + 1, :], (n_blocks, L))
                beats = (gb > gate) | ((gb == gate) & (jp < blk))
                rank = rank + jnp.where(beats, 1.0, 0.0)
            pen = jnp.where(rank < MOBA_TOP_K, 0.0, neg)
        chunks = []
        for jb in range(i + 1):
            sj = st[jb * L:(jb + 1) * L, :]
            if jb == i:
                sj = jnp.where(causal, sj, neg)
            elif pen is not None:
                sj = sj + pen[jb:jb + 1, :]
            chunks.append(sj)
        top = chunks[0]
        for c in chunks[1:]:
            top = jnp.maximum(top, c)
        m = top.max(axis=0, keepdims=True)
        es = [jnp.exp(c - m) for c in chunks]
        tot = es[0]
        for e in es[1:]:
            tot = tot + e
        inv_l = 1.0 / tot.sum(axis=0, keepdims=True)
        return jnp.concatenate([(e * inv_l).astype(BF16) for e in es], axis=0)

    def output(i, pt):
        n_keys = (i + 1) * L
        ot = lax.dot_general(v_ref[0:n_keys, :], pt, tn, preferred_element_type=F32)
        o_ref[i * L:(i + 1) * L, :] = ot.T.astype(o_ref.dtype)

    st = {}
    pt = {}
    for i in range(min(ATTN_SCORES_AHEAD, n_blocks)):
        st[i] = scores(i)
    for i in range(n_blocks):
        pt[i] = probabilities(i, *st.pop(i))
        if i + ATTN_SCORES_AHEAD < n_blocks:
            st[i + ATTN_SCORES_AHEAD] = scores(i + ATTN_SCORES_AHEAD)
        if i >= ATTN_OUTPUT_BEHIND:
            output(i - ATTN_OUTPUT_BEHIND, pt.pop(i - ATTN_OUTPUT_BEHIND))
    for i in sorted(pt):
        output(i, pt[i])


def _moba_attention(qk, v, kmean, batch, seq, n_heads):
    t = qk.shape[0]
    n_blocks = seq // MOBA_BLOCK
    return pl.pallas_call(
        functools.partial(_attn_kernel, n_blocks=n_blocks),
        grid=(batch, n_heads),
        in_specs=[pl.BlockSpec((seq, HEAD_DIM), lambda b, h: (b, h)),
                  pl.BlockSpec((seq, HEAD_DIM), lambda b, h: (b, n_heads + h)),
                  pl.BlockSpec((seq, HEAD_DIM), lambda b, h: (b, h)),
                  pl.BlockSpec((n_blocks, HEAD_DIM), lambda b, h: (b, n_heads + h))],
        out_specs=pl.BlockSpec((seq, HEAD_DIM), lambda b, h: (b, h)),
        out_shape=jax.ShapeDtypeStruct((t, n_heads * HEAD_DIM), BF16),
        compiler_params=_params(2),
        name="moba_attention",
    )(qk, qk, v, kmean)


def _conv_kernel(h_ref, halo_ref, cw_ref, cb_ref, g_ref, b_ref, o_ref, xpad_ref, y_ref, z_ref,
                 *, ts, halo, width):
    t = pl.program_id(1)
    prev = halo_ref[...]
    xpad_ref[0:halo, :] = jnp.where(t == 0, jnp.zeros_like(prev), prev)
    xpad_ref[halo:, :] = h_ref[...]
    n_chunks = h_ref.shape[1] // LANES
    base = halo - (width - 1)

    def chunk(c, carry):
        cols = pl.ds(pl.multiple_of(c * LANES, LANES), LANES)
        groups = []
        for r in range(SUBLANES):
            taps = [w for w in range(width) if (base + w) % SUBLANES == r]
            if taps:
                span = taps[-1] - taps[0] + ts
                z_ref[r, 0:span, :] = xpad_ref[base + taps[0]:base + taps[0] + span, cols]
                groups.append((r, taps))
        for r0 in range(0, ts, CONV_ROW_GROUP):
            acc = jnp.broadcast_to(cb_ref[:, cols], (CONV_ROW_GROUP, LANES))
            for r, taps in groups:
                for w in taps:
                    off = r0 + w - taps[0]
                    acc = acc + cw_ref[w:w + 1, cols] * z_ref[r, off:off + CONV_ROW_GROUP, :]
            y_ref[r0:r0 + CONV_ROW_GROUP, cols] = acc
        return carry

    lax.fori_loop(0, n_chunks, chunk, 0)
    y = y_ref[...]
    mu = jnp.mean(y, axis=-1, keepdims=True)
    yc = y - mu
    var = jnp.mean(yc * yc, axis=-1, keepdims=True)
    z = yc * lax.rsqrt(var + EPS) * g_ref[...] + b_ref[...]
    o_ref[...] = jax.nn.silu(z).astype(o_ref.dtype)


def _conv_module(h, conv_w, conv_b, ln_g, ln_b, batch, seq, ts=256, halo=32):
    t, c = h.shape
    width = conv_w.shape[0]
    tiles = seq // ts
    halo_per_tile = ts // halo

    def halo_map(b, s):
        return (jnp.maximum((b * tiles + s) * halo_per_tile - 1, 0), 0)

    vec = lambda: pl.BlockSpec((1, c), lambda b, s: (0, 0))
    return pl.pallas_call(
        functools.partial(_conv_kernel, ts=ts, halo=halo, width=width),
        grid=(batch, tiles),
        in_specs=[pl.BlockSpec((ts, c), lambda b, s: (b * tiles + s, 0)),
                  pl.BlockSpec((halo, c), halo_map),
                  pl.BlockSpec((width, c), lambda b, s: (0, 0)),
                  vec(), vec(), vec()],
        out_specs=pl.BlockSpec((ts, c), lambda b, s: (b * tiles + s, 0)),
        out_shape=jax.ShapeDtypeStruct((t, c), BF16),
        scratch_shapes=[pltpu.VMEM((halo + ts, c), F32), pltpu.VMEM((ts, c), F32),
                        pltpu.VMEM((SUBLANES, ts + halo, LANES), F32)],
        compiler_params=_params(2),
        name="conv_module",
    )(h, h, conv_w, conv_b.reshape(1, c), ln_g.reshape(1, c), ln_b.reshape(1, c))


def _layer(x, positions, norm1_g, w_in, q_norm_g, k_norm_g, w_o_attn, conv_w, conv_b,
           conv_ln_g, conv_ln_b, w_o_conv, w_out, norm2_g, w_ffn_in, ffn_conv_w,
           ffn_conv_b, w_ffn_out):
    batch, seq, d_model = x.shape
    t = batch * seq
    attn_width = w_o_attn.shape[0]
    conv_dim = w_o_conv.shape[0]
    d_ff = w_ffn_out.shape[0]
    n_heads = attn_width // HEAD_DIM
    assert seq % MOBA_BLOCK == 0
    assert w_in.shape[1] == 3 * attn_width + 2 * conv_dim + 2 * d_model

    x2 = x.reshape(t, d_model)
    xn, cos, sin = _rmsnorm(x2, norm1_g, positions)

    tm, tn = ROW_TILE, COL_TILE
    qk_g = jnp.stack([q_norm_g, k_norm_g])
    n_q_tiles = attn_width // tn
    qk, kmean = _ws_matmul(
        "qk_proj", [xn], [(w_in, 0)], [(0, 0)], 2 * n_q_tiles, tm, tn,
        functools.partial(_qk_epilogue, n_q_tiles=n_q_tiles, tm=tm, tn=tn), lag=1, a_ring=True,
        extras=(cos, sin, qk_g),
        extra_specs=(((tm, HEAD_DIM), lambda j, i: (i, 0)),
                     ((tm, HEAD_DIM), lambda j, i: (i, 0)),
                     ((2, HEAD_DIM), lambda j, i: (0, 0))),
        out_shapes=(jax.ShapeDtypeStruct((t, 2 * attn_width), BF16),
                    jax.ShapeDtypeStruct((t // tm, tm // MOBA_BLOCK, 2 * attn_width), F32)),
        out_specs=(_tile_spec(tm, tn),
                   ((1, tm // MOBA_BLOCK, tn), lambda j, i: (i, 0, j))))
    kmean = kmean.reshape(t // MOBA_BLOCK, 2 * attn_width)
    (v,) = _ws_matmul(
        "v_proj", [xn], [(w_in, 2 * attn_width // tn)], [(0, 0)], attn_width // tn, tm, tn,
        _store_bf16_epilogue, a_ring=True,
        out_shapes=(jax.ShapeDtypeStruct((t, attn_width), BF16),),
        out_specs=(_tile_spec(tm, tn),))
    tg = PAIR_COL_TILE
    u_off = 3 * attn_width
    (h,) = _ws_matmul(
        "glu_proj", [xn], [(w_in, u_off // tg), (w_in, (u_off + conv_dim) // tg)],
        [(0, 0), (0, 1)], conv_dim // tg, tm, tg, _glu_epilogue, a_ring=True,
        out_shapes=(jax.ShapeDtypeStruct((t, conv_dim), F32),),
        out_specs=(_tile_spec(tm, tg),))
    g_off = u_off + 2 * conv_dim
    (gates,) = _ws_matmul(
        "gate_proj", [xn], [(w_in, g_off // tn)], [(0, 0)], 2 * d_model // tn, tm, tn,
        _sigmoid_epilogue, lag=1, m_split=tm // MXU_ROW_BLOCK, a_ring=True,
        out_shapes=(jax.ShapeDtypeStruct((t, 2 * d_model), F32),),
        out_specs=(_tile_spec(tm, tn),))

    attn = _moba_attention(qk, v, kmean, batch, seq, n_heads)
    hc = _conv_module(h, conv_w, conv_b, conv_ln_g, conv_ln_b, batch, seq)

    n_model_tiles = d_model // tn
    tmo = FFN_OUT_ROW_TILE
    (merged,) = _ws_matmul(
        "merge_proj", [attn, hc], [(w_o_attn, 0), (w_o_conv, 0)], [(0, 0), (1, 1)],
        n_model_tiles, tm, tn, _merge_epilogue, a_ring=True,
        extras=(gates, gates),
        extra_specs=(_tile_spec(tm, tn), _tile_spec(tm, tn, n_model_tiles)),
        out_shapes=(jax.ShapeDtypeStruct((t, d_model), BF16),),
        out_specs=(_tile_spec(tm, tn),))
    (x1,) = _ws_matmul(
        "out_proj", [merged], [(w_out, 0)], [(0, 0)], n_model_tiles, tm, tn,
        _residual_epilogue, a_ring=True,
        extras=(x2,), extra_specs=(_tile_spec(tm, tn),),
        out_shapes=(jax.ShapeDtypeStruct((t, d_model), F32),),
        out_specs=(_tile_spec(tm, tn),))

    xn2 = _rmsnorm(x1, norm2_g)
    tf = PAIR_COL_TILE
    assert d_ff % tf == 0 and seq % tm == 0
    n_ff_tiles = d_ff // tf
    fcw = ffn_conv_w
    fcb = ffn_conv_b.reshape(1, 2 * d_ff)
    width = fcw.shape[0]
    n_ff_steps = n_ff_tiles * (t // tm)
    assert d_ff % (n_ff_steps * SUBLANES) == 0
    slab = ((d_ff // n_ff_steps, d_model), lambda j, i: (j * (t // tm) + i, 0))
    act, w_ffn_out_bf = _ws_matmul(
        "ffn_in", [xn2], [(w_ffn_in, 0), (w_ffn_in, n_ff_tiles)], [(0, 0), (0, 1)],
        n_ff_tiles, tm, tf,
        _with_side_cast(functools.partial(
            _ffn_in_epilogue, tm=tm, tiles_per_seq=seq // tm, width=width)),
        extras=(fcw, fcw, fcb, fcb, w_ffn_out),
        extra_specs=(((width, tf), lambda j, i: (0, j)),
                     ((width, tf), lambda j, i: (0, j + n_ff_tiles)),
                     ((1, tf), lambda j, i: (0, j)),
                     ((1, tf), lambda j, i: (0, j + n_ff_tiles)),
                     slab),
        out_shapes=(jax.ShapeDtypeStruct((t, d_ff), BF16),
                    jax.ShapeDtypeStruct((d_ff, d_model), BF16)),
        out_specs=(_tile_spec(tm, tf), slab),
        scratch=[pltpu.VMEM((tf // LANES, SUBLANES, LANES), F32)] * 2,
        pad=SUBLANES, lane_split=True, lag=1, m_split=tm // MXU_ROW_BLOCK, a_ring=True)

    (out,) = _ws_matmul(
        "ffn_out", [act], [(w_ffn_out_bf, 0)], [(0, 0)], n_model_tiles, tmo, tn,
        _residual_epilogue,
        extras=(x1,), extra_specs=(_tile_spec(tmo, tn),),
        out_shapes=(jax.ShapeDtypeStruct((t, d_model), F32),),
        out_specs=(_tile_spec(tmo, tn),))
    return out.reshape(batch, seq, d_model)


def kernel(x, positions, norm1_g, w_in, q_norm_g, k_norm_g, w_o_attn, conv_w, conv_b,
           conv_ln_g, conv_ln_b, w_o_conv, w_out, norm2_g, w_ffn_in, ffn_conv_w,
           ffn_conv_b, w_ffn_out):
    depth = norm1_g.shape[0]
    params = (norm1_g, w_in, q_norm_g, k_norm_g, w_o_attn, conv_w, conv_b, conv_ln_g,
              conv_ln_b, w_o_conv, w_out, norm2_g, w_ffn_in, ffn_conv_w, ffn_conv_b, w_ffn_out)
    for l in range(depth):
        layer = [p.reshape(p.shape[1:]) if depth == 1 else p[l] for p in params]
        x = _layer(x, positions, *layer)
    return x
```

```python
import functools

import jax
import jax.numpy as jnp
from jax import lax
from jax.experimental import pallas as pl
from jax.experimental.pallas import tpu as pltpu

HEAD_DIM = 128
MOBA_BLOCK = 256
MOBA_TOP_K = 3
ROPE_THETA = 10000.0
EPS = 1e-6

LANES = 128
SUBLANES = 8
VMEM_LIMIT_BYTES = 56 * 1024 * 1024
ROW_TILE = 1024
COL_TILE = 512
PAIR_COL_TILE = 256
FFN_OUT_ROW_TILE = 512
NORM_ROW_TILE = 512
FFN_ROW_CHUNK = 64
CONV_ROW_GROUP = 128
MXU_ROW_BLOCK = 128
A_RING_AHEAD = 3
ATTN_SCORES_AHEAD = 3
ATTN_OUTPUT_BEHIND = 2

F32 = jnp.float32
BF16 = jnp.bfloat16


def _params(n_grid_dims):
    return pltpu.CompilerParams(
        dimension_semantics=("arbitrary",) * n_grid_dims,
        vmem_limit_bytes=VMEM_LIMIT_BYTES)


def _rmsnorm_kernel(x_ref, g_ref, o_ref):
    x = x_ref[...]
    ms = jnp.mean(x * x, axis=-1, keepdims=True)
    o_ref[...] = (x * lax.rsqrt(ms + EPS) * g_ref[...]).astype(o_ref.dtype)


def _rmsnorm_rope_kernel(x_ref, g_ref, pos_ref, invf_ref, sign_ref, o_ref, cos_ref, sin_ref):
    _rmsnorm_kernel(x_ref, g_ref, o_ref)
    ang = pos_ref[...] * invf_ref[...]
    cos_ref[...] = jnp.cos(ang)
    sin_ref[...] = jnp.sin(ang) * sign_ref[...]


def _rmsnorm(x, g, positions=None, tr=NORM_ROW_TILE):
    t, d = x.shape
    row_spec = pl.BlockSpec((tr, d), lambda i: (i, 0))
    const = lambda cols: pl.BlockSpec((1, cols), lambda i: (0, 0))
    if positions is None:
        return pl.pallas_call(
            _rmsnorm_kernel,
            grid=(t // tr,),
            in_specs=[row_spec, const(d)],
            out_specs=row_spec,
            out_shape=jax.ShapeDtypeStruct((t, d), BF16),
            compiler_params=_params(1),
            name="rmsnorm",
        )(x, g.reshape(1, d))
    half = HEAD_DIM // 2
    inv = ROPE_THETA ** (-jnp.arange(half, dtype=F32) / half)
    invf = jnp.concatenate([inv, inv]).reshape(1, HEAD_DIM)
    sign = jnp.concatenate([-jnp.ones((half,), F32), jnp.ones((half,), F32)]).reshape(1, HEAD_DIM)
    pos = positions.astype(F32).reshape(t, 1)
    table_spec = pl.BlockSpec((tr, HEAD_DIM), lambda i: (i, 0))
    return pl.pallas_call(
        _rmsnorm_rope_kernel,
        grid=(t // tr,),
        in_specs=[row_spec, const(d), pl.BlockSpec((tr, 1), lambda i: (i, 0)),
                  const(HEAD_DIM), const(HEAD_DIM)],
        out_specs=[row_spec, table_spec, table_spec],
        out_shape=[jax.ShapeDtypeStruct((t, d), BF16),
                   jax.ShapeDtypeStruct((t, HEAD_DIM), F32),
                   jax.ShapeDtypeStruct((t, HEAD_DIM), F32)],
        compiler_params=_params(1),
        name="rmsnorm_rope",
    )(x, g.reshape(1, d), pos, invf, sign)


def _ws_kernel(*refs, n_a, n_w, dots, n_extra, n_out, epilogue, n_row_tiles, n_tiles, pad, lag,
               lane_split, m_split, a_ring, w_offsets):
    a_refs = refs[:n_a]
    w_refs = refs[n_a:n_a + n_w]
    extra_refs = refs[n_a + n_w:n_a + n_w + n_extra]
    out_refs = refs[n_a + n_w + n_extra:n_a + n_w + n_extra + n_out]
    scratch = refs[n_a + n_w + n_extra + n_out:]
    if a_ring:
        a_bufs = scratch[-2 * n_a::2]
        a_sems = scratch[-2 * n_a + 1::2]
        scratch = scratch[:-2 * n_a]
    w_prefetch = w_offsets is not None
    if w_prefetch:
        w_bufs = scratch[-2 * n_w::2]
        w_sems = scratch[-2 * n_w + 1::2]
        scratch = scratch[:-2 * n_w]
    to_cast = [w_ref for w_ref in w_refs if w_ref.dtype != BF16]
    cast_refs = scratch[:len(to_cast)]
    n_slots = 1 + lag
    n_raw = n_slots * len(dots)
    raw_refs = scratch[len(to_cast):len(to_cast) + n_raw]
    user_scratch = scratch[len(to_cast) + n_raw:]
    s = pl.program_id(0)
    tile = jnp.minimum(s, n_tiles - 1)

    if a_ring:
        depth = a_bufs[0].shape[0]
        tm_a = a_bufs[0].shape[1]

        def a_copies(t):
            rows = pl.ds(pl.multiple_of((t % n_row_tiles) * tm_a, tm_a), tm_a)
            slot = t % depth
            return [pltpu.make_async_copy(a_ref.at[rows, :], a_buf.at[slot], a_sem.at[slot])
                    for a_ref, a_buf, a_sem in zip(a_refs, a_bufs, a_sems)]

        @pl.when(s == 0)
        def _prime():
            for t in range(min(A_RING_AHEAD, n_tiles)):
                for copy in a_copies(t):
                    copy.start()

        @pl.when(s + A_RING_AHEAD < n_tiles)
        def _prefetch():
            for copy in a_copies(s + A_RING_AHEAD):
                copy.start()

        @pl.when(s < n_tiles)
        def _arrive():
            for copy in a_copies(s):
                copy.wait()

        a_tiles = [a_buf.at[tile % depth] for a_buf in a_bufs]
    else:
        a_tiles = a_refs

    if w_prefetch:
        col = tile // n_row_tiles
        n_col_tiles = n_tiles // n_row_tiles

        def w_copies(j):
            copies = []
            for w_ref, w_buf, w_sem, off in zip(w_refs, w_bufs, w_sems, w_offsets):
                tn_w = w_buf.shape[1]
                cols = pl.ds(pl.multiple_of((j + off) * tn_w, tn_w), tn_w)
                copies.append(pltpu.make_async_copy(w_ref.at[:, cols], w_buf, w_sem.at[0]))
            return copies

        @pl.when(s == 0)
        def _first_weights():
            for copy in w_copies(0):
                copy.start()

        @pl.when((tile % n_row_tiles == 0) & (s < n_tiles))
        def _cast_weights():
            for copy in w_copies(col):
                copy.wait()
            for w_buf, wbf_ref in zip(w_bufs, cast_refs):
                wbf_ref[...] = w_buf[...].astype(BF16)

            @pl.when(col + 1 < n_col_tiles)
            def _next_weights():
                for copy in w_copies(col + 1):
                    copy.start()
    elif to_cast:
        @pl.when((tile % n_row_tiles == 0) & (s < n_tiles))
        def _cast_weights():
            for w_ref, wbf_ref in zip(to_cast, cast_refs):
                wbf_ref[...] = w_ref[...].astype(BF16)

    if lag or user_scratch:
        @pl.when(s == 0)
        def _init():
            if lag:
                for raw in raw_refs[1::2]:
                    raw[...] = jnp.zeros(raw.shape, F32)
            for u in user_scratch:
                u[...] = jnp.zeros(u.shape, u.dtype)

    cast_iter = iter(cast_refs)
    wbf_refs = [w_ref if w_ref.dtype == BF16 else next(cast_iter) for w_ref in w_refs]
    done = jnp.maximum(s - lag, 0)

    def step(slot):
        tm = a_tiles[0].shape[0]
        rb = tm // m_split
        for k in range(m_split):
            rows = (k * rb, (k + 1) * rb)
            for d, (ai, wi) in enumerate(dots):
                res = jnp.dot(a_tiles[ai][rows[0]:rows[1], :], wbf_refs[wi][...],
                              preferred_element_type=F32)
                raw = raw_refs[n_slots * d + slot]
                if lane_split:
                    for lt in range(raw.shape[0]):
                        raw[lt, pad + rows[0]:pad + rows[1], :] = res[:, lt * LANES:(lt + 1) * LANES]
                else:
                    raw[pad + rows[0]:pad + rows[1], :] = res
            epilogue(done // n_row_tiles, done % n_row_tiles,
                     [raw_refs[n_slots * d + (slot + lag) % n_slots] for d in range(len(dots))],
                     extra_refs, out_refs, user_scratch, rows)

    if lag:
        for slot in range(2):
            pl.when(s % 2 == slot)(functools.partial(step, slot))
    else:
        step(0)


def _ws_matmul(name, a_list, w_list, dots, n_col_tiles, tm, tn, epilogue,
               extras=(), extra_specs=(), out_shapes=(), out_specs=(), scratch=(), pad=0, lag=0,
               lane_split=False, m_split=1, a_ring=False):
    t = a_list[0].shape[0]
    n_row_tiles = t // tm
    n_tiles = n_col_tiles * n_row_tiles

    def dot_tile(s):
        return jnp.minimum(s, n_tiles - 1)

    def lagged(index_map):
        def wrapped(s):
            done = jnp.maximum(s - lag, 0)
            return index_map(done // n_row_tiles, done % n_row_tiles)
        return wrapped

    ring_scratch = []
    if a_ring:
        in_specs = [pl.BlockSpec(memory_space=pl.ANY) for _ in a_list]
        for a in a_list:
            ring_scratch += [pltpu.VMEM((A_RING_AHEAD + 1, tm, a.shape[1]), a.dtype),
                             pltpu.SemaphoreType.DMA((A_RING_AHEAD + 1,))]
    else:
        in_specs = [pl.BlockSpec((tm, a.shape[1]), lambda s: (dot_tile(s) % n_row_tiles, 0))
                    for a in a_list]
    w_prefetch = all(w.dtype != BF16 for w, _ in w_list)
    w_scratch = []
    for w, off in w_list:
        if w_prefetch:
            in_specs.append(pl.BlockSpec(memory_space=pl.ANY))
            w_scratch += [pltpu.VMEM((w.shape[0], tn), w.dtype), pltpu.SemaphoreType.DMA((1,))]
        else:
            in_specs.append(pl.BlockSpec((w.shape[0], tn), functools.partial(
                lambda s, off: (0, dot_tile(s) // n_row_tiles + off), off=off)))
    in_specs.extend(pl.BlockSpec(shape, lagged(fn)) for shape, fn in extra_specs)
    wbf_scratch = [pltpu.VMEM((w.shape[0], tn), BF16) for w, _ in w_list if w.dtype != BF16]
    raw_shape = (tn // LANES, pad + tm, LANES) if lane_split else (pad + tm, tn)
    raw_scratch = [pltpu.VMEM(raw_shape, F32) for _ in range((1 + lag) * len(dots))]
    kernel = functools.partial(
        _ws_kernel, n_a=len(a_list), n_w=len(w_list), dots=tuple(dots),
        n_extra=len(extras), n_out=len(out_shapes), epilogue=epilogue,
        n_row_tiles=n_row_tiles, n_tiles=n_tiles, pad=pad, lag=lag, lane_split=lane_split,
        m_split=m_split, a_ring=a_ring,
        w_offsets=tuple(off for _, off in w_list) if w_prefetch else None)
    return pl.pallas_call(
        kernel,
        grid=(n_tiles + lag,),
        in_specs=in_specs,
        out_specs=[pl.BlockSpec(shape, lagged(fn)) for shape, fn in out_specs],
        out_shape=list(out_shapes),
        scratch_shapes=wbf_scratch + raw_scratch + list(scratch) + w_scratch + ring_scratch,
        compiler_params=_params(1),
        name=name,
    )(*a_list, *[w for w, _ in w_list], *extras)


def _tile_spec(tm, tn, col_offset=0):
    return ((tm, tn), lambda j, i: (i, j + col_offset))


def _qk_epilogue(j, i, acc_refs, extra_refs, out_refs, scratch, rows, *, n_q_tiles, tm, tn):
    assert rows == (0, tm)
    cos_ref, sin_ref, g_ref = extra_refs
    o_ref, mean_ref = out_refs
    acc_ref = acc_refs[0]
    is_q = j < n_q_tiles
    g = jnp.where(is_q, g_ref[0:1, :], g_ref[1:2, :])
    scale = jnp.where(is_q, jnp.float32(HEAD_DIM ** -0.5), jnp.float32(1.0))
    cos = cos_ref[...]
    sin = sin_ref[...]
    rows_per_tile = tm // MOBA_BLOCK
    for h in range(tn // HEAD_DIM):
        cols = slice(h * HEAD_DIM, (h + 1) * HEAD_DIM)
        x = acc_ref[:, cols]
        ms = jnp.mean(x * x, axis=-1, keepdims=True)
        y = x * lax.rsqrt(ms + EPS) * g
        y = y * cos + pltpu.roll(y, HEAD_DIM // 2, 1) * sin
        o_ref[:, cols] = (y * scale).astype(o_ref.dtype)
        for r in range(rows_per_tile):
            blk = y[r * MOBA_BLOCK:(r + 1) * MOBA_BLOCK]
            mean_ref[0, r:r + 1, cols] = jnp.mean(blk, axis=0, keepdims=True)


def _store_bf16_epilogue(j, i, acc_refs, extra_refs, out_refs, scratch, rows):
    r = slice(*rows)
    out_refs[0][r, :] = acc_refs[0][r, :].astype(out_refs[0].dtype)


def _glu_epilogue(j, i, acc_refs, extra_refs, out_refs, scratch, rows):
    r = slice(*rows)
    out_refs[0][r, :] = acc_refs[0][r, :] * jax.nn.sigmoid(acc_refs[1][r, :])


def _sigmoid_epilogue(j, i, acc_refs, extra_refs, out_refs, scratch, rows):
    r = slice(*rows)
    out_refs[0][r, :] = jax.nn.sigmoid(acc_refs[0][r, :])


def _with_side_cast(epilogue):
    def wrapped(j, i, acc_refs, extra_refs, out_refs, scratch, rows):
        if rows[0] == 0:
            out_refs[-1][...] = extra_refs[-1][...].astype(out_refs[-1].dtype)
        epilogue(j, i, acc_refs, extra_refs[:-1], out_refs[:-1], scratch, rows)
    return wrapped


def _merge_epilogue(j, i, acc_refs, extra_refs, out_refs, scratch, rows):
    ga_ref, gc_ref = extra_refs
    r = slice(*rows)
    merged = ga_ref[r, :] * acc_refs[0][r, :] + gc_ref[r, :] * acc_refs[1][r, :]
    out_refs[0][r, :] = merged.astype(out_refs[0].dtype)


def _residual_epilogue(j, i, acc_refs, extra_refs, out_refs, scratch, rows):
    r = slice(*rows)
    out_refs[0][r, :] = extra_refs[0][r, :] + acc_refs[0][r, :]


def _ffn_in_epilogue(j, i, acc_refs, extra_refs, out_refs, scratch, rows, *, tm, tiles_per_seq,
                     width):
    cwg_ref, cwu_ref, cbg_ref, cbu_ref = extra_refs
    pad = SUBLANES
    seq_start = i % tiles_per_seq == 0
    if rows[0] == 0:
        for hp, carry in zip(acc_refs, scratch):
            prev = carry[...]
            hp[:, 0:pad, :] = jnp.where(seq_start, jnp.zeros_like(prev), prev)
    for lt in range(acc_refs[0].shape[0]):
        lanes = slice(lt * LANES, (lt + 1) * LANES)
        for r in range(rows[0], rows[1], FFN_ROW_CHUNK):
            convs = []
            for hp, cw_ref, cb_ref in zip(acc_refs, (cwg_ref, cwu_ref), (cbg_ref, cbu_ref)):
                y = cb_ref[:, lanes]
                for w in range(width):
                    lo = pad + r - (width - 1 - w)
                    y = y + cw_ref[w:w + 1, lanes] * hp[lt, lo:lo + FFN_ROW_CHUNK, :]
                convs.append(y)
            gate, up = convs
            out_refs[0][r:r + FFN_ROW_CHUNK, lanes] = (
                jax.nn.silu(gate) * up).astype(out_refs[0].dtype)
    if rows[1] == tm:
        for hp, carry in zip(acc_refs, scratch):
            carry[...] = hp[:, tm:tm + pad, :]


def _attn_kernel(q_ref, k_ref, v_ref, km_ref, o_ref, *, n_blocks):
    L = MOBA_BLOCK
    nt = (((1,), (1,)), ((), ()))
    tn = (((0,), (0,)), ((), ()))
    km = km_ref[...].astype(BF16)
    neg = jnp.float32(-jnp.inf)
    key = lax.broadcasted_iota(jnp.int32, (L, L), 0)
    qry = lax.broadcasted_iota(jnp.int32, (L, L), 1)
    causal = key <= qry
    blk = lax.broadcasted_iota(jnp.int32, (n_blocks, L), 0)

    def scores(i):
        qi = q_ref[i * L:(i + 1) * L, :]
        n_keys = (i + 1) * L
        st = lax.dot_general(k_ref[0:n_keys, :], qi, nt, preferred_element_type=F32)
        gate = None
        if i > MOBA_TOP_K:
            gate = lax.dot_general(km, qi, nt, preferred_element_type=F32)
        return st, gate

    def probabilities(i, st, gate):
        pen = None
        if gate is not None:
            rank = jnp.zeros((n_blocks, L), F32)
            for jp in range(i):
                gb = jnp.broadcast_to(gate[jp:jp + 1, :], (n_blocks, L))
                beats = (gb > gate) | ((gb == gate) & (jp < blk))
                rank = rank + jnp.where(beats, 1.0, 0.0)
            pen = jnp.where(rank < MOBA_TOP_K, 0.0, neg)
        chunks = []
        for jb in range(i + 1):
            sj = st[jb * L:(jb + 1) * L, :]
            if jb == i:
                sj = jnp.where(causal, sj, neg)
            elif pen is not None:
                sj = sj + pen[jb:jb + 1, :]
            chunks.append(sj)
        top = chunks[0]
        for c in chunks[1:]:
            top = jnp.maximum(top, c)
        m = top.max(axis=0, keepdims=True)
        es = [jnp.exp(c - m) for c in chunks]
        tot = es[0]
        for e in es[1:]:
            tot = tot + e
        inv_l = 1.0 / tot.sum(axis=0, keepdims=True)
        return jnp.concatenate([(e * inv_l).astype(BF16) for e in es], axis=0)

    def output(i, pt):
        n_keys = (i + 1) * L
        ot = lax.dot_general(v_ref[0:n_keys, :], pt, tn, preferred_element_type=F32)
        o_ref[i * L:(i + 1) * L, :] = ot.T.astype(o_ref.dtype)

    st = {}
    pt = {}
    for i in range(min(ATTN_SCORES_AHEAD, n_blocks)):
        st[i] = scores(i)
    for i in range(n_blocks):
        pt[i] = probabilities(i, *st.pop(i))
        if i + ATTN_SCORES_AHEAD < n_blocks:
            st[i + ATTN_SCORES_AHEAD] = scores(i + ATTN_SCORES_AHEAD)
        if i >= ATTN_OUTPUT_BEHIND:
            output(i - ATTN_OUTPUT_BEHIND, pt.pop(i - ATTN_OUTPUT_BEHIND))
    for i in sorted(pt):
        output(i, pt[i])


def _moba_attention(qk, v, kmean, batch, seq, n_heads):
    t = qk.shape[0]
    n_blocks = seq // MOBA_BLOCK
    return pl.pallas_call(
        functools.partial(_attn_kernel, n_blocks=n_blocks),
        grid=(batch, n_heads),
        in_specs=[pl.BlockSpec((seq, HEAD_DIM), lambda b, h: (b, h)),
                  pl.BlockSpec((seq, HEAD_DIM), lambda b, h: (b, n_heads + h)),
                  pl.BlockSpec((seq, HEAD_DIM), lambda b, h: (b, h)),
                  pl.BlockSpec((n_blocks, HEAD_DIM), lambda b, h: (b, n_heads + h))],
        out_specs=pl.BlockSpec((seq, HEAD_DIM), lambda b, h: (b, h)),
        out_shape=jax.ShapeDtypeStruct((t, n_heads * HEAD_DIM), BF16),
        compiler_params=_params(2),
        name="moba_attention",
    )(qk, qk, v, kmean)


def _conv_kernel(h_ref, halo_ref, cw_ref, cb_ref, g_ref, b_ref, o_ref, xpad_ref, y_ref, z_ref,
                 *, ts, halo, width):
    t = pl.program_id(1)
    prev = halo_ref[...]
    xpad_ref[0:halo, :] = jnp.where(t == 0, jnp.zeros_like(prev), prev)
    xpad_ref[halo:, :] = h_ref[...]
    n_chunks = h_ref.shape[1] // LANES
    base = halo - (width - 1)

    def chunk(c, carry):
        cols = pl.ds(pl.multiple_of(c * LANES, LANES), LANES)
        groups = []
        for r in range(SUBLANES):
            taps = [w for w in range(width) if (base + w) % SUBLANES == r]
            if taps:
                span = taps[-1] - taps[0] + ts
                z_ref[r, 0:span, :] = xpad_ref[base + taps[0]:base + taps[0] + span, cols]
                groups.append((r, taps))
        for r0 in range(0, ts, CONV_ROW_GROUP):
            acc = jnp.broadcast_to(cb_ref[:, cols], (CONV_ROW_GROUP, LANES))
            for r, taps in groups:
                for w in taps:
                    off = r0 + w - taps[0]
                    acc = acc + cw_ref[w:w + 1, cols] * z_ref[r, off:off + CONV_ROW_GROUP, :]
            y_ref[r0:r0 + CONV_ROW_GROUP, cols] = acc
        return carry

    lax.fori_loop(0, n_chunks, chunk, 0)
    y = y_ref[...]
    mu = jnp.mean(y, axis=-1, keepdims=True)
    yc = y - mu
    var = jnp.mean(yc * yc, axis=-1, keepdims=True)
    z = yc * lax.rsqrt(var + EPS) * g_ref[...] + b_ref[...]
    o_ref[...] = jax.nn.silu(z).astype(o_ref.dtype)


def _conv_module(h, conv_w, conv_b, ln_g, ln_b, batch, seq, ts=256, halo=32):
    t, c = h.shape
    width = conv_w.shape[0]
    tiles = seq // ts
    halo_per_tile = ts // halo

    def halo_map(b, s):
        return (jnp.maximum((b * tiles + s) * halo_per_tile - 1, 0), 0)

    vec = lambda: pl.BlockSpec((1, c), lambda b, s: (0, 0))
    return pl.pallas_call(
        functools.partial(_conv_kernel, ts=ts, halo=halo, width=width),
        grid=(batch, tiles),
        in_specs=[pl.BlockSpec((ts, c), lambda b, s: (b * tiles + s, 0)),
                  pl.BlockSpec((halo, c), halo_map),
                  pl.BlockSpec((width, c), lambda b, s: (0, 0)),
                  vec(), vec(), vec()],
        out_specs=pl.BlockSpec((ts, c), lambda b, s: (b * tiles + s, 0)),
        out_shape=jax.ShapeDtypeStruct((t, c), BF16),
        scratch_shapes=[pltpu.VMEM((halo + ts, c), F32), pltpu.VMEM((ts, c), F32),
                        pltpu.VMEM((SUBLANES, ts + halo, LANES), F32)],
        compiler_params=_params(2),
        name="conv_module",
    )(h, h, conv_w, conv_b.reshape(1, c), ln_g.reshape(1, c), ln_b.reshape(1, c))


def _layer(x, positions, norm1_g, w_in, q_norm_g, k_norm_g, w_o_attn, conv_w, conv_b,
           conv_ln_g, conv_ln_b, w_o_conv, w_out, norm2_g, w_ffn_in, ffn_conv_w,
           ffn_conv_b, w_ffn_out):
    batch, seq, d_model = x.shape
    t = batch * seq
    attn_width = w_o_attn.shape[0]
    conv_dim = w_o_conv.shape[0]
    d_ff = w_ffn_out.shape[0]
    n_heads = attn_width // HEAD_DIM
    assert seq % MOBA_BLOCK == 0
    assert w_in.shape[1] == 3 * attn_width + 2 * conv_dim + 2 * d_model

    x2 = x.reshape(t, d_model)
    xn, cos, sin = _rmsnorm(x2, norm1_g, positions)

    tm, tn = ROW_TILE, COL_TILE
    qk_g = jnp.stack([q_norm_g, k_norm_g])
    n_q_tiles = attn_width // tn
    qk, kmean = _ws_matmul(
        "qk_proj", [xn], [(w_in, 0)], [(0, 0)], 2 * n_q_tiles, tm, tn,
        functools.partial(_qk_epilogue, n_q_tiles=n_q_tiles, tm=tm, tn=tn), lag=1, a_ring=True,
        extras=(cos, sin, qk_g),
        extra_specs=(((tm, HEAD_DIM), lambda j, i: (i, 0)),
                     ((tm, HEAD_DIM), lambda j, i: (i, 0)),
                     ((2, HEAD_DIM), lambda j, i: (0, 0))),
        out_shapes=(jax.ShapeDtypeStruct((t, 2 * attn_width), BF16),
                    jax.ShapeDtypeStruct((t // tm, tm // MOBA_BLOCK, 2 * attn_width), F32)),
        out_specs=(_tile_spec(tm, tn),
                   ((1, tm // MOBA_BLOCK, tn), lambda j, i: (i, 0, j))))
    kmean = kmean.reshape(t // MOBA_BLOCK, 2 * attn_width)
    (v,) = _ws_matmul(
        "v_proj", [xn], [(w_in, 2 * attn_width // tn)], [(0, 0)], attn_width // tn, tm, tn,
        _store_bf16_epilogue, a_ring=True,
        out_shapes=(jax.ShapeDtypeStruct((t, attn_width), BF16),),
        out_specs=(_tile_spec(tm, tn),))
    tg = PAIR_COL_TILE
    u_off = 3 * attn_width
    (h,) = _ws_matmul(
        "glu_proj", [xn], [(w_in, u_off // tg), (w_in, (u_off + conv_dim) // tg)],
        [(0, 0), (0, 1)], conv_dim // tg, tm, tg, _glu_epilogue, a_ring=True,
        out_shapes=(jax.ShapeDtypeStruct((t, conv_dim), F32),),
        out_specs=(_tile_spec(tm, tg),))
    g_off = u_off + 2 * conv_dim
    (gates,) = _ws_matmul(
        "gate_proj", [xn], [(w_in, g_off // tn)], [(0, 0)], 2 * d_model // tn, tm, tn,
        _sigmoid_epilogue, lag=1, m_split=tm // MXU_ROW_BLOCK, a_ring=True,
        out_shapes=(jax.ShapeDtypeStruct((t, 2 * d_model), F32),),
        out_specs=(_tile_spec(tm, tn),))

    attn = _moba_attention(qk, v, kmean, batch, seq, n_heads)
    hc = _conv_module(h, conv_w, conv_b, conv_ln_g, conv_ln_b, batch, seq)

    n_model_tiles = d_model // tn
    tmo = FFN_OUT_ROW_TILE
    (merged,) = _ws_matmul(
        "merge_proj", [attn, hc], [(w_o_attn, 0), (w_o_conv, 0)], [(0, 0), (1, 1)],
        n_model_tiles, tm, tn, _merge_epilogue, a_ring=True,
        extras=(gates, gates),
        extra_specs=(_tile_spec(tm, tn), _tile_spec(tm, tn, n_model_tiles)),
        out_shapes=(jax.ShapeDtypeStruct((t, d_model), BF16),),
        out_specs=(_tile_spec(tm, tn),))
    (x1,) = _ws_matmul(
        "out_proj", [merged], [(w_out, 0)], [(0, 0)], n_model_tiles, tm, tn,
        _residual_epilogue, a_ring=True,
        extras=(x2,), extra_specs=(_tile_spec(tm, tn),),
        out_shapes=(jax.ShapeDtypeStruct((t, d_model), F32),),
        out_specs=(_tile_spec(tm, tn),))

    xn2 = _rmsnorm(x1, norm2_g)
    tf = PAIR_COL_TILE
    assert d_ff % tf == 0 and seq % tm == 0
    n_ff_tiles = d_ff // tf
    fcw = ffn_conv_w
    fcb = ffn_conv_b.reshape(1, 2 * d_ff)
    width = fcw.shape[0]
    n_ff_steps = n_ff_tiles * (t // tm)
    assert d_ff % (n_ff_steps * SUBLANES) == 0
    slab = ((d_ff // n_ff_steps, d_model), lambda j, i: (j * (t // tm) + i, 0))
    act, w_ffn_out_bf = _ws_matmul(
        "ffn_in", [xn2], [(w_ffn_in, 0), (w_ffn_in, n_ff_tiles)], [(0, 0), (0, 1)],
        n_ff_tiles, tm, tf,
        _with_side_cast(functools.partial(
            _ffn_in_epilogue, tm=tm, tiles_per_seq=seq // tm, width=width)),
        extras=(fcw, fcw, fcb, fcb, w_ffn_out),
        extra_specs=(((width, tf), lambda j, i: (0, j)),
                     ((width, tf), lambda j, i: (0, j + n_ff_tiles)),
                     ((1, tf), lambda j, i: (0, j)),
                     ((1, tf), lambda j, i: (0, j + n_ff_tiles)),
                     slab),
        out_shapes=(jax.ShapeDtypeStruct((t, d_ff), BF16),
                    jax.ShapeDtypeStruct((d_ff, d_model), BF16)),
        out_specs=(_tile_spec(tm, tf), slab),
        scratch=[pltpu.VMEM((tf // LANES, SUBLANES, LANES), F32)] * 2,
        pad=SUBLANES, lane_split=True, lag=1, m_split=tm // MXU_ROW_BLOCK, a_ring=True)

    (out,) = _ws_matmul(
        "ffn_out", [act], [(w_ffn_out_bf, 0)], [(0, 0)], n_model_tiles, tmo, tn,
        _residual_epilogue,
        extras=(x1,), extra_specs=(_tile_spec(tmo, tn),),
        out_shapes=(jax.ShapeDtypeStruct((t, d_model), F32),),
        out_specs=(_tile_spec(tmo, tn),))
    return out.reshape(batch, seq, d_model)


def kernel(x, positions, norm1_g, w_in, q_norm_g, k_norm_g, w_o_attn, conv_w, conv_b,
           conv_ln_g, conv_ln_b, w_o_conv, w_out, norm2_g, w_ffn_in, ffn_conv_w,
           ffn_conv_b, w_ffn_out):
    depth = norm1_g.shape[0]
    params = (norm1_g, w_in, q_norm_g, k_norm_g, w_o_attn, conv_w, conv_b, conv_ln_g,
              conv_ln_b, w_o_conv, w_out, norm2_g, w_ffn_in, ffn_conv_w, ffn_conv_b, w_ffn_out)
    for l in range(depth):
        layer = [p.reshape(p.shape[1:]) if depth == 1 else p[l] for p in params]
        x = _layer(x, positions, *layer)
    return x
```

```python
import functools

import jax
import jax.numpy as jnp
from jax import lax
from jax.experimental import pallas as pl
from jax.experimental.pallas import tpu as pltpu

HEAD_DIM = 128
MOBA_BLOCK = 256
MOBA_TOP_K = 3
ROPE_THETA = 10000.0
EPS = 1e-6

LANES = 128
SUBLANES = 8
VMEM_LIMIT_BYTES = 56 * 1024 * 1024
ROW_TILE = 1024
COL_TILE = 512
PAIR_COL_TILE = 256
FFN_OUT_ROW_TILE = 512
NORM_ROW_TILE = 512
FFN_ROW_CHUNK = 64
CONV_ROW_GROUP = 128
MXU_ROW_BLOCK = 128
A_RING_AHEAD = 2
A_RING_DMA_PRIORITY = 1
ATTN_SCORES_AHEAD = 3
ATTN_OUTPUT_BEHIND = 2

F32 = jnp.float32
BF16 = jnp.bfloat16


def _params(n_grid_dims):
    return pltpu.CompilerParams(
        dimension_semantics=("arbitrary",) * n_grid_dims,
        vmem_limit_bytes=VMEM_LIMIT_BYTES)


def _rmsnorm_kernel(x_ref, g_ref, o_ref):
    x = x_ref[...]
    ms = jnp.mean(x * x, axis=-1, keepdims=True)
    o_ref[...] = (x * lax.rsqrt(ms + EPS) * g_ref[...]).astype(o_ref.dtype)


def _rmsnorm_rope_kernel(x_ref, g_ref, pos_ref, invf_ref, sign_ref, o_ref, cos_ref, sin_ref):
    _rmsnorm_kernel(x_ref, g_ref, o_ref)
    ang = pos_ref[...] * invf_ref[...]
    cos_ref[...] = jnp.cos(ang)
    sin_ref[...] = jnp.sin(ang) * sign_ref[...]


def _rmsnorm(x, g, positions=None, tr=NORM_ROW_TILE):
    t, d = x.shape
    row_spec = pl.BlockSpec((tr, d), lambda i: (i, 0))
    const = lambda cols: pl.BlockSpec((1, cols), lambda i: (0, 0))
    if positions is None:
        return pl.pallas_call(
            _rmsnorm_kernel,
            grid=(t // tr,),
            in_specs=[row_spec, const(d)],
            out_specs=row_spec,
            out_shape=jax.ShapeDtypeStruct((t, d), BF16),
            compiler_params=_params(1),
            name="rmsnorm",
        )(x, g.reshape(1, d))
    half = HEAD_DIM // 2
    inv = ROPE_THETA ** (-jnp.arange(half, dtype=F32) / half)
    invf = jnp.concatenate([inv, inv]).reshape(1, HEAD_DIM)
    sign = jnp.concatenate([-jnp.ones((half,), F32), jnp.ones((half,), F32)]).reshape(1, HEAD_DIM)
    pos = positions.astype(F32).reshape(t, 1)
    table_spec = pl.BlockSpec((tr, HEAD_DIM), lambda i: (i, 0))
    return pl.pallas_call(
        _rmsnorm_rope_kernel,
        grid=(t // tr,),
        in_specs=[row_spec, const(d), pl.BlockSpec((tr, 1), lambda i: (i, 0)),
                  const(HEAD_DIM), const(HEAD_DIM)],
        out_specs=[row_spec, table_spec, table_spec],
        out_shape=[jax.ShapeDtypeStruct((t, d), BF16),
                   jax.ShapeDtypeStruct((t, HEAD_DIM), F32),
                   jax.ShapeDtypeStruct((t, HEAD_DIM), F32)],
        compiler_params=_params(1),
        name="rmsnorm_rope",
    )(x, g.reshape(1, d), pos, invf, sign)


def _ws_kernel(*refs, n_a, n_w, dots, n_extra, n_out, epilogue, n_row_tiles, n_tiles, pad, lag,
               lane_split, m_split, a_ring):
    a_refs = refs[:n_a]
    w_refs = refs[n_a:n_a + n_w]
    extra_refs = refs[n_a + n_w:n_a + n_w + n_extra]
    out_refs = refs[n_a + n_w + n_extra:n_a + n_w + n_extra + n_out]
    scratch = refs[n_a + n_w + n_extra + n_out:]
    if a_ring:
        a_bufs = scratch[-2 * n_a::2]
        a_sems = scratch[-2 * n_a + 1::2]
        scratch = scratch[:-2 * n_a]
    to_cast = [w_ref for w_ref in w_refs if w_ref.dtype != BF16]
    cast_refs = scratch[:len(to_cast)]
    n_slots = 1 + lag
    n_raw = n_slots * len(dots)
    raw_refs = scratch[len(to_cast):len(to_cast) + n_raw]
    user_scratch = scratch[len(to_cast) + n_raw:]
    s = pl.program_id(0)
    tile = jnp.minimum(s, n_tiles - 1)

    if a_ring:
        depth = a_bufs[0].shape[0]
        tm_a = a_bufs[0].shape[1]

        def a_copies(t):
            rows = pl.ds(pl.multiple_of((t % n_row_tiles) * tm_a, tm_a), tm_a)
            slot = t % depth
            return [pltpu.make_async_copy(a_ref.at[rows, :], a_buf.at[slot], a_sem.at[slot])
                    for a_ref, a_buf, a_sem in zip(a_refs, a_bufs, a_sems)]

        @pl.when(s == 0)
        def _prime():
            for t in range(min(A_RING_AHEAD, n_tiles)):
                for copy in a_copies(t):
                    copy.start(priority=A_RING_DMA_PRIORITY)

        @pl.when(s + A_RING_AHEAD < n_tiles)
        def _prefetch():
            for copy in a_copies(s + A_RING_AHEAD):
                copy.start(priority=A_RING_DMA_PRIORITY)

        @pl.when(s < n_tiles)
        def _arrive():
            for copy in a_copies(s):
                copy.wait()

        a_tiles = [a_buf.at[tile % depth] for a_buf in a_bufs]
    else:
        a_tiles = a_refs

    if to_cast:
        @pl.when((tile % n_row_tiles == 0) & (s < n_tiles))
        def _cast_weights():
            for w_ref, wbf_ref in zip(to_cast, cast_refs):
                wbf_ref[...] = w_ref[...].astype(BF16)

    if lag or user_scratch:
        @pl.when(s == 0)
        def _init():
            if lag:
                for raw in raw_refs[1::2]:
                    raw[...] = jnp.zeros(raw.shape, F32)
            for u in user_scratch:
                u[...] = jnp.zeros(u.shape, u.dtype)

    cast_iter = iter(cast_refs)
    wbf_refs = [w_ref if w_ref.dtype == BF16 else next(cast_iter) for w_ref in w_refs]
    done = jnp.maximum(s - lag, 0)

    def step(slot):
        tm = a_tiles[0].shape[0]
        rb = tm // m_split
        for k in range(m_split):
            rows = (k * rb, (k + 1) * rb)
            for d, (ai, wi) in enumerate(dots):
                res = jnp.dot(a_tiles[ai][rows[0]:rows[1], :], wbf_refs[wi][...],
                              preferred_element_type=F32)
                raw = raw_refs[n_slots * d + slot]
                if lane_split:
                    for lt in range(raw.shape[0]):
                        raw[lt, pad + rows[0]:pad + rows[1], :] = res[:, lt * LANES:(lt + 1) * LANES]
                else:
                    raw[pad + rows[0]:pad + rows[1], :] = res
            epilogue(done // n_row_tiles, done % n_row_tiles,
                     [raw_refs[n_slots * d + (slot + lag) % n_slots] for d in range(len(dots))],
                     extra_refs, out_refs, user_scratch, rows)

    if lag:
        for slot in range(2):
            pl.when(s % 2 == slot)(functools.partial(step, slot))
    else:
        step(0)


def _ws_matmul(name, a_list, w_list, dots, n_col_tiles, tm, tn, epilogue,
               extras=(), extra_specs=(), out_shapes=(), out_specs=(), scratch=(), pad=0, lag=0,
               lane_split=False, m_split=1, a_ring=False):
    t = a_list[0].shape[0]
    n_row_tiles = t // tm
    n_tiles = n_col_tiles * n_row_tiles

    def dot_tile(s):
        return jnp.minimum(s, n_tiles - 1)

    def lagged(index_map):
        def wrapped(s):
            done = jnp.maximum(s - lag, 0)
            return index_map(done // n_row_tiles, done % n_row_tiles)
        return wrapped

    ring_scratch = []
    if a_ring:
        in_specs = [pl.BlockSpec(memory_space=pl.ANY) for _ in a_list]
        for a in a_list:
            ring_scratch += [pltpu.VMEM((A_RING_AHEAD + 1, tm, a.shape[1]), a.dtype),
                             pltpu.SemaphoreType.DMA((A_RING_AHEAD + 1,))]
    else:
        in_specs = [pl.BlockSpec((tm, a.shape[1]), lambda s: (dot_tile(s) % n_row_tiles, 0))
                    for a in a_list]
    for w, off in w_list:
        in_specs.append(pl.BlockSpec((w.shape[0], tn), functools.partial(
            lambda s, off: (0, dot_tile(s) // n_row_tiles + off), off=off)))
    in_specs.extend(pl.BlockSpec(shape, lagged(fn)) for shape, fn in extra_specs)
    wbf_scratch = [pltpu.VMEM((w.shape[0], tn), BF16) for w, _ in w_list if w.dtype != BF16]
    raw_shape = (tn // LANES, pad + tm, LANES) if lane_split else (pad + tm, tn)
    raw_scratch = [pltpu.VMEM(raw_shape, F32) for _ in range((1 + lag) * len(dots))]
    kernel = functools.partial(
        _ws_kernel, n_a=len(a_list), n_w=len(w_list), dots=tuple(dots),
        n_extra=len(extras), n_out=len(out_shapes), epilogue=epilogue,
        n_row_tiles=n_row_tiles, n_tiles=n_tiles, pad=pad, lag=lag, lane_split=lane_split,
        m_split=m_split, a_ring=a_ring)
    return pl.pallas_call(
        kernel,
        grid=(n_tiles + lag,),
        in_specs=in_specs,
        out_specs=[pl.BlockSpec(shape, lagged(fn)) for shape, fn in out_specs],
        out_shape=list(out_shapes),
        scratch_shapes=wbf_scratch + raw_scratch + list(scratch) + ring_scratch,
        compiler_params=_params(1),
        name=name,
    )(*a_list, *[w for w, _ in w_list], *extras)


def _tile_spec(tm, tn, col_offset=0):
    return ((tm, tn), lambda j, i: (i, j + col_offset))


def _qk_epilogue(j, i, acc_refs, extra_refs, out_refs, scratch, rows, *, n_q_tiles, tm, tn):
    assert rows == (0, tm)
    cos_ref, sin_ref, g_ref = extra_refs
    o_ref, mean_ref = out_refs
    acc_ref = acc_refs[0]
    is_q = j < n_q_tiles
    g = jnp.where(is_q, g_ref[0:1, :], g_ref[1:2, :])
    scale = jnp.where(is_q, jnp.float32(HEAD_DIM ** -0.5), jnp.float32(1.0))
    cos = cos_ref[...]
    sin = sin_ref[...]
    rows_per_tile = tm // MOBA_BLOCK
    for h in range(tn // HEAD_DIM):
        cols = slice(h * HEAD_DIM, (h + 1) * HEAD_DIM)
        x = acc_ref[:, cols]
        ms = jnp.mean(x * x, axis=-1, keepdims=True)
        y = x * lax.rsqrt(ms + EPS) * g
        y = y * cos + pltpu.roll(y, HEAD_DIM // 2, 1) * sin
        o_ref[:, cols] = (y * scale).astype(o_ref.dtype)
        for r in range(rows_per_tile):
            blk = y[r * MOBA_BLOCK:(r + 1) * MOBA_BLOCK]
            mean_ref[0, r:r + 1, cols] = jnp.mean(blk, axis=0, keepdims=True)


def _store_bf16_epilogue(j, i, acc_refs, extra_refs, out_refs, scratch, rows):
    r = slice(*rows)
    out_refs[0][r, :] = acc_refs[0][r, :].astype(out_refs[0].dtype)


def _glu_epilogue(j, i, acc_refs, extra_refs, out_refs, scratch, rows):
    r = slice(*rows)
    out_refs[0][r, :] = acc_refs[0][r, :] * jax.nn.sigmoid(acc_refs[1][r, :])


def _sigmoid_epilogue(j, i, acc_refs, extra_refs, out_refs, scratch, rows):
    r = slice(*rows)
    out_refs[0][r, :] = jax.nn.sigmoid(acc_refs[0][r, :])


def _with_side_cast(epilogue):
    def wrapped(j, i, acc_refs, extra_refs, out_refs, scratch, rows):
        if rows[0] == 0:
            out_refs[-1][...] = extra_refs[-1][...].astype(out_refs[-1].dtype)
        epilogue(j, i, acc_refs, extra_refs[:-1], out_refs[:-1], scratch, rows)
    return wrapped


def _merge_epilogue(j, i, acc_refs, extra_refs, out_refs, scratch, rows):
    ga_ref, gc_ref = extra_refs
    r = slice(*rows)
    merged = ga_ref[r, :] * acc_refs[0][r, :] + gc_ref[r, :] * acc_refs[1][r, :]
    out_refs[0][r, :] = merged.astype(out_refs[0].dtype)


def _residual_epilogue(j, i, acc_refs, extra_refs, out_refs, scratch, rows):
    r = slice(*rows)
    out_refs[0][r, :] = extra_refs[0][r, :] + acc_refs[0][r, :]


def _ffn_in_epilogue(j, i, acc_refs, extra_refs, out_refs, scratch, rows, *, tm, tiles_per_seq,
                     width):
    cwg_ref, cwu_ref, cbg_ref, cbu_ref = extra_refs
    pad = SUBLANES
    seq_start = i % tiles_per_seq == 0
    if rows[0] == 0:
        for hp, carry in zip(acc_refs, scratch):
            prev = carry[...]
            hp[:, 0:pad, :] = jnp.where(seq_start, jnp.zeros_like(prev), prev)
    for lt in range(acc_refs[0].shape[0]):
        lanes = slice(lt * LANES, (lt + 1) * LANES)
        for r in range(rows[0], rows[1], FFN_ROW_CHUNK):
            convs = []
            for hp, cw_ref, cb_ref in zip(acc_refs, (cwg_ref, cwu_ref), (cbg_ref, cbu_ref)):
                y = cb_ref[:, lanes]
                for w in range(width):
                    lo = pad + r - (width - 1 - w)
                    y = y + cw_ref[w:w + 1, lanes] * hp[lt, lo:lo + FFN_ROW_CHUNK, :]
                convs.append(y)
            gate, up = convs
            out_refs[0][r:r + FFN_ROW_CHUNK, lanes] = (
                jax.nn.silu(gate) * up).astype(out_refs[0].dtype)
    if rows[1] == tm:
        for hp, carry in zip(acc_refs, scratch):
            carry[...] = hp[:, tm:tm + pad, :]


def _attn_kernel(q_ref, k_ref, v_ref, km_ref, o_ref, *, n_blocks):
    L = MOBA_BLOCK
    nt = (((1,), (1,)), ((), ()))
    tn = (((0,), (0,)), ((), ()))
    km = km_ref[...].astype(BF16)
    neg = jnp.float32(-jnp.inf)
    key = lax.broadcasted_iota(jnp.int32, (L, L), 0)
    qry = lax.broadcasted_iota(jnp.int32, (L, L), 1)
    causal = key <= qry
    blk = lax.broadcasted_iota(jnp.int32, (n_blocks, L), 0)

    def scores(i):
        qi = q_ref[i * L:(i + 1) * L, :]
        n_keys = (i + 1) * L
        st = lax.dot_general(k_ref[0:n_keys, :], qi, nt, preferred_element_type=F32)
        gate = None
        if i > MOBA_TOP_K:
            gate = lax.dot_general(km, qi, nt, preferred_element_type=F32)
        return st, gate

    def probabilities(i, st, gate):
        pen = None
        if gate is not None:
            rank = jnp.zeros((n_blocks, L), F32)
            for jp in range(i):
                gb = jnp.broadcast_to(gate[jp:jp + 1, :], (n_blocks, L))
                beats = (gb > gate) | ((gb == gate) & (jp < blk))
                rank = rank + jnp.where(beats, 1.0, 0.0)
            pen = jnp.where(rank < MOBA_TOP_K, 0.0, neg)
        chunks = []
        for jb in range(i + 1):
            sj = st[jb * L:(jb + 1) * L, :]
            if jb == i:
                sj = jnp.where(causal, sj, neg)
            elif pen is not None:
                sj = sj + pen[jb:jb + 1, :]
            chunks.append(sj)
        top = chunks[0]
        for c in chunks[1:]:
            top = jnp.maximum(top, c)
        m = top.max(axis=0, keepdims=True)
        es = [jnp.exp(c - m) for c in chunks]
        tot = es[0]
        for e in es[1:]:
            tot = tot + e
        inv_l = 1.0 / tot.sum(axis=0, keepdims=True)
        return jnp.concatenate([(e * inv_l).astype(BF16) for e in es], axis=0)

    def output(i, pt):
        n_keys = (i + 1) * L
        ot = lax.dot_general(v_ref[0:n_keys, :], pt, tn, preferred_element_type=F32)
        o_ref[i * L:(i + 1) * L, :] = ot.T.astype(o_ref.dtype)

    st = {}
    pt = {}
    for i in range(min(ATTN_SCORES_AHEAD, n_blocks)):
        st[i] = scores(i)
    for i in range(n_blocks):
        pt[i] = probabilities(i, *st.pop(i))
        if i + ATTN_SCORES_AHEAD < n_blocks:
            st[i + ATTN_SCORES_AHEAD] = scores(i + ATTN_SCORES_AHEAD)
        if i >= ATTN_OUTPUT_BEHIND:
            output(i - ATTN_OUTPUT_BEHIND, pt.pop(i - ATTN_OUTPUT_BEHIND))
    for i in sorted(pt):
        output(i, pt[i])


def _moba_attention(qk, v, kmean, batch, seq, n_heads):
    t = qk.shape[0]
    n_blocks = seq // MOBA_BLOCK
    return pl.pallas_call(
        functools.partial(_attn_kernel, n_blocks=n_blocks),
        grid=(batch, n_heads),
        in_specs=[pl.BlockSpec((seq, HEAD_DIM), lambda b, h: (b, h)),
                  pl.BlockSpec((seq, HEAD_DIM), lambda b, h: (b, n_heads + h)),
                  pl.BlockSpec((seq, HEAD_DIM), lambda b, h: (b, h)),
                  pl.BlockSpec((n_blocks, HEAD_DIM), lambda b, h: (b, n_heads + h))],
        out_specs=pl.BlockSpec((seq, HEAD_DIM), lambda b, h: (b, h)),
        out_shape=jax.ShapeDtypeStruct((t, n_heads * HEAD_DIM), BF16),
        compiler_params=_params(2),
        name="moba_attention",
    )(qk, qk, v, kmean)


def _conv_kernel(h_ref, halo_ref, cw_ref, cb_ref, g_ref, b_ref, o_ref, xpad_ref, y_ref, z_ref,
                 *, ts, halo, width):
    t = pl.program_id(1)
    prev = halo_ref[...]
    xpad_ref[0:halo, :] = jnp.where(t == 0, jnp.zeros_like(prev), prev)
    xpad_ref[halo:, :] = h_ref[...]
    n_chunks = h_ref.shape[1] // LANES
    base = halo - (width - 1)

    def chunk(c, carry):
        cols = pl.ds(pl.multiple_of(c * LANES, LANES), LANES)
        groups = []
        for r in range(SUBLANES):
            taps = [w for w in range(width) if (base + w) % SUBLANES == r]
            if taps:
                span = taps[-1] - taps[0] + ts
                z_ref[r, 0:span, :] = xpad_ref[base + taps[0]:base + taps[0] + span, cols]
                groups.append((r, taps))
        for r0 in range(0, ts, CONV_ROW_GROUP):
            acc = jnp.broadcast_to(cb_ref[:, cols], (CONV_ROW_GROUP, LANES))
            for r, taps in groups:
                for w in taps:
                    off = r0 + w - taps[0]
                    acc = acc + cw_ref[w:w + 1, cols] * z_ref[r, off:off + CONV_ROW_GROUP, :]
            y_ref[r0:r0 + CONV_ROW_GROUP, cols] = acc
        return carry

    lax.fori_loop(0, n_chunks, chunk, 0)
    y = y_ref[...]
    mu = jnp.mean(y, axis=-1, keepdims=True)
    yc = y - mu
    var = jnp.mean(yc * yc, axis=-1, keepdims=True)
    z = yc * lax.rsqrt(var + EPS) * g_ref[...] + b_ref[...]
    o_ref[...] = jax.nn.silu(z).astype(o_ref.dtype)


def _conv_module(h, conv_w, conv_b, ln_g, ln_b, batch, seq, ts=256, halo=32):
    t, c = h.shape
    width = conv_w.shape[0]
    tiles = seq // ts
    halo_per_tile = ts // halo

    def halo_map(b, s):
        return (jnp.maximum((b * tiles + s) * halo_per_tile - 1, 0), 0)

    vec = lambda: pl.BlockSpec((1, c), lambda b, s: (0, 0))
    return pl.pallas_call(
        functools.partial(_conv_kernel, ts=ts, halo=halo, width=width),
        grid=(batch, tiles),
        in_specs=[pl.BlockSpec((ts, c), lambda b, s: (b * tiles + s, 0)),
                  pl.BlockSpec((halo, c), halo_map),
                  pl.BlockSpec((width, c), lambda b, s: (0, 0)),
                  vec(), vec(), vec()],
        out_specs=pl.BlockSpec((ts, c), lambda b, s: (b * tiles + s, 0)),
        out_shape=jax.ShapeDtypeStruct((t, c), BF16),
        scratch_shapes=[pltpu.VMEM((halo + ts, c), F32), pltpu.VMEM((ts, c), F32),
                        pltpu.VMEM((SUBLANES, ts + halo, LANES), F32)],
        compiler_params=_params(2),
        name="conv_module",
    )(h, h, conv_w, conv_b.reshape(1, c), ln_g.reshape(1, c), ln_b.reshape(1, c))


def _layer(x, positions, norm1_g, w_in, q_norm_g, k_norm_g, w_o_attn, conv_w, conv_b,
           conv_ln_g, conv_ln_b, w_o_conv, w_out, norm2_g, w_ffn_in, ffn_conv_w,
           ffn_conv_b, w_ffn_out):
    batch, seq, d_model = x.shape
    t = batch * seq
    attn_width = w_o_attn.shape[0]
    conv_dim = w_o_conv.shape[0]
    d_ff = w_ffn_out.shape[0]
    n_heads = attn_width // HEAD_DIM
    assert seq % MOBA_BLOCK == 0
    assert w_in.shape[1] == 3 * attn_width + 2 * conv_dim + 2 * d_model

    x2 = x.reshape(t, d_model)
    xn, cos, sin = _rmsnorm(x2, norm1_g, positions)

    tm, tn = ROW_TILE, COL_TILE
    qk_g = jnp.stack([q_norm_g, k_norm_g])
    n_q_tiles = attn_width // tn
    qk, kmean = _ws_matmul(
        "qk_proj", [xn], [(w_in, 0)], [(0, 0)], 2 * n_q_tiles, tm, tn,
        functools.partial(_qk_epilogue, n_q_tiles=n_q_tiles, tm=tm, tn=tn), lag=1, a_ring=True,
        extras=(cos, sin, qk_g),
        extra_specs=(((tm, HEAD_DIM), lambda j, i: (i, 0)),
                     ((tm, HEAD_DIM), lambda j, i: (i, 0)),
                     ((2, HEAD_DIM), lambda j, i: (0, 0))),
        out_shapes=(jax.ShapeDtypeStruct((t, 2 * attn_width), BF16),
                    jax.ShapeDtypeStruct((t // tm, tm // MOBA_BLOCK, 2 * attn_width), F32)),
        out_specs=(_tile_spec(tm, tn),
                   ((1, tm // MOBA_BLOCK, tn), lambda j, i: (i, 0, j))))
    kmean = kmean.reshape(t // MOBA_BLOCK, 2 * attn_width)
    (v,) = _ws_matmul(
        "v_proj", [xn], [(w_in, 2 * attn_width // tn)], [(0, 0)], attn_width // tn, tm, tn,
        _store_bf16_epilogue, a_ring=True,
        out_shapes=(jax.ShapeDtypeStruct((t, attn_width), BF16),),
        out_specs=(_tile_spec(tm, tn),))
    tg = PAIR_COL_TILE
    u_off = 3 * attn_width
    (h,) = _ws_matmul(
        "glu_proj", [xn], [(w_in, u_off // tg), (w_in, (u_off + conv_dim) // tg)],
        [(0, 0), (0, 1)], conv_dim // tg, tm, tg, _glu_epilogue, a_ring=True,
        out_shapes=(jax.ShapeDtypeStruct((t, conv_dim), F32),),
        out_specs=(_tile_spec(tm, tg),))
    g_off = u_off + 2 * conv_dim
    (gates,) = _ws_matmul(
        "gate_proj", [xn], [(w_in, g_off // tn)], [(0, 0)], 2 * d_model // tn, tm, tn,
        _sigmoid_epilogue, lag=1, m_split=tm // MXU_ROW_BLOCK, a_ring=True,
        out_shapes=(jax.ShapeDtypeStruct((t, 2 * d_model), F32),),
        out_specs=(_tile_spec(tm, tn),))

    attn = _moba_attention(qk, v, kmean, batch, seq, n_heads)
    hc = _conv_module(h, conv_w, conv_b, conv_ln_g, conv_ln_b, batch, seq)

    n_model_tiles = d_model // tn
    tmo = FFN_OUT_ROW_TILE
    (merged,) = _ws_matmul(
        "merge_proj", [attn, hc], [(w_o_attn, 0), (w_o_conv, 0)], [(0, 0), (1, 1)],
        n_model_tiles, tm, tn, _merge_epilogue, a_ring=True,
        extras=(gates, gates),
        extra_specs=(_tile_spec(tm, tn), _tile_spec(tm, tn, n_model_tiles)),
        out_shapes=(jax.ShapeDtypeStruct((t, d_model), BF16),),
        out_specs=(_tile_spec(tm, tn),))
    (x1,) = _ws_matmul(
        "out_proj", [merged], [(w_out, 0)], [(0, 0)], n_model_tiles, tm, tn,
        _residual_epilogue, a_ring=True,
        extras=(x2,), extra_specs=(_tile_spec(tm, tn),),
        out_shapes=(jax.ShapeDtypeStruct((t, d_model), F32),),
        out_specs=(_tile_spec(tm, tn),))

    xn2 = _rmsnorm(x1, norm2_g)
    tf = PAIR_COL_TILE
    assert d_ff % tf == 0 and seq % tm == 0
    n_ff_tiles = d_ff // tf
    fcw = ffn_conv_w
    fcb = ffn_conv_b.reshape(1, 2 * d_ff)
    width = fcw.shape[0]
    n_ff_steps = n_ff_tiles * (t // tm)
    assert d_ff % (n_ff_steps * SUBLANES) == 0
    slab = ((d_ff // n_ff_steps, d_model), lambda j, i: (j * (t // tm) + i, 0))
    act, w_ffn_out_bf = _ws_matmul(
        "ffn_in", [xn2], [(w_ffn_in, 0), (w_ffn_in, n_ff_tiles)], [(0, 0), (0, 1)],
        n_ff_tiles, tm, tf,
        _with_side_cast(functools.partial(
            _ffn_in_epilogue, tm=tm, tiles_per_seq=seq // tm, width=width)),
        extras=(fcw, fcw, fcb, fcb, w_ffn_out),
        extra_specs=(((width, tf), lambda j, i: (0, j)),
                     ((width, tf), lambda j, i: (0, j + n_ff_tiles)),
                     ((1, tf), lambda j, i: (0, j)),
                     ((1, tf), lambda j, i: (0, j + n_ff_tiles)),
                     slab),
        out_shapes=(jax.ShapeDtypeStruct((t, d_ff), BF16),
                    jax.ShapeDtypeStruct((d_ff, d_model), BF16)),
        out_specs=(_tile_spec(tm, tf), slab),
        scratch=[pltpu.VMEM((tf // LANES, SUBLANES, LANES), F32)] * 2,
        pad=SUBLANES, lane_split=True, lag=1, m_split=tm // MXU_ROW_BLOCK, a_ring=True)

    (out,) = _ws_matmul(
        "ffn_out", [act], [(w_ffn_out_bf, 0)], [(0, 0)], n_model_tiles, tmo, tn,
        _residual_epilogue,
        extras=(x1,), extra_specs=(_tile_spec(tmo, tn),),
        out_shapes=(jax.ShapeDtypeStruct((t, d_model), F32),),
        out_specs=(_tile_spec(tmo, tn),))
    return out.reshape(batch, seq, d_model)


def kernel(x, positions, norm1_g, w_in, q_norm_g, k_norm_g, w_o_attn, conv_w, conv_b,
           conv_ln_g, conv_ln_b, w_o_conv, w_out, norm2_g, w_ffn_in, ffn_conv_w,
           ffn_conv_b, w_ffn_out):
    depth = norm1_g.shape[0]
    params = (norm1_g, w_in, q_norm_g, k_norm_g, w_o_attn, conv_w, conv_b, conv_ln_g,
              conv_ln_b, w_o_conv, w_out, norm2_g, w_ffn_in, ffn_conv_w, ffn_conv_b, w_ffn_out)
    for l in range(depth):
        layer = [p.reshape(p.shape[1:]) if depth == 1 else p[l] for p in params]
        x = _layer(x, positions, *layer)
    return x
```

```python
import functools

import jax
import jax.numpy as jnp
from jax import lax
from jax.experimental import pallas as pl
from jax.experimental.pallas import tpu as pltpu

HEAD_DIM = 128
MOBA_BLOCK = 256
MOBA_TOP_K = 3
ROPE_THETA = 10000.0
EPS = 1e-6

LANES = 128
SUBLANES = 8
VMEM_LIMIT_BYTES = 56 * 1024 * 1024
ROW_TILE = 1024
COL_TILE = 512
PAIR_COL_TILE = 256
FFN_OUT_ROW_TILE = 512
NORM_ROW_TILE = 512
FFN_ROW_CHUNK = 64
CONV_ROW_GROUP = 128
MXU_ROW_BLOCK = 128
A_RING_AHEAD = 2
ATTN_SCORES_AHEAD = 3
ATTN_OUTPUT_BEHIND = 2

F32 = jnp.float32
BF16 = jnp.bfloat16


def _params(n_grid_dims):
    return pltpu.CompilerParams(
        dimension_semantics=("arbitrary",) * n_grid_dims,
        vmem_limit_bytes=VMEM_LIMIT_BYTES)


def _rmsnorm_kernel(x_ref, g_ref, o_ref):
    x = x_ref[...]
    ms = jnp.mean(x * x, axis=-1, keepdims=True)
    o_ref[...] = (x * lax.rsqrt(ms + EPS) * g_ref[...]).astype(o_ref.dtype)


def _rmsnorm_rope_kernel(x_ref, g_ref, pos_ref, invf_ref, sign_ref, o_ref, cos_ref, sin_ref):
    _rmsnorm_kernel(x_ref, g_ref, o_ref)
    ang = pos_ref[...] * invf_ref[...]
    cos_ref[...] = jnp.cos(ang)
    sin_ref[...] = jnp.sin(ang) * sign_ref[...]


def _rmsnorm(x, g, positions=None, tr=NORM_ROW_TILE):
    t, d = x.shape
    row_spec = pl.BlockSpec((tr, d), lambda i: (i, 0))
    const = lambda cols: pl.BlockSpec((1, cols), lambda i: (0, 0))
    if positions is None:
        return pl.pallas_call(
            _rmsnorm_kernel,
            grid=(t // tr,),
            in_specs=[row_spec, const(d)],
            out_specs=row_spec,
            out_shape=jax.ShapeDtypeStruct((t, d), BF16),
            compiler_params=_params(1),
            name="rmsnorm",
        )(x, g.reshape(1, d))
    half = HEAD_DIM // 2
    inv = ROPE_THETA ** (-jnp.arange(half, dtype=F32) / half)
    invf = jnp.concatenate([inv, inv]).reshape(1, HEAD_DIM)
    sign = jnp.concatenate([-jnp.ones((half,), F32), jnp.ones((half,), F32)]).reshape(1, HEAD_DIM)
    pos = positions.astype(F32).reshape(t, 1)
    table_spec = pl.BlockSpec((tr, HEAD_DIM), lambda i: (i, 0))
    return pl.pallas_call(
        _rmsnorm_rope_kernel,
        grid=(t // tr,),
        in_specs=[row_spec, const(d), pl.BlockSpec((tr, 1), lambda i: (i, 0)),
                  const(HEAD_DIM), const(HEAD_DIM)],
        out_specs=[row_spec, table_spec, table_spec],
        out_shape=[jax.ShapeDtypeStruct((t, d), BF16),
                   jax.ShapeDtypeStruct((t, HEAD_DIM), F32),
                   jax.ShapeDtypeStruct((t, HEAD_DIM), F32)],
        compiler_params=_params(1),
        name="rmsnorm_rope",
    )(x, g.reshape(1, d), pos, invf, sign)


def _ws_kernel(*refs, n_a, n_w, dots, n_extra, n_out, epilogue, n_row_tiles, n_tiles, pad, lag,
               lane_split, m_split, a_ring):
    a_refs = refs[:n_a]
    w_refs = refs[n_a:n_a + n_w]
    extra_refs = refs[n_a + n_w:n_a + n_w + n_extra]
    out_refs = refs[n_a + n_w + n_extra:n_a + n_w + n_extra + n_out]
    scratch = refs[n_a + n_w + n_extra + n_out:]
    if a_ring:
        a_bufs = scratch[-2 * n_a::2]
        a_sems = scratch[-2 * n_a + 1::2]
        scratch = scratch[:-2 * n_a]
    to_cast = [w_ref for w_ref in w_refs if w_ref.dtype != BF16]
    cast_refs = scratch[:len(to_cast)]
    n_slots = 1 + lag
    n_raw = n_slots * len(dots)
    raw_refs = scratch[len(to_cast):len(to_cast) + n_raw]
    user_scratch = scratch[len(to_cast) + n_raw:]
    s = pl.program_id(0)
    tile = jnp.minimum(s, n_tiles - 1)

    if a_ring:
        depth = a_bufs[0].shape[0]
        tm_a = a_bufs[0].shape[1]

        def a_copies(t):
            rows = pl.ds(pl.multiple_of((t % n_row_tiles) * tm_a, tm_a), tm_a)
            slot = t % depth
            return [pltpu.make_async_copy(a_ref.at[rows, :], a_buf.at[slot], a_sem.at[slot])
                    for a_ref, a_buf, a_sem in zip(a_refs, a_bufs, a_sems)]

        @pl.when(s == 0)
        def _prime():
            for t in range(min(A_RING_AHEAD, n_tiles)):
                for copy in a_copies(t):
                    copy.start()

        @pl.when(s + A_RING_AHEAD < n_tiles)
        def _prefetch():
            for copy in a_copies(s + A_RING_AHEAD):
                copy.start()

        @pl.when(s < n_tiles)
        def _arrive():
            for copy in a_copies(s):
                copy.wait()

        a_tiles = [a_buf.at[tile % depth] for a_buf in a_bufs]
    else:
        a_tiles = a_refs

    if to_cast:
        @pl.when((tile % n_row_tiles == 0) & (s < n_tiles))
        def _cast_weights():
            for w_ref, wbf_ref in zip(to_cast, cast_refs):
                wbf_ref[...] = w_ref[...].astype(BF16)

    if lag or user_scratch:
        @pl.when(s == 0)
        def _init():
            if lag:
                for raw in raw_refs[1::2]:
                    raw[...] = jnp.zeros(raw.shape, F32)
            for u in user_scratch:
                u[...] = jnp.zeros(u.shape, u.dtype)

    cast_iter = iter(cast_refs)
    wbf_refs = [w_ref if w_ref.dtype == BF16 else next(cast_iter) for w_ref in w_refs]
    done = jnp.maximum(s - lag, 0)

    def step(slot):
        tm = a_tiles[0].shape[0]
        rb = tm // m_split
        for k in range(m_split):
            rows = (k * rb, (k + 1) * rb)
            for d, (ai, wi) in enumerate(dots):
                res = jnp.dot(a_tiles[ai][rows[0]:rows[1], :], wbf_refs[wi][...],
                              preferred_element_type=F32)
                raw = raw_refs[n_slots * d + slot]
                if lane_split:
                    for lt in range(raw.shape[0]):
                        raw[lt, pad + rows[0]:pad + rows[1], :] = res[:, lt * LANES:(lt + 1) * LANES]
                else:
                    raw[pad + rows[0]:pad + rows[1], :] = res
            epilogue(done // n_row_tiles, done % n_row_tiles,
                     [raw_refs[n_slots * d + (slot + lag) % n_slots] for d in range(len(dots))],
                     extra_refs, out_refs, user_scratch, rows)

    if lag:
        for slot in range(2):
            pl.when(s % 2 == slot)(functools.partial(step, slot))
    else:
        step(0)


def _ws_matmul(name, a_list, w_list, dots, n_col_tiles, tm, tn, epilogue,
               extras=(), extra_specs=(), out_shapes=(), out_specs=(), scratch=(), pad=0, lag=0,
               lane_split=False, m_split=1, a_ring=False):
    t = a_list[0].shape[0]
    n_row_tiles = t // tm
    n_tiles = n_col_tiles * n_row_tiles

    def dot_tile(s):
        return jnp.minimum(s, n_tiles - 1)

    def lagged(index_map):
        def wrapped(s):
            done = jnp.maximum(s - lag, 0)
            return index_map(done // n_row_tiles, done % n_row_tiles)
        return wrapped

    ring_scratch = []
    if a_ring:
        in_specs = [pl.BlockSpec(memory_space=pl.ANY) for _ in a_list]
        for a in a_list:
            ring_scratch += [pltpu.VMEM((A_RING_AHEAD + 1, tm, a.shape[1]), a.dtype),
                             pltpu.SemaphoreType.DMA((A_RING_AHEAD + 1,))]
    else:
        in_specs = [pl.BlockSpec((tm, a.shape[1]), lambda s: (dot_tile(s) % n_row_tiles, 0))
                    for a in a_list]
    for w, off in w_list:
        in_specs.append(pl.BlockSpec((w.shape[0], tn), functools.partial(
            lambda s, off: (0, dot_tile(s) // n_row_tiles + off), off=off)))
    in_specs.extend(pl.BlockSpec(shape, lagged(fn)) for shape, fn in extra_specs)
    wbf_scratch = [pltpu.VMEM((w.shape[0], tn), BF16) for w, _ in w_list if w.dtype != BF16]
    raw_shape = (tn // LANES, pad + tm, LANES) if lane_split else (pad + tm, tn)
    raw_scratch = [pltpu.VMEM(raw_shape, F32) for _ in range((1 + lag) * len(dots))]
    kernel = functools.partial(
        _ws_kernel, n_a=len(a_list), n_w=len(w_list), dots=tuple(dots),
        n_extra=len(extras), n_out=len(out_shapes), epilogue=epilogue,
        n_row_tiles=n_row_tiles, n_tiles=n_tiles, pad=pad, lag=lag, lane_split=lane_split,
        m_split=m_split, a_ring=a_ring)
    return pl.pallas_call(
        kernel,
        grid=(n_tiles + lag,),
        in_specs=in_specs,
        out_specs=[pl.BlockSpec(shape, lagged(fn)) for shape, fn in out_specs],
        out_shape=list(out_shapes),
        scratch_shapes=wbf_scratch + raw_scratch + list(scratch) + ring_scratch,
        compiler_params=_params(1),
        name=name,
    )(*a_list, *[w for w, _ in w_list], *extras)


def _tile_spec(tm, tn, col_offset=0):
    return ((tm, tn), lambda j, i: (i, j + col_offset))


def _qk_epilogue(j, i, acc_refs, extra_refs, out_refs, scratch, rows, *, n_q_tiles, tm, tn):
    assert rows == (0, tm)
    cos_ref, sin_ref, g_ref = extra_refs
    o_ref, mean_ref = out_refs
    acc_ref = acc_refs[0]
    is_q = j < n_q_tiles
    g = jnp.where(is_q, g_ref[0:1, :], g_ref[1:2, :])
    scale = jnp.where(is_q, jnp.float32(HEAD_DIM ** -0.5), jnp.float32(1.0))
    cos = cos_ref[...]
    sin = sin_ref[...]
    rows_per_tile = tm // MOBA_BLOCK
    for h in range(tn // HEAD_DIM):
        cols = slice(h * HEAD_DIM, (h + 1) * HEAD_DIM)
        x = acc_ref[:, cols]
        ms = jnp.mean(x * x, axis=-1, keepdims=True)
        y = x * lax.rsqrt(ms + EPS) * g
        y = y * cos + pltpu.roll(y, HEAD_DIM // 2, 1) * sin
        o_ref[:, cols] = (y * scale).astype(o_ref.dtype)
        for r in range(rows_per_tile):
            blk = y[r * MOBA_BLOCK:(r + 1) * MOBA_BLOCK]
            mean_ref[0, r:r + 1, cols] = jnp.mean(blk, axis=0, keepdims=True)


def _store_bf16_epilogue(j, i, acc_refs, extra_refs, out_refs, scratch, rows):
    r = slice(*rows)
    out_refs[0][r, :] = acc_refs[0][r, :].astype(out_refs[0].dtype)


def _glu_epilogue(j, i, acc_refs, extra_refs, out_refs, scratch, rows):
    r = slice(*rows)
    out_refs[0][r, :] = acc_refs[0][r, :] * jax.nn.sigmoid(acc_refs[1][r, :])


def _sigmoid_epilogue(j, i, acc_refs, extra_refs, out_refs, scratch, rows):
    r = slice(*rows)
    out_refs[0][r, :] = jax.nn.sigmoid(acc_refs[0][r, :])


def _with_side_cast(epilogue):
    def wrapped(j, i, acc_refs, extra_refs, out_refs, scratch, rows):
        if rows[0] == 0:
            out_refs[-1][...] = extra_refs[-1][...].astype(out_refs[-1].dtype)
        epilogue(j, i, acc_refs, extra_refs[:-1], out_refs[:-1], scratch, rows)
    return wrapped


def _merge_epilogue(j, i, acc_refs, extra_refs, out_refs, scratch, rows):
    ga_ref, gc_ref = extra_refs
    r = slice(*rows)
    merged = ga_ref[r, :] * acc_refs[0][r, :] + gc_ref[r, :] * acc_refs[1][r, :]
    out_refs[0][r, :] = merged.astype(out_refs[0].dtype)


def _residual_epilogue(j, i, acc_refs, extra_refs, out_refs, scratch, rows):
    r = slice(*rows)
    out_refs[0][r, :] = extra_refs[0][r, :] + acc_refs[0][r, :]


def _ffn_in_epilogue(j, i, acc_refs, extra_refs, out_refs, scratch, rows, *, tm, tiles_per_seq,
                     width):
    cwg_ref, cwu_ref, cbg_ref, cbu_ref = extra_refs
    pad = SUBLANES
    seq_start = i % tiles_per_seq == 0
    if rows[0] == 0:
        for hp, carry in zip(acc_refs, scratch):
            prev = carry[...]
            hp[:, 0:pad, :] = jnp.where(seq_start, jnp.zeros_like(prev), prev)
    for lt in range(acc_refs[0].shape[0]):
        lanes = slice(lt * LANES, (lt + 1) * LANES)
        for r in range(rows[0], rows[1], FFN_ROW_CHUNK):
            convs = []
            for hp, cw_ref, cb_ref in zip(acc_refs, (cwg_ref, cwu_ref), (cbg_ref, cbu_ref)):
                y = cb_ref[:, lanes]
                for w in range(width):
                    lo = pad + r - (width - 1 - w)
                    y = y + cw_ref[w:w + 1, lanes] * hp[lt, lo:lo + FFN_ROW_CHUNK, :]
                convs.append(y)
            gate, up = convs
            out_refs[0][r:r + FFN_ROW_CHUNK, lanes] = (
                jax.nn.silu(gate) * up).astype(out_refs[0].dtype)
    if rows[1] == tm:
        for hp, carry in zip(acc_refs, scratch):
            carry[...] = hp[:, tm:tm + pad, :]


def _attn_kernel(q_ref, k_ref, v_ref, km_ref, o_ref, *, n_blocks):
    L = MOBA_BLOCK
    nt = (((1,), (1,)), ((), ()))
    tn = (((0,), (0,)), ((), ()))
    km = km_ref[...].astype(BF16)
    neg = jnp.float32(-jnp.inf)
    key = lax.broadcasted_iota(jnp.int32, (L, L), 0)
    qry = lax.broadcasted_iota(jnp.int32, (L, L), 1)
    causal = key <= qry
    blk = lax.broadcasted_iota(jnp.int32, (n_blocks, L), 0)

    def scores(i):
        qi = q_ref[i * L:(i + 1) * L, :]
        n_keys = (i + 1) * L
        st = lax.dot_general(k_ref[0:n_keys, :], qi, nt, preferred_element_type=F32)
        gate = None
        if i > MOBA_TOP_K:
            gate = lax.dot_general(km, qi, nt, preferred_element_type=F32)
        return st, gate

    def probabilities(i, st, gate):
        pen = None
        if gate is not None:
            rank = jnp.zeros((n_blocks, L), F32)
            for jp in range(i):
                gb = jnp.broadcast_to(gate[jp:jp + 1, :], (n_blocks, L))
                beats = (gb > gate) | ((gb == gate) & (jp < blk))
                rank = rank + jnp.where(beats, 1.0, 0.0)
            pen = jnp.where(rank < MOBA_TOP_K, 0.0, neg)
        chunks = []
        for jb in range(i + 1):
            sj = st[jb * L:(jb + 1) * L, :]
            if jb == i:
                sj = jnp.where(causal, sj, neg)
            elif pen is not None:
                sj = sj + pen[jb:jb + 1, :]
            chunks.append(sj)
        top = chunks[0]
        for c in chunks[1:]:
            top = jnp.maximum(top, c)
        m = top.max(axis=0, keepdims=True)
        es = [jnp.exp(c - m) for c in chunks]
        tot = es[0]
        for e in es[1:]:
            tot = tot + e
        inv_l = 1.0 / tot.sum(axis=0, keepdims=True)
        return jnp.concatenate([e.astype(BF16) for e in es], axis=0), inv_l

    def output(i, weights):
        et, inv_l = weights
        n_keys = (i + 1) * L
        ot = lax.dot_general(v_ref[0:n_keys, :], et, tn, preferred_element_type=F32)
        o_ref[i * L:(i + 1) * L, :] = (ot * inv_l).T.astype(o_ref.dtype)

    st = {}
    pt = {}
    for i in range(min(ATTN_SCORES_AHEAD, n_blocks)):
        st[i] = scores(i)
    for i in range(n_blocks):
        pt[i] = probabilities(i, *st.pop(i))
        if i + ATTN_SCORES_AHEAD < n_blocks:
            st[i + ATTN_SCORES_AHEAD] = scores(i + ATTN_SCORES_AHEAD)
        if i >= ATTN_OUTPUT_BEHIND:
            output(i - ATTN_OUTPUT_BEHIND, pt.pop(i - ATTN_OUTPUT_BEHIND))
    for i in sorted(pt):
        output(i, pt[i])


def _moba_attention(qk, v, kmean, batch, seq, n_heads):
    t = qk.shape[0]
    n_blocks = seq // MOBA_BLOCK
    return pl.pallas_call(
        functools.partial(_attn_kernel, n_blocks=n_blocks),
        grid=(batch, n_heads),
        in_specs=[pl.BlockSpec((seq, HEAD_DIM), lambda b, h: (b, h)),
                  pl.BlockSpec((seq, HEAD_DIM), lambda b, h: (b, n_heads + h)),
                  pl.BlockSpec((seq, HEAD_DIM), lambda b, h: (b, h)),
                  pl.BlockSpec((n_blocks, HEAD_DIM), lambda b, h: (b, n_heads + h))],
        out_specs=pl.BlockSpec((seq, HEAD_DIM), lambda b, h: (b, h)),
        out_shape=jax.ShapeDtypeStruct((t, n_heads * HEAD_DIM), BF16),
        compiler_params=_params(2),
        name="moba_attention",
    )(qk, qk, v, kmean)


def _conv_kernel(h_ref, halo_ref, cw_ref, cb_ref, g_ref, b_ref, o_ref, xpad_ref, y_ref, z_ref,
                 *, ts, halo, width):
    t = pl.program_id(1)
    prev = halo_ref[...]
    xpad_ref[0:halo, :] = jnp.where(t == 0, jnp.zeros_like(prev), prev)
    xpad_ref[halo:, :] = h_ref[...]
    n_chunks = h_ref.shape[1] // LANES
    base = halo - (width - 1)

    def chunk(c, carry):
        cols = pl.ds(pl.multiple_of(c * LANES, LANES), LANES)
        groups = []
        for r in range(SUBLANES):
            taps = [w for w in range(width) if (base + w) % SUBLANES == r]
            if taps:
                span = taps[-1] - taps[0] + ts
                z_ref[r, 0:span, :] = xpad_ref[base + taps[0]:base + taps[0] + span, cols]
                groups.append((r, taps))
        for r0 in range(0, ts, CONV_ROW_GROUP):
            acc = jnp.broadcast_to(cb_ref[:, cols], (CONV_ROW_GROUP, LANES))
            for r, taps in groups:
                for w in taps:
                    off = r0 + w - taps[0]
                    acc = acc + cw_ref[w:w + 1, cols] * z_ref[r, off:off + CONV_ROW_GROUP, :]
            y_ref[r0:r0 + CONV_ROW_GROUP, cols] = acc
        return carry

    lax.fori_loop(0, n_chunks, chunk, 0)
    y = y_ref[...]
    mu = jnp.mean(y, axis=-1, keepdims=True)
    yc = y - mu
    var = jnp.mean(yc * yc, axis=-1, keepdims=True)
    z = yc * lax.rsqrt(var + EPS) * g_ref[...] + b_ref[...]
    o_ref[...] = jax.nn.silu(z).astype(o_ref.dtype)


def _conv_module(h, conv_w, conv_b, ln_g, ln_b, batch, seq, ts=256, halo=32):
    t, c = h.shape
    width = conv_w.shape[0]
    tiles = seq // ts
    halo_per_tile = ts // halo

    def halo_map(b, s):
        return (jnp.maximum((b * tiles + s) * halo_per_tile - 1, 0), 0)

    vec = lambda: pl.BlockSpec((1, c), lambda b, s: (0, 0))
    return pl.pallas_call(
        functools.partial(_conv_kernel, ts=ts, halo=halo, width=width),
        grid=(batch, tiles),
        in_specs=[pl.BlockSpec((ts, c), lambda b, s: (b * tiles + s, 0)),
                  pl.BlockSpec((halo, c), halo_map),
                  pl.BlockSpec((width, c), lambda b, s: (0, 0)),
                  vec(), vec(), vec()],
        out_specs=pl.BlockSpec((ts, c), lambda b, s: (b * tiles + s, 0)),
        out_shape=jax.ShapeDtypeStruct((t, c), BF16),
        scratch_shapes=[pltpu.VMEM((halo + ts, c), F32), pltpu.VMEM((ts, c), F32),
                        pltpu.VMEM((SUBLANES, ts + halo, LANES), F32)],
        compiler_params=_params(2),
        name="conv_module",
    )(h, h, conv_w, conv_b.reshape(1, c), ln_g.reshape(1, c), ln_b.reshape(1, c))


def _layer(x, positions, norm1_g, w_in, q_norm_g, k_norm_g, w_o_attn, conv_w, conv_b,
           conv_ln_g, conv_ln_b, w_o_conv, w_out, norm2_g, w_ffn_in, ffn_conv_w,
           ffn_conv_b, w_ffn_out):
    batch, seq, d_model = x.shape
    t = batch * seq
    attn_width = w_o_attn.shape[0]
    conv_dim = w_o_conv.shape[0]
    d_ff = w_ffn_out.shape[0]
    n_heads = attn_width // HEAD_DIM
    assert seq % MOBA_BLOCK == 0
    assert w_in.shape[1] == 3 * attn_width + 2 * conv_dim + 2 * d_model

    x2 = x.reshape(t, d_model)
    xn, cos, sin = _rmsnorm(x2, norm1_g, positions)

    tm, tn = ROW_TILE, COL_TILE
    qk_g = jnp.stack([q_norm_g, k_norm_g])
    n_q_tiles = attn_width // tn
    qk, kmean = _ws_matmul(
        "qk_proj", [xn], [(w_in, 0)], [(0, 0)], 2 * n_q_tiles, tm, tn,
        functools.partial(_qk_epilogue, n_q_tiles=n_q_tiles, tm=tm, tn=tn), lag=1, a_ring=True,
        extras=(cos, sin, qk_g),
        extra_specs=(((tm, HEAD_DIM), lambda j, i: (i, 0)),
                     ((tm, HEAD_DIM), lambda j, i: (i, 0)),
                     ((2, HEAD_DIM), lambda j, i: (0, 0))),
        out_shapes=(jax.ShapeDtypeStruct((t, 2 * attn_width), BF16),
                    jax.ShapeDtypeStruct((t // tm, tm // MOBA_BLOCK, 2 * attn_width), F32)),
        out_specs=(_tile_spec(tm, tn),
                   ((1, tm // MOBA_BLOCK, tn), lambda j, i: (i, 0, j))))
    kmean = kmean.reshape(t // MOBA_BLOCK, 2 * attn_width)
    (v,) = _ws_matmul(
        "v_proj", [xn], [(w_in, 2 * attn_width // tn)], [(0, 0)], attn_width // tn, tm, tn,
        _store_bf16_epilogue, a_ring=True,
        out_shapes=(jax.ShapeDtypeStruct((t, attn_width), BF16),),
        out_specs=(_tile_spec(tm, tn),))
    tg = PAIR_COL_TILE
    u_off = 3 * attn_width
    (h,) = _ws_matmul(
        "glu_proj", [xn], [(w_in, u_off // tg), (w_in, (u_off + conv_dim) // tg)],
        [(0, 0), (0, 1)], conv_dim // tg, tm, tg, _glu_epilogue, a_ring=True,
        out_shapes=(jax.ShapeDtypeStruct((t, conv_dim), F32),),
        out_specs=(_tile_spec(tm, tg),))
    g_off = u_off + 2 * conv_dim
    (gates,) = _ws_matmul(
        "gate_proj", [xn], [(w_in, g_off // tn)], [(0, 0)], 2 * d_model // tn, tm, tn,
        _sigmoid_epilogue, lag=1, m_split=tm // MXU_ROW_BLOCK, a_ring=True,
        out_shapes=(jax.ShapeDtypeStruct((t, 2 * d_model), F32),),
        out_specs=(_tile_spec(tm, tn),))

    attn = _moba_attention(qk, v, kmean, batch, seq, n_heads)
    hc = _conv_module(h, conv_w, conv_b, conv_ln_g, conv_ln_b, batch, seq)

    n_model_tiles = d_model // tn
    tmo = FFN_OUT_ROW_TILE
    (merged,) = _ws_matmul(
        "merge_proj", [attn, hc], [(w_o_attn, 0), (w_o_conv, 0)], [(0, 0), (1, 1)],
        n_model_tiles, tm, tn, _merge_epilogue, a_ring=True,
        extras=(gates, gates),
        extra_specs=(_tile_spec(tm, tn), _tile_spec(tm, tn, n_model_tiles)),
        out_shapes=(jax.ShapeDtypeStruct((t, d_model), BF16),),
        out_specs=(_tile_spec(tm, tn),))
    (x1,) = _ws_matmul(
        "out_proj", [merged], [(w_out, 0)], [(0, 0)], n_model_tiles, tm, tn,
        _residual_epilogue, a_ring=True,
        extras=(x2,), extra_specs=(_tile_spec(tm, tn),),
        out_shapes=(jax.ShapeDtypeStruct((t, d_model), F32),),
        out_specs=(_tile_spec(tm, tn),))

    xn2 = _rmsnorm(x1, norm2_g)
    tf = PAIR_COL_TILE
    assert d_ff % tf == 0 and seq % tm == 0
    n_ff_tiles = d_ff // tf
    fcw = ffn_conv_w
    fcb = ffn_conv_b.reshape(1, 2 * d_ff)
    width = fcw.shape[0]
    n_ff_steps = n_ff_tiles * (t // tm)
    assert d_ff % (n_ff_steps * SUBLANES) == 0
    slab = ((d_ff // n_ff_steps, d_model), lambda j, i: (j * (t // tm) + i, 0))
    act, w_ffn_out_bf = _ws_matmul(
        "ffn_in", [xn2], [(w_ffn_in, 0), (w_ffn_in, n_ff_tiles)], [(0, 0), (0, 1)],
        n_ff_tiles, tm, tf,
        _with_side_cast(functools.partial(
            _ffn_in_epilogue, tm=tm, tiles_per_seq=seq // tm, width=width)),
        extras=(fcw, fcw, fcb, fcb, w_ffn_out),
        extra_specs=(((width, tf), lambda j, i: (0, j)),
                     ((width, tf), lambda j, i: (0, j + n_ff_tiles)),
                     ((1, tf), lambda j, i: (0, j)),
                     ((1, tf), lambda j, i: (0, j + n_ff_tiles)),
                     slab),
        out_shapes=(jax.ShapeDtypeStruct((t, d_ff), BF16),
                    jax.ShapeDtypeStruct((d_ff, d_model), BF16)),
        out_specs=(_tile_spec(tm, tf), slab),
        scratch=[pltpu.VMEM((tf // LANES, SUBLANES, LANES), F32)] * 2,
        pad=SUBLANES, lane_split=True, lag=1, m_split=tm // MXU_ROW_BLOCK, a_ring=True)

    (out,) = _ws_matmul(
        "ffn_out", [act], [(w_ffn_out_bf, 0)], [(0, 0)], n_model_tiles, tmo, tn,
        _residual_epilogue,
        extras=(x1,), extra_specs=(_tile_spec(tmo, tn),),
        out_shapes=(jax.ShapeDtypeStruct((t, d_model), F32),),
        out_specs=(_tile_spec(tmo, tn),))
    return out.reshape(batch, seq, d_model)


def kernel(x, positions, norm1_g, w_in, q_norm_g, k_norm_g, w_o_attn, conv_w, conv_b,
           conv_ln_g, conv_ln_b, w_o_conv, w_out, norm2_g, w_ffn_in, ffn_conv_w,
           ffn_conv_b, w_ffn_out):
    depth = norm1_g.shape[0]
    params = (norm1_g, w_in, q_norm_g, k_norm_g, w_o_attn, conv_w, conv_b, conv_ln_g,
              conv_ln_b, w_o_conv, w_out, norm2_g, w_ffn_in, ffn_conv_w, ffn_conv_b, w_ffn_out)
    for l in range(depth):
        layer = [p.reshape(p.shape[1:]) if depth == 1 else p[l] for p in params]
        x = _layer(x, positions, *layer)
    return x
```
